```python
import math
import jax, jax.numpy as jnp
from jax import lax
import numpy as np

D_MODEL = 1024
BATCH = 32
SEQ = 2048
DEPTH = 2
DEC_BATCH = 16
DEC_SEQ = 32
PAST_LEN = 4096

CHUNK = 64
N_EVEN = (DEPTH + 1) // 2
N_ODD = DEPTH // 2
EPS = 1e-6
D_A = D_MODEL // 2
G_A = 4
DG_A = D_A // G_A
SGU_CHUNK = 128
D_B = D_MODEL // 2
K_B = 31
H_C = 12
HD_C = 64
D_C = H_C * HD_C
H_I = 8
D_I = 64
TOPK_MAX = 256
QBLOCK = 128
N_BUCKETS = 32
MAX_DIST = 128
D_D = D_MODEL - D_C
POOL_WINDOWS = (2, 4, 8, 16)
G_D = 4
DG_D = D_D // G_D
POOL_HIST = 15
D_FF = 2816
K_FFN = 3
EVEN_IN = 2 * D_A + 2 * D_B
ODD_IN = D_C + 2 * HD_C + H_I * D_I + D_I + H_I + D_D

kernel_name = "hybrid_streaming_encoder_step"


def rms_norm(x, g):
    xf = x.astype(jnp.float32)
    y = xf * lax.rsqrt(jnp.mean(xf * xf, axis=-1, keepdims=True) + EPS)
    return (y * g.astype(jnp.float32)).astype(x.dtype)


def layer_norm(x, g, b):
    xf = x.astype(jnp.float32)
    mu = jnp.mean(xf, axis=-1, keepdims=True)
    xc = xf - mu
    y = xc * lax.rsqrt(jnp.mean(xc * xc, axis=-1, keepdims=True) + EPS)
    return (y * g.astype(jnp.float32) + b.astype(jnp.float32)).astype(x.dtype)


def causal_dwconv(xx, w, b):
    out = lax.conv_general_dilated(xx, w[:, None, :], window_strides=(1,), padding='VALID',
                                   dimension_numbers=('NWC', 'WIO', 'NWC'),
                                   feature_group_count=xx.shape[-1])
    return out + b


def spatial_gating(v, w_s, b_s):
    B, T = v.shape[:2]
    n = min(T, SGU_CHUNK)
    nc = T // n
    mask = jnp.tril(jnp.ones((n, n), dtype=bool))
    ws = jnp.where(mask[None], w_s[:, :n, :n], 0.0).astype(v.dtype)
    vc = v.reshape(B, nc, n, G_A, DG_A)
    out = jnp.einsum('gij,bcjgd->bcigd', ws, vc) + b_s[:, :n].T[:, :, None]
    return out.reshape(B, T, G_A, DG_A)


def t5_bucket(rel):
    nb = N_BUCKETS // 2
    exact = nb // 2
    side = jnp.where(rel > 0, nb, 0)
    n = jnp.abs(rel)
    large = exact + (jnp.log(jnp.maximum(n, 1).astype(jnp.float32) / exact)
                     / math.log(MAX_DIST / exact) * (nb - exact)).astype(jnp.int32)
    large = jnp.minimum(large, nb - 1)
    return side + jnp.where(n < exact, n, large)


def dsa_attention(q, q_idx, w_idx, keys, vals, k_idx, qpos, kpos, rel_bias):
    B, T = q.shape[:2]
    S = keys.shape[1]
    top = min(TOPK_MAX, S // 4)
    qb = min(QBLOCK, T)
    nb = T // qb
    scale = HD_C ** -0.5

    def block(args):
        qi, qii, wi, pi = args
        s = jnp.einsum('bqhd,bsd->bqhs', qii, k_idx).astype(jnp.float32) * (D_I ** -0.5)
        score = jnp.einsum('bqhs,bqh->bqs', jax.nn.relu(s), wi.astype(jnp.float32) * (H_I ** -0.5))
        admissible = (kpos[None, :] // CHUNK) <= (pi[:, None] // CHUNK)
        score = jnp.where(admissible[None], score, -jnp.inf)
        _, idx = lax.top_k(score, top)
        kg = jax.vmap(lambda a, i: a[i])(keys, idx)
        vg = jax.vmap(lambda a, i: a[i])(vals, idx)
        kp = kpos[idx]
        valid = (kp // CHUNK) <= (pi[None, :, None] // CHUNK)
        bias = rel_bias[t5_bucket(kp - pi[None, :, None])].astype(jnp.float32)
        logits = jnp.einsum('bqhd,bqkd->bqhk', qi, kg).astype(jnp.float32) * scale \
            + jnp.transpose(bias, (0, 1, 3, 2))
        logits = jnp.where(valid[:, :, None, :], logits, -jnp.inf)
        p = jax.nn.softmax(logits, axis=-1)
        return jnp.einsum('bqhk,bqkd->bqhd', p.astype(vg.dtype), vg)

    qs = q.reshape(B, nb, qb, H_C, HD_C).swapaxes(0, 1)
    qis = q_idx.reshape(B, nb, qb, H_I, D_I).swapaxes(0, 1)
    ws = w_idx.reshape(B, nb, qb, H_I).swapaxes(0, 1)
    out = lax.map(block, (qs, qis, ws, qpos.reshape(nb, qb)))
    return out.swapaxes(0, 1).reshape(B, T, D_C)


def pool_mixer(xd, hist, n_valid_hist, w_pool, s_pool):
    B, T, _ = xd.shape
    P = POOL_HIST
    xx = jnp.concatenate([hist, xd], axis=1)
    cs = jnp.concatenate([jnp.zeros((B, 1, D_D), jnp.float32),
                          jnp.cumsum(xx.astype(jnp.float32), axis=1)], axis=1)
    j = jnp.arange(T)
    parts = []
    for g, w in enumerate(POOL_WINDOWS):
        lo, hi = g * DG_D, (g + 1) * DG_D
        win_sum = cs[:, P + 1:P + 1 + T, lo:hi] - cs[:, P + 1 - w:P + 1 - w + T, lo:hi]
        count = jnp.minimum(n_valid_hist + j + 1, w).astype(jnp.float32)
        parts.append(win_sum / count[None, :, None] - xd[:, :, lo:hi].astype(jnp.float32))
    m = jnp.stack(parts, axis=2).astype(xd.dtype)
    y = jnp.einsum('btgc,gcd->btgd', m, w_pool).reshape(B, T, D_D) * s_pool
    return y, xx[:, -P:]


def even_layer(x, conv_hist, g_norm, w_in, ln_v_g, ln_v_b, w_s, b_s, w_dw, b_dw, ln_c_g, ln_c_b, w_out):
    B, T, _ = x.shape
    z = rms_norm(x, g_norm) @ w_in
    z_a, z_b = jnp.split(z, [2 * D_A], axis=-1)
    u, v = jnp.split(jax.nn.gelu(z_a), 2, axis=-1)
    v = layer_norm(v.reshape(B, T, G_A, DG_A), ln_v_g.reshape(G_A, DG_A), ln_v_b.reshape(G_A, DG_A))
    y_a = u * spatial_gating(v, w_s, b_s).reshape(B, T, D_A)
    a, gate = jnp.split(z_b, 2, axis=-1)
    xx = jnp.concatenate([conv_hist, a * jax.nn.sigmoid(gate)], axis=1)
    y_b = jax.nn.silu(layer_norm(causal_dwconv(xx, w_dw, b_dw), ln_c_g, ln_c_b))
    y = jnp.concatenate([y_a, y_b], axis=-1) @ w_out
    return x + y, v.reshape(B, T, D_A), xx[:, -(K_B - 1):]


def odd_layer(x, k_hist, v_hist, ki_hist, pool_hist, n_valid_hist, g_norm, w_in, w_pool, s_pool, w_out, rel_bias):
    B, T, _ = x.shape
    P = k_hist.shape[1]
    z = rms_norm(x, g_norm) @ w_in
    offs = np.cumsum([D_C, HD_C, HD_C, H_I * D_I, D_I, H_I]).tolist()
    q, k, v, qi, ki, wi, xd = jnp.split(z, offs, axis=-1)
    keys = jnp.concatenate([k_hist, k], axis=1)
    vals = jnp.concatenate([v_hist, v], axis=1)
    kidx = jnp.concatenate([ki_hist, ki], axis=1)
    qpos = P + jnp.arange(T)
    kpos = jnp.arange(P + T)
    y_c = dsa_attention(q.reshape(B, T, H_C, HD_C), qi.reshape(B, T, H_I, D_I), wi,
                        keys, vals, kidx, qpos, kpos, rel_bias)
    y_d, pool_new = pool_mixer(xd, pool_hist, n_valid_hist, w_pool, s_pool)
    y = jnp.concatenate([y_c, y_d], axis=-1) @ w_out
    return x + y, k, v, ki, pool_new


def conv_ffn(x, hist, g_norm, w_up, w_dw, b_dw, w_down):
    h = rms_norm(x, g_norm) @ w_up
    a, val = jnp.split(h, 2, axis=-1)
    aa = jnp.concatenate([hist, a], axis=1)
    y = (jax.nn.gelu(causal_dwconv(aa, w_dw, b_dw)) * val) @ w_down
    return x + y, aa[:, -(K_FFN - 1):]


def setup_inputs(seed: int = 0) -> dict:
    key = jax.random.key(seed)
    ks = iter(jax.random.split(key, 40))

    def nrm(shape, scale=1.0):
        return jax.random.normal(next(ks), shape, jnp.float32) * scale

    def gain(shape):
        return 1.0 + nrm(shape, 0.1)

    return {
        "x_prompt": nrm((BATCH, SEQ, D_MODEL)),
        "x_sample": nrm((DEC_BATCH, DEC_SEQ, D_MODEL)),
        "cache_b_conv": nrm((N_EVEN, DEC_BATCH, K_B - 1, D_B)),
        "cache_c_k": nrm((N_ODD, DEC_BATCH, PAST_LEN, HD_C)),
        "cache_c_v": nrm((N_ODD, DEC_BATCH, PAST_LEN, HD_C)),
        "cache_c_kidx": nrm((N_ODD, DEC_BATCH, PAST_LEN, D_I)),
        "cache_d_pool": nrm((N_ODD, DEC_BATCH, POOL_HIST, D_D)),
        "cache_ffn_conv": nrm((DEPTH, DEC_BATCH, K_FFN - 1, D_FF)),
        "ln_mix": gain((DEPTH, D_MODEL)),
        "ln_ffn": gain((DEPTH, D_MODEL)),
        "ln_final": gain((D_MODEL,)),
        "e_w_in": nrm((N_EVEN, D_MODEL, EVEN_IN), D_MODEL ** -0.5),
        "e_ln_v_g": gain((N_EVEN, D_A)),
        "e_ln_v_b": nrm((N_EVEN, D_A), 0.1),
        "e_w_s": nrm((N_EVEN, G_A, SGU_CHUNK, SGU_CHUNK), SGU_CHUNK ** -0.5),
        "e_b_s": gain((N_EVEN, G_A, SGU_CHUNK)),
        "e_w_dw": nrm((N_EVEN, K_B, D_B), K_B ** -0.5),
        "e_b_dw": nrm((N_EVEN, D_B), 0.1),
        "e_ln_c_g": gain((N_EVEN, D_B)),
        "e_ln_c_b": nrm((N_EVEN, D_B), 0.1),
        "e_w_out": nrm((N_EVEN, D_A + D_B, D_MODEL), (D_A + D_B) ** -0.5),
        "o_w_in": nrm((N_ODD, D_MODEL, ODD_IN), D_MODEL ** -0.5),
        "o_w_pool": nrm((N_ODD, G_D, DG_D, DG_D), DG_D ** -0.5),
        "o_s_pool": gain((N_ODD, D_D)),
        "o_w_out": nrm((N_ODD, D_C + D_D, D_MODEL), (D_C + D_D) ** -0.5),
        "rel_bias": nrm((N_BUCKETS, H_C), 0.5),
        "f_w_up": nrm((DEPTH, D_MODEL, 2 * D_FF), D_MODEL ** -0.5),
        "f_w_dw": nrm((DEPTH, K_FFN, D_FF), K_FFN ** -0.5),
        "f_b_dw": nrm((DEPTH, D_FF), 0.1),
        "f_w_down": nrm((DEPTH, D_FF, D_MODEL), D_FF ** -0.5),
    }


def reference(x_prompt, x_sample, cache_b_conv, cache_c_k, cache_c_v, cache_c_kidx, cache_d_pool, cache_ffn_conv,
              ln_mix, ln_ffn, ln_final, e_w_in, e_ln_v_g, e_ln_v_b, e_w_s, e_b_s, e_w_dw, e_b_dw, e_ln_c_g, e_ln_c_b,
              e_w_out, o_w_in, o_w_pool, o_s_pool, o_w_out, rel_bias, f_w_up, f_w_dw, f_b_dw, f_w_down):
    hp, hs = x_prompt, x_sample
    Bp, Bs = x_prompt.shape[0], x_sample.shape[0]
    dt = x_prompt.dtype
    a_s_l, b_p_l, b_s_l = [], [], []
    ck_p_l, cv_p_l, cki_p_l, ck_s_l, cv_s_l, cki_s_l, d_p_l, d_s_l = [], [], [], [], [], [], [], []
    f_p_l, f_s_l = [], []
    for layer in range(DEPTH):
        i = layer // 2
        if layer % 2 == 0:
            ew = (ln_mix[layer], e_w_in[i], e_ln_v_g[i], e_ln_v_b[i], e_w_s[i], e_b_s[i], e_w_dw[i], e_b_dw[i],
                  e_ln_c_g[i], e_ln_c_b[i], e_w_out[i])
            hp, _, b_p = even_layer(hp, jnp.zeros((Bp, K_B - 1, D_B), dt), *ew)
            hs, a_s, b_s = even_layer(hs, cache_b_conv[i], *ew)
            a_s_l.append(a_s); b_p_l.append(b_p); b_s_l.append(b_s)
        else:
            ow = (ln_mix[layer], o_w_in[i], o_w_pool[i], o_s_pool[i], o_w_out[i], rel_bias)
            hp, k_p, v_p, ki_p, d_p = odd_layer(hp, jnp.zeros((Bp, 0, HD_C), dt), jnp.zeros((Bp, 0, HD_C), dt),
                                                jnp.zeros((Bp, 0, D_I), dt), jnp.zeros((Bp, POOL_HIST, D_D), dt),
                                                0, *ow)
            hs, k_s, v_s, ki_s, d_s = odd_layer(hs, cache_c_k[i], cache_c_v[i], cache_c_kidx[i], cache_d_pool[i],
                                                POOL_HIST, *ow)
            ck_p_l.append(k_p); cv_p_l.append(v_p); cki_p_l.append(ki_p)
            ck_s_l.append(k_s); cv_s_l.append(v_s); cki_s_l.append(ki_s)
            d_p_l.append(d_p); d_s_l.append(d_s)
        fw = (ln_ffn[layer], f_w_up[layer], f_w_dw[layer], f_b_dw[layer], f_w_down[layer])
        hp, f_p = conv_ffn(hp, jnp.zeros((Bp, K_FFN - 1, D_FF), dt), *fw)
        hs, f_s = conv_ffn(hs, cache_ffn_conv[layer], *fw)
        f_p_l.append(f_p); f_s_l.append(f_s)
    y_prompt = rms_norm(hp, ln_final)
    y_sample = rms_norm(hs, ln_final)
    return (y_prompt, y_sample,
            jnp.stack(a_s_l), jnp.stack(b_p_l), jnp.stack(b_s_l),
            jnp.stack(ck_p_l), jnp.stack(cv_p_l), jnp.stack(cki_p_l),
            jnp.stack(ck_s_l), jnp.stack(cv_s_l), jnp.stack(cki_s_l),
            jnp.stack(d_p_l), jnp.stack(d_s_l),
            jnp.stack(f_p_l), jnp.stack(f_s_l))
```

```python
import functools
import math

import jax
import jax.numpy as jnp
import numpy as np
from jax import lax
from jax.experimental import pallas as pl
from jax.experimental.pallas import tpu as pltpu

F32 = jnp.float32
BF16 = jnp.bfloat16

EPS = 1e-6
CHUNK = 64
CHUNK_SHIFT = CHUNK.bit_length() - 1
SGU_CHUNK = 128
G_A = 4
K_B = 31
H_C = 12
HD_C = 64
H_I = 8
D_I = 64
TOPK_MAX = 256
N_BUCKETS = 32
MAX_DIST = 128
POOL_WINDOWS = (2, 4, 8, 16)
POOL_HIST = 15
K_FFN = 3

V7X_LANES = 128
V7X_SUBLANES = 8
V7X_MXU_DIM = 256
V7X_VMEM_LIMIT_BYTES = 56 * 1024 * 1024

KEY_TILE = V7X_MXU_DIM
PROMPT_ROWS = 512
FFN_CHUNKS = 2
NEG_INF = float("-inf")


def _rms(x, g):
    return x * lax.rsqrt(jnp.mean(x * x, axis=-1, keepdims=True) + EPS) * g


def _layer_norm(x, g, b):
    mu = jnp.mean(x, axis=-1, keepdims=True)
    xc = x - mu
    return xc * lax.rsqrt(jnp.mean(xc * xc, axis=-1, keepdims=True) + EPS) * g + b


def _gelu_tanh(x):
    cdf = 0.5 * (1.0 + jnp.tanh(math.sqrt(2.0 / math.pi) * (x + 0.044715 * (x * x * x))))
    return x * cdf


def _dot(a, b):
    return jnp.dot(a, b, preferred_element_type=F32)


def _dot_nt(a, b):
    return lax.dot_general(a, b, (((1,), (1,)), ((), ())), preferred_element_type=F32)


def _const_spec(shape):
    nd = len(shape)
    return pl.BlockSpec(shape, lambda *_: (0,) * nd, pipeline_mode=pl.Buffered(1))


def _ffn_kernel(*refs, n_streams, rows, n_chunks, final_norm):
    if final_norm:
        (x_ref, hist_ref, g_ref, wa_ref, wv_ref, wd_ref, dw_ref, bdw_ref, gf_ref,
         y_ref, nh_ref, carry_ref) = refs
    else:
        (x_ref, hist_ref, g_ref, wa_ref, wv_ref, wd_ref, dw_ref, bdw_ref,
         y_ref, nh_ref, carry_ref) = refs
        gf_ref = None
    d_model = x_ref.shape[-1]
    fc = wa_ref.shape[-1]
    m = n_streams * rows

    @pl.when(pl.program_id(1) == 0)
    def _():
        for c in range(n_chunks):
            carry_ref[:, c, 6:8, :] = hist_ref[:, :, c * fc:(c + 1) * fc]

    x = x_ref[...].reshape(m, d_model)
    xn = _rms(x, g_ref[...]).astype(BF16)
    acc = jnp.zeros((m, d_model), F32)
    for c in range(n_chunks):
        a = _dot(xn, wa_ref[c])
        val = _dot(xn, wv_ref[c])
        w = dw_ref[c]
        ys = []
        for s in range(n_streams):
            a_s = a[s * rows:(s + 1) * rows]
            ext = jnp.concatenate([carry_ref[s, c], a_s], axis=0)
            ys.append(w[0:1] * ext[6:6 + rows] + w[1:2] * ext[7:7 + rows] + w[2:3] * a_s)
            carry_ref[s, c] = a_s[rows - 8:rows]
            nh_ref[s, :, c * fc:(c + 1) * fc] = a_s[rows - 2:rows]
        y = (ys[0] if n_streams == 1 else jnp.concatenate(ys, axis=0)) + bdw_ref[c]
        acc = acc + _dot((_gelu_tanh(y) * val).astype(BF16), wd_ref[c])
    out = x + acc
    if final_norm:
        out = _rms(out, gf_ref[...])
    y_ref[...] = out.reshape(y_ref.shape)


def _conv_ffn(x, hist, g_norm, w_up, w_dw, b_dw, w_down, g_final, *, n_streams, rows, n_chunks):
    b, t, d = x.shape
    d_ff = w_down.shape[0]
    fc = d_ff // n_chunks
    assert fc * n_chunks == d_ff and fc % V7X_LANES == 0
    assert b % n_streams == 0 and t % rows == 0 and rows % V7X_SUBLANES == 0
    wa = w_up[:, :d_ff].astype(BF16).reshape(d, n_chunks, fc).transpose(1, 0, 2)
    wv = w_up[:, d_ff:].astype(BF16).reshape(d, n_chunks, fc).transpose(1, 0, 2)
    wd = w_down.astype(BF16).reshape(n_chunks, fc, d)
    dw = w_dw.reshape(K_FFN, n_chunks, fc).transpose(1, 0, 2)
    bdw = b_dw.reshape(n_chunks, 1, fc)
    final_norm = g_final is not None
    args = [x, hist, g_norm.reshape(1, d), wa, wv, wd, dw, bdw]
    in_specs = [
        pl.BlockSpec((n_streams, rows, d), lambda i, j: (i, j, 0)),
        pl.BlockSpec((n_streams, K_FFN - 1, d_ff), lambda i, j: (i, 0, 0)),
        _const_spec((1, d)),
        _const_spec(wa.shape), _const_spec(wv.shape), _const_spec(wd.shape),
        _const_spec(dw.shape), _const_spec(bdw.shape),
    ]
    if final_norm:
        args.append(g_final.reshape(1, d))
        in_specs.append(_const_spec((1, d)))
    kern = functools.partial(_ffn_kernel, n_streams=n_streams, rows=rows, n_chunks=n_chunks,
                             final_norm=final_norm)
    return pl.pallas_call(
        kern,
        grid=(b // n_streams, t // rows),
        in_specs=in_specs,
        out_specs=[
            pl.BlockSpec((n_streams, rows, d), lambda i, j: (i, j, 0)),
            pl.BlockSpec((n_streams, K_FFN - 1, d_ff), lambda i, j: (i, 0, 0)),
        ],
        out_shape=[jax.ShapeDtypeStruct((b, t, d), F32),
                   jax.ShapeDtypeStruct((b, K_FFN - 1, d_ff), F32)],
        scratch_shapes=[pltpu.VMEM((n_streams, n_chunks, 8, fc), F32)],
        compiler_params=pltpu.CompilerParams(
            dimension_semantics=("arbitrary", "arbitrary"),
            vmem_limit_bytes=V7X_VMEM_LIMIT_BYTES),
        name="conv_ffn",
    )(*args)


CONV_PAD = 32
CONV_ROW_BLOCK = 32


def _even_kernel(*refs, n_streams, rows, sgu_n, emit_v):
    (x_ref, hist_ref, g_ref, win_ref, lnvg_ref, lnvb_ref, ws_ref, bs_ref, wdw_ref, bdw_ref,
     lncg_ref, lncb_ref, wout_ref) = refs[:13]
    if emit_v:
        y_ref, av_ref, nb_ref, conv_ref, cout_ref = refs[13:]
    else:
        y_ref, nb_ref, conv_ref, cout_ref = refs[13:]
        av_ref = None
    d_model = x_ref.shape[-1]
    d_b = wdw_ref.shape[-1]
    d_a = d_b
    dg = d_a // G_A
    m = n_streams * rows
    hist_rows = K_B - 1
    off = CONV_PAD - hist_rows

    @pl.when(pl.program_id(1) == 0)
    def _():
        conv_ref[:, off:CONV_PAD, :] = hist_ref[...]

    x = x_ref[...].reshape(m, d_model)
    xn = _rms(x, g_ref[...]).astype(BF16)
    z = _dot(xn, win_ref[...])

    za = jax.nn.gelu(z[:, :2 * d_a], approximate=True)
    u = za[:, :d_a]
    lnvg = lnvg_ref[...]
    lnvb = lnvb_ref[...]
    v = jnp.concatenate(
        [_layer_norm(za[:, d_a + g * dg:d_a + (g + 1) * dg], lnvg[:, g * dg:(g + 1) * dg],
                     lnvb[:, g * dg:(g + 1) * dg]) for g in range(G_A)], axis=1)
    if emit_v:
        av_ref[...] = v.reshape(av_ref.shape)
    vb = v.astype(BF16)
    tril = (lax.broadcasted_iota(jnp.int32, (sgu_n, sgu_n), 0)
            >= lax.broadcasted_iota(jnp.int32, (sgu_n, sgu_n), 1))
    ws = [jnp.where(tril, ws_ref[g], 0.0).astype(BF16) for g in range(G_A)]
    bs = bs_ref[...]
    sg_rows = []
    for c in range(m // sgu_n):
        vc = vb[c * sgu_n:(c + 1) * sgu_n]
        sg_rows.append(jnp.concatenate(
            [_dot(ws[g], vc[:, g * dg:(g + 1) * dg]) for g in range(G_A)], axis=1) + bs)
    sg = sg_rows[0] if len(sg_rows) == 1 else jnp.concatenate(sg_rows, axis=0)
    y_a = u * sg

    glu = z[:, 2 * d_a:2 * d_a + d_b] * jax.nn.sigmoid(z[:, 2 * d_a + d_b:])
    wdw = wdw_ref[...]
    bdw = bdw_ref[...]
    for s in range(n_streams):
        conv_ref[s, CONV_PAD:CONV_PAD + rows, :] = glu[s * rows:(s + 1) * rows]
        for rb in range(rows // CONV_ROW_BLOCK):
            r0 = rb * CONV_ROW_BLOCK
            acc = jnp.zeros((CONV_ROW_BLOCK, d_b), F32) + bdw
            for k in range(K_B):
                acc = acc + wdw[k:k + 1] * conv_ref[s, r0 + off + k:r0 + off + k + CONV_ROW_BLOCK, :]
            cout_ref[s * rows + r0:s * rows + r0 + CONV_ROW_BLOCK, :] = acc
        nb_ref[s] = conv_ref[s, rows + off:rows + CONV_PAD, :]
        conv_ref[s, 0:CONV_PAD, :] = conv_ref[s, rows:rows + CONV_PAD, :]
    y_b = jax.nn.silu(_layer_norm(cout_ref[...], lncg_ref[...], lncb_ref[...]))

    y = _dot(jnp.concatenate([y_a, y_b], axis=1).astype(BF16), wout_ref[...])
    y_ref[...] = (x + y).reshape(y_ref.shape)


def _even_layer(x, conv_hist, g_norm, w_in, ln_v_g, ln_v_b, w_s, b_s, w_dw, b_dw, ln_c_g, ln_c_b,
                w_out, *, n_streams, rows, emit_v):
    b, t, d = x.shape
    d_b = w_dw.shape[-1]
    d_a = ln_v_g.shape[-1]
    dg = d_a // G_A
    sgu_n = min(t, SGU_CHUNK)
    assert b % n_streams == 0 and t % rows == 0 and rows % sgu_n == 0 and (n_streams * rows) % sgu_n == 0
    assert rows % CONV_ROW_BLOCK == 0 and rows >= CONV_PAD and d_a == d_b
    bs_full = jnp.repeat(b_s[:, :sgu_n].T, dg, axis=1)
    args = [x, conv_hist, g_norm.reshape(1, d), w_in.astype(BF16), ln_v_g.reshape(1, d_a),
            ln_v_b.reshape(1, d_a), w_s[:, :sgu_n, :sgu_n], bs_full, w_dw, b_dw.reshape(1, d_b),
            ln_c_g.reshape(1, d_b), ln_c_b.reshape(1, d_b), w_out.astype(BF16)]
    in_specs = [
        pl.BlockSpec((n_streams, rows, d), lambda i, j: (i, j, 0)),
        pl.BlockSpec((n_streams, K_B - 1, d_b), lambda i, j: (i, 0, 0)),
    ] + [_const_spec(a.shape) for a in args[2:]]
    out_specs = [pl.BlockSpec((n_streams, rows, d), lambda i, j: (i, j, 0))]
    out_shape = [jax.ShapeDtypeStruct((b, t, d), F32)]
    if emit_v:
        out_specs.append(pl.BlockSpec((n_streams, rows, d_a), lambda i, j: (i, j, 0)))
        out_shape.append(jax.ShapeDtypeStruct((b, t, d_a), F32))
    out_specs.append(pl.BlockSpec((n_streams, K_B - 1, d_b), lambda i, j: (i, 0, 0)))
    out_shape.append(jax.ShapeDtypeStruct((b, K_B - 1, d_b), F32))
    kern = functools.partial(_even_kernel, n_streams=n_streams, rows=rows, sgu_n=sgu_n, emit_v=emit_v)
    return pl.pallas_call(
        kern,
        grid=(b // n_streams, t // rows),
        in_specs=in_specs,
        out_specs=out_specs,
        out_shape=out_shape,
        scratch_shapes=[pltpu.VMEM((n_streams, CONV_PAD + rows, d_b), F32),
                        pltpu.VMEM((n_streams * rows, d_b), F32)],
        compiler_params=pltpu.CompilerParams(
            dimension_semantics=("arbitrary", "arbitrary"),
            vmem_limit_bytes=V7X_VMEM_LIMIT_BYTES),
        name="even_layer",
    )(*args)


POOL_PAD = 16
INT_MIN = -2 ** 31
NEG_INF_KEY = -2 ** 31 + 0x7FFFFF


def _t5_bucket(rel):
    nb = N_BUCKETS // 2
    exact = nb // 2
    side = jnp.where(rel > 0, nb, 0)
    n = jnp.abs(rel)
    large = exact + (jnp.log(jnp.maximum(n, 1).astype(jnp.float32) / exact)
                     / math.log(MAX_DIST / exact) * (nb - exact)).astype(jnp.int32)
    large = jnp.minimum(large, nb - 1)
    return side + jnp.where(n < exact, n, large)


def _sortable(score):
    bits = pltpu.bitcast(score, jnp.int32)
    return bits ^ ((bits >> 31) & 0x7FFFFFFF)


def _odd_kernel(*refs, rows, qb, hist_len, n_valid_hist, top, has_hist):
    refs = list(refs)
    x_ref = refs.pop(0)
    if has_hist:
        kkh_ref, kk2h_ref, vvh_ref = refs[:3]
        refs = refs[3:]
    (poolh_ref, g_ref, win_ref, wpool_ref, spool_ref, wout_ref, bucket_ref, rb_ref, tri_ref,
     y_ref, k_ref, v_ref, ki_ref, np_ref,
     kk_ref, kk2_ref, vv_ref, keys_ref, lg_ref, acc_ref, mx_ref, off_ref, thr_ref, need_ref,
     bias_ref, pool_ref, yc_ref, qe_ref, qo_ref) = refs
    d_model = x_ref.shape[-1]
    d_c = H_C * HD_C
    d_qi = H_I * D_I
    d_d = wpool_ref.shape[0]
    n_pairs = H_C // 2
    n_ipairs = H_I // 2
    kt = KEY_TILE
    lanes = V7X_LANES
    t = pl.program_id(1)
    t0 = t * rows
    hist_tiles = hist_len // kt
    col_xd = d_c + d_qi
    col_kv = col_xd + d_d
    col_kw = col_kv + lanes

    @pl.when(jnp.logical_and(pl.program_id(0) == 0, t == 0))
    def _():
        bucket = bucket_ref[...]
        for h in range(H_C):
            b_acc = jnp.zeros(bucket.shape, F32)
            for b in range(N_BUCKETS):
                b_acc = jnp.where(bucket == b, rb_ref[b, h], b_acc)
            bias_ref[:, h] = b_acc

    @pl.when(t == 0)
    def _():
        pool_ref[POOL_PAD - POOL_HIST:POOL_PAD, :] = poolh_ref[0]
        if has_hist:
            for j in range(hist_tiles):
                kk_ref[j] = kkh_ref[0, j * kt:(j + 1) * kt, :]
                kk2_ref[j] = kk2h_ref[0, j * kt:(j + 1) * kt, :]
                vv_ref[j] = vvh_ref[0, j * kt:(j + 1) * kt, :]

    x = x_ref[0]
    xn = _rms(x, g_ref[...]).astype(BF16)
    z = _dot(xn, win_ref[...])

    g1 = z[:, col_kv:col_kv + lanes]
    g2 = z[:, col_kw:col_kw + lanes]
    k_ref[0] = g1[:, :HD_C]
    v_ref[0] = g1[:, HD_C:]
    ki_ref[0] = g2[:, :D_I]
    lane = lax.broadcasted_iota(jnp.int32, (rows, lanes), 1)
    low = lane < HD_C
    g1r = pltpu.roll(g1, HD_C, 1)
    g2r = pltpu.roll(g2, D_I, 1)
    kk_new = jnp.where(low, g1, g2r).astype(BF16)
    kk2_new = jnp.where(low, g2, g1r).astype(BF16)
    vv_new = jnp.where(low, g1r, jnp.where(lane == HD_C, 1.0, 0.0)).astype(BF16)
    if rows % kt == 0:
        base_tile = (hist_len + t0) // kt
        for i in range(rows // kt):
            kk_ref[base_tile + i] = kk_new[i * kt:(i + 1) * kt]
            kk2_ref[base_tile + i] = kk2_new[i * kt:(i + 1) * kt]
            vv_ref[base_tile + i] = vv_new[i * kt:(i + 1) * kt]
    else:
        zpad = jnp.zeros((kt - rows, lanes), BF16)
        kk_ref[hist_tiles] = jnp.concatenate([kk_new, zpad], axis=0)
        kk2_ref[hist_tiles] = jnp.concatenate([kk2_new, zpad], axis=0)
        vv_ref[hist_tiles] = jnp.concatenate([vv_new, zpad], axis=0)
    kv_len = hist_len + t0 + rows

    w_idx = g2[:, D_I:D_I + H_I] * ((H_I ** -0.5) * (D_I ** -0.5))
    tri = tri_ref[...]
    ones_rhs = jnp.ones((kt, lanes), BF16)
    lane_q = lax.broadcasted_iota(jnp.int32, (qb, lanes), 1)
    low_q = lane_q < HD_C
    kchunk_rel = lax.broadcasted_iota(jnp.int32, (qb, kt), 1)
    row_q = lax.broadcasted_iota(jnp.int32, (qb, 1), 0)

    for s in range(rows // qb):
        r0 = s * qb
        qpos0 = hist_len + t0 + r0
        n_tiles = (qpos0 + qb + kt - 1) // kt
        last = n_tiles - 1
        parity = (r0 // lanes) % 2

        for g in range(n_pairs):
            grp = z[r0:r0 + qb, g * lanes:(g + 1) * lanes] * (HD_C ** -0.5)
            qe_ref[g * qb:(g + 1) * qb] = jnp.where(low_q, grp, 0.0).astype(BF16)
            qo_ref[g * qb:(g + 1) * qb] = jnp.where(low_q, 0.0, grp).astype(BF16)
        ie = []
        io = []
        for g in range(n_ipairs):
            grp = z[r0:r0 + qb, d_c + g * lanes:d_c + (g + 1) * lanes]
            ie.append(jnp.where(low_q, grp, 0.0).astype(BF16))
            io.append(jnp.where(low_q, 0.0, grp).astype(BF16))
        ie = jnp.concatenate(ie, axis=0)
        io = jnp.concatenate(io, axis=0)
        w_s = w_idx[r0:r0 + qb]
        qchunk = (qpos0 + row_q) >> CHUNK_SHIFT

        def score_body(j, carry):
            se = jnp.maximum(_dot_nt(ie, kk2_ref[j]), 0.0)
            so = jnp.maximum(_dot_nt(io, kk_ref[j]), 0.0)
            score = jnp.zeros((qb, kt), F32)
            for g in range(n_ipairs):
                score = score + se[g * qb:(g + 1) * qb] * w_s[:, 2 * g:2 * g + 1]
                score = score + so[g * qb:(g + 1) * qb] * w_s[:, 2 * g + 1:2 * g + 2]
            kpos = kchunk_rel + j * kt
            adm = jnp.logical_and((kpos >> CHUNK_SHIFT) <= qchunk, kpos < kv_len)
            keys_ref[j] = _sortable(jnp.where(adm, score, NEG_INF))
            return carry
        lax.fori_loop(0, n_tiles, score_body, 0)

        def count_ge(cand):
            def body(j, acc):
                kj = keys_ref[j]
                for c in range(kt // lanes):
                    acc = acc + jnp.where(kj[:, c * lanes:(c + 1) * lanes] >= cand, 1.0, 0.0)
                return acc
            acc = lax.fori_loop(0, n_tiles, body, jnp.zeros((qb, lanes), F32))
            return jnp.sum(acc, axis=1, keepdims=True)

        u = jnp.zeros((qb, lanes), jnp.int32)
        for b in range(31, -1, -1):
            bit = INT_MIN if b == 31 else (1 << b)
            cand_u = u | jnp.int32(bit)
            cnt = count_ge(cand_u ^ jnp.int32(INT_MIN))
            u = jnp.where(cnt >= float(top), cand_u, u)
        thr = u ^ jnp.int32(INT_MIN)
        thr_ref[...] = thr

        def count_gt_body(j, acc):
            kj = keys_ref[j]
            for c in range(kt // lanes):
                acc = acc + jnp.where(kj[:, c * lanes:(c + 1) * lanes] > thr, 1.0, 0.0)
            return acc
        acc_gt = lax.fori_loop(0, n_tiles, count_gt_body, jnp.zeros((qb, lanes), F32))
        need_ref[...] = jnp.broadcast_to(float(top) - jnp.sum(acc_gt, axis=1, keepdims=True),
                                         (qb, lanes))
        off_ref[...] = jnp.zeros((qb, lanes), F32)
        mx_ref[...] = jnp.full((H_C * qb, lanes), NEG_INF, F32)

        def logits_tile(j, bias_halves):
            kj = keys_ref[j]
            thr_t = jnp.concatenate([thr_ref[...]] * (kt // lanes), axis=1)
            eq = jnp.where(kj == thr_t, 1.0, 0.0)
            eq_b = eq.astype(BF16)
            rank = _dot(eq_b, tri) + jnp.concatenate([off_ref[...]] * (kt // lanes), axis=1)
            need_t = jnp.concatenate([need_ref[...]] * (kt // lanes), axis=1)
            self_ = jnp.where(kj > thr_t, 1.0, jnp.where(rank <= need_t, eq, 0.0))
            sel = jnp.where(kj > NEG_INF_KEY, self_, 0.0) > 0.5
            off_ref[...] = off_ref[...] + _dot(eq_b, ones_rhs)
            lg_e = _dot_nt(qe_ref[...], kk_ref[j])
            lg_o = _dot_nt(qo_ref[...], kk2_ref[j])
            for h in range(H_C):
                src = lg_e if h % 2 == 0 else lg_o
                l = src[(h // 2) * qb:(h // 2 + 1) * qb]
                if bias_halves is not None:
                    parts = []
                    for c in range(kt // lanes):
                        part = l[:, c * lanes:(c + 1) * lanes]
                        if bias_halves[c] is not None:
                            part = part + bias_ref[bias_halves[c], h]
                        parts.append(part)
                    l = jnp.concatenate(parts, axis=1)
                l = jnp.where(sel, l, NEG_INF)
                lg_ref[j, h * qb:(h + 1) * qb] = l
                m_old = mx_ref[h * qb:(h + 1) * qb]
                for c in range(kt // lanes):
                    m_old = jnp.maximum(m_old, l[:, c * lanes:(c + 1) * lanes])
                mx_ref[h * qb:(h + 1) * qb] = m_old

        def far_body(j, carry):
            logits_tile(j, None)
            return carry
        if parity == 0:
            lax.fori_loop(0, last - 1, far_body, 0)

            @pl.when(last >= 1)
            def _():
                logits_tile(last - 1, (None, 0))
            logits_tile(last, (1, None))
        else:
            lax.fori_loop(0, last, far_body, 0)
            logits_tile(last, (0, 1))

        for h in range(H_C):
            mh = jnp.max(mx_ref[h * qb:(h + 1) * qb], axis=1, keepdims=True)
            mx_ref[h * qb:(h + 1) * qb] = jnp.broadcast_to(mh, (qb, lanes))
        acc_ref[...] = jnp.zeros((H_C * qb, lanes), F32)

        def pv_body(j, carry):
            mt = jnp.concatenate([mx_ref[...]] * (kt // lanes), axis=1)
            p = jnp.exp(lg_ref[j] - mt).astype(BF16)
            acc_ref[...] = acc_ref[...] + _dot(p, vv_ref[j])
            return carry
        lax.fori_loop(0, n_tiles, pv_body, 0)

        for g in range(n_pairs):
            oe = acc_ref[(2 * g) * qb:(2 * g + 1) * qb]
            oo = acc_ref[(2 * g + 1) * qb:(2 * g + 2) * qb]
            oe = oe / oe[:, HD_C:HD_C + 1]
            oo = oo / oo[:, HD_C:HD_C + 1]
            yc_ref[r0:r0 + qb, g * lanes:(g + 1) * lanes] = jnp.where(
                low_q, oe, pltpu.roll(oo, HD_C, 1))

    xd = z[:, col_xd:col_xd + d_d]
    pool_ref[POOL_PAD:POOL_PAD + rows, :] = xd
    np_ref[0] = pool_ref[rows + POOL_PAD - POOL_HIST:rows + POOL_PAD, :]
    run = xd
    wins = {}
    for dshift in range(1, POOL_WINDOWS[-1]):
        run = run + pool_ref[POOL_PAD - dshift:POOL_PAD - dshift + rows, :]
        if dshift + 1 in POOL_WINDOWS:
            wins[dshift + 1] = run
    pool_ref[0:POOL_PAD, :] = pool_ref[rows:rows + POOL_PAD, :]
    dg_d = d_d // len(POOL_WINDOWS)
    lane_d = lax.broadcasted_iota(jnp.int32, (rows, d_d), 1)
    tpos = n_valid_hist + t0 + 1 + lax.broadcasted_iota(jnp.int32, (rows, d_d), 0)
    win_sum = wins[POOL_WINDOWS[-1]]
    width = jnp.full((rows, d_d), POOL_WINDOWS[-1], jnp.int32)
    for gi in range(len(POOL_WINDOWS) - 2, -1, -1):
        in_g = lane_d < (gi + 1) * dg_d
        win_sum = jnp.where(in_g, wins[POOL_WINDOWS[gi]], win_sum)
        width = jnp.where(in_g, POOL_WINDOWS[gi], width)
    count = jnp.minimum(tpos, width).astype(F32)
    m_pool = win_sum / count - xd
    y_d = _dot(m_pool.astype(BF16), wpool_ref[...]) * spool_ref[...]

    y_cat = jnp.concatenate([yc_ref[...], y_d], axis=1).astype(BF16)
    y_ref[0] = x + _dot(y_cat, wout_ref[...])


def _odd_layer(x, k_hist, v_hist, ki_hist, pool_hist, n_valid_hist, g_norm, w_in, w_pool, s_pool,
               w_out, rel_bias, *, rows):
    b, t, d = x.shape
    hist_len = k_hist.shape[1]
    has_hist = hist_len > 0
    d_c = H_C * HD_C
    d_qi = H_I * D_I
    d_d = d - d_c
    qb = min(2 * CHUNK, rows)
    kt = KEY_TILE
    lanes = V7X_LANES
    s_total = hist_len + t
    top = min(TOPK_MAX, s_total // 4)
    assert t % rows == 0 and rows % qb == 0 and hist_len % kt == 0
    assert rows % kt == 0 or (rows == t and rows == qb and rows <= CHUNK)
    n_tiles = (s_total + kt - 1) // kt

    offs = np.cumsum([0, d_c, HD_C, HD_C, d_qi, D_I, H_I]).tolist()
    q_w, k_w, v_w, qi_w, ki_w, wi_w = (w_in[:, offs[i]:offs[i + 1]] for i in range(6))
    xd_w = w_in[:, offs[6]:]
    pad_w = jnp.zeros((d, lanes - D_I - H_I), w_in.dtype)
    w_all = jnp.concatenate([q_w, qi_w, xd_w, k_w, v_w, ki_w, wi_w, pad_w], axis=1).astype(BF16)
    wpool_bd = jax.scipy.linalg.block_diag(*[w_pool[g] for g in range(w_pool.shape[0])]).astype(BF16)

    rel = (lanes * jnp.arange(-1, 1, dtype=jnp.int32)[:, None, None]
           + jnp.arange(lanes, dtype=jnp.int32)[None, None, :]
           - jnp.arange(qb, dtype=jnp.int32)[None, :, None])
    bucket = _t5_bucket(rel)
    far_bucket = _t5_bucket(jnp.int32(-2 * lanes))
    rb_shift = rel_bias - rel_bias[far_bucket][None, :]
    tri = (jnp.arange(kt)[:, None] <= jnp.arange(kt)[None, :]).astype(BF16)

    args = [x]
    in_specs = [pl.BlockSpec((1, rows, d), lambda i, j: (i, j, 0))]
    if has_hist:
        ones = jnp.ones((b, hist_len, 1), F32)
        zeros = jnp.zeros((b, hist_len, lanes - HD_C - 1), F32)
        args += [jnp.concatenate([k_hist, ki_hist], axis=-1).astype(BF16),
                 jnp.concatenate([ki_hist, k_hist], axis=-1).astype(BF16),
                 jnp.concatenate([v_hist, ones, zeros], axis=-1).astype(BF16)]
        in_specs += [pl.BlockSpec((1, hist_len, lanes), lambda i, j: (i, 0, 0))] * 3
    consts = [g_norm.reshape(1, d), w_all, wpool_bd, s_pool.reshape(1, d_d), w_out.astype(BF16), bucket]
    args += [pool_hist] + consts + [rb_shift, tri]
    in_specs += ([pl.BlockSpec((1, POOL_HIST, d_d), lambda i, j: (i, 0, 0))]
                 + [_const_spec(a.shape) for a in consts]
                 + [pl.BlockSpec(memory_space=pltpu.SMEM), _const_spec(tri.shape)])
    kern = functools.partial(_odd_kernel, rows=rows, qb=qb, hist_len=hist_len,
                             n_valid_hist=n_valid_hist, top=top, has_hist=has_hist)
    return pl.pallas_call(
        kern,
        grid=(b, t // rows),
        in_specs=in_specs,
        out_specs=[
            pl.BlockSpec((1, rows, d), lambda i, j: (i, j, 0)),
            pl.BlockSpec((1, rows, HD_C), lambda i, j: (i, j, 0)),
            pl.BlockSpec((1, rows, HD_C), lambda i, j: (i, j, 0)),
            pl.BlockSpec((1, rows, D_I), lambda i, j: (i, j, 0)),
            pl.BlockSpec((1, POOL_HIST, d_d), lambda i, j: (i, 0, 0)),
        ],
        out_shape=[jax.ShapeDtypeStruct((b, t, d), F32),
                   jax.ShapeDtypeStruct((b, t, HD_C), F32),
                   jax.ShapeDtypeStruct((b, t, HD_C), F32),
                   jax.ShapeDtypeStruct((b, t, D_I), F32),
                   jax.ShapeDtypeStruct((b, POOL_HIST, d_d), F32)],
        scratch_shapes=[
            pltpu.VMEM((n_tiles, kt, lanes), BF16),
            pltpu.VMEM((n_tiles, kt, lanes), BF16),
            pltpu.VMEM((n_tiles, kt, lanes), BF16),
            pltpu.VMEM((n_tiles, qb, kt), jnp.int32),
            pltpu.VMEM((n_tiles, H_C * qb, kt), F32),
            pltpu.VMEM((H_C * qb, lanes), F32),
            pltpu.VMEM((H_C * qb, lanes), F32),
            pltpu.VMEM((qb, lanes), F32),
            pltpu.VMEM((qb, lanes), jnp.int32),
            pltpu.VMEM((qb, lanes), F32),
            pltpu.VMEM((2, H_C, qb, lanes), F32),
            pltpu.VMEM((POOL_PAD + rows, d_d), F32),
            pltpu.VMEM((rows, d_c), F32),
            pltpu.VMEM((H_C // 2 * qb, lanes), BF16),
            pltpu.VMEM((H_C // 2 * qb, lanes), BF16),
        ],
        compiler_params=pltpu.CompilerParams(
            dimension_semantics=("arbitrary", "arbitrary"),
            vmem_limit_bytes=V7X_VMEM_LIMIT_BYTES),
        name="odd_layer",
    )(*args)


def kernel(x_prompt, x_sample, cache_b_conv, cache_c_k, cache_c_v, cache_c_kidx, cache_d_pool, cache_ffn_conv, ln_mix, ln_ffn, ln_final, e_w_in, e_ln_v_g, e_ln_v_b, e_w_s, e_b_s, e_w_dw, e_b_dw, e_ln_c_g, e_ln_c_b, e_w_out, o_w_in, o_w_pool, o_s_pool, o_w_out, rel_bias, f_w_up, f_w_dw, f_b_dw, f_w_down):
    hp, hs = x_prompt, x_sample
    bp, bs = x_prompt.shape[0], x_sample.shape[0]
    ts = x_sample.shape[1]
    depth = ln_mix.shape[0]
    d_ff = f_w_down.shape[1]
    a_s_l, b_p_l, b_s_l = [], [], []
    ck_p_l, cv_p_l, cki_p_l, ck_s_l, cv_s_l, cki_s_l, d_p_l, d_s_l = [], [], [], [], [], [], [], []
    f_p_l, f_s_l = [], []
    for layer in range(depth):
        i = layer // 2
        if layer % 2 == 0:
            ew = (ln_mix[layer], e_w_in[i], e_ln_v_g[i], e_ln_v_b[i], e_w_s[i], e_b_s[i], e_w_dw[i],
                  e_b_dw[i], e_ln_c_g[i], e_ln_c_b[i], e_w_out[i])
            hp, b_p = _even_layer(hp, jnp.zeros((bp, K_B - 1, e_w_dw.shape[-1]), F32), *ew,
                                  n_streams=1, rows=PROMPT_ROWS, emit_v=False)
            hs, a_s, b_s = _even_layer(hs, cache_b_conv[i], *ew, n_streams=bs, rows=ts, emit_v=True)
            a_s_l.append(a_s); b_p_l.append(b_p); b_s_l.append(b_s)
        else:
            ow = (ln_mix[layer], o_w_in[i], o_w_pool[i], o_s_pool[i], o_w_out[i], rel_bias)
            d_d = o_w_pool.shape[1] * o_w_pool.shape[2]
            hp, k_p, v_p, ki_p, d_p = _odd_layer(
                hp, jnp.zeros((bp, 0, HD_C), F32), jnp.zeros((bp, 0, HD_C), F32),
                jnp.zeros((bp, 0, D_I), F32), jnp.zeros((bp, POOL_HIST, d_d), F32), 0, *ow,
                rows=PROMPT_ROWS)
            hs, k_s, v_s, ki_s, d_s = _odd_layer(
                hs, cache_c_k[i], cache_c_v[i], cache_c_kidx[i], cache_d_pool[i], POOL_HIST, *ow,
                rows=ts)
            ck_p_l.append(k_p); cv_p_l.append(v_p); cki_p_l.append(ki_p)
            ck_s_l.append(k_s); cv_s_l.append(v_s); cki_s_l.append(ki_s)
            d_p_l.append(d_p); d_s_l.append(d_s)
        g_final = ln_final if layer == depth - 1 else None
        fw = (ln_ffn[layer], f_w_up[layer], f_w_dw[layer], f_b_dw[layer], f_w_down[layer], g_final)
        hp, f_p = _conv_ffn(hp, jnp.zeros((bp, K_FFN - 1, d_ff), F32), *fw,
                            n_streams=1, rows=PROMPT_ROWS, n_chunks=FFN_CHUNKS)
        hs, f_s = _conv_ffn(hs, cache_ffn_conv[layer], *fw, n_streams=bs, rows=ts, n_chunks=FFN_CHUNKS)
        f_p_l.append(f_p); f_s_l.append(f_s)
    return (hp, hs,
            jnp.stack(a_s_l), jnp.stack(b_p_l), jnp.stack(b_s_l),
            jnp.stack(ck_p_l), jnp.stack(cv_p_l), jnp.stack(cki_p_l),
            jnp.stack(ck_s_l), jnp.stack(cv_s_l), jnp.stack(cki_s_l),
            jnp.stack(d_p_l), jnp.stack(d_s_l),
            jnp.stack(f_p_l), jnp.stack(f_s_l))
```

```python
import functools
import math

import jax
import jax.numpy as jnp
import numpy as np
from jax import lax
from jax.experimental import pallas as pl
from jax.experimental.pallas import tpu as pltpu

F32 = jnp.float32
BF16 = jnp.bfloat16

EPS = 1e-6
CHUNK = 64
CHUNK_SHIFT = CHUNK.bit_length() - 1
SGU_CHUNK = 128
G_A = 4
K_B = 31
H_C = 12
HD_C = 64
H_I = 8
D_I = 64
TOPK_MAX = 256
N_BUCKETS = 32
MAX_DIST = 128
POOL_WINDOWS = (2, 4, 8, 16)
POOL_HIST = 15
K_FFN = 3

V7X_LANES = 128
V7X_SUBLANES = 8
V7X_MXU_DIM = 256
V7X_VMEM_LIMIT_BYTES = 56 * 1024 * 1024

KEY_TILE = V7X_MXU_DIM
PROMPT_ROWS = 512
FFN_CHUNKS = 2
NEG_INF = float("-inf")


def _rms(x, g):
    return x * lax.rsqrt(jnp.mean(x * x, axis=-1, keepdims=True) + EPS) * g


def _layer_norm(x, g, b):
    mu = jnp.mean(x, axis=-1, keepdims=True)
    xc = x - mu
    return xc * lax.rsqrt(jnp.mean(xc * xc, axis=-1, keepdims=True) + EPS) * g + b


def _gelu_tanh(x):
    cdf = 0.5 * (1.0 + jnp.tanh(math.sqrt(2.0 / math.pi) * (x + 0.044715 * (x * x * x))))
    return x * cdf


def _dot(a, b):
    return jnp.dot(a, b, preferred_element_type=F32)


def _dot_nt(a, b):
    return lax.dot_general(a, b, (((1,), (1,)), ((), ())), preferred_element_type=F32)


def _const_spec(shape):
    nd = len(shape)
    return pl.BlockSpec(shape, lambda *_: (0,) * nd, pipeline_mode=pl.Buffered(1))


def _ffn_kernel(*refs, n_streams, rows, n_chunks, final_norm):
    if final_norm:
        (x_ref, hist_ref, g_ref, wa_ref, wv_ref, wd_ref, dw_ref, bdw_ref, gf_ref,
         y_ref, nh_ref, carry_ref) = refs
    else:
        (x_ref, hist_ref, g_ref, wa_ref, wv_ref, wd_ref, dw_ref, bdw_ref,
         y_ref, nh_ref, carry_ref) = refs
        gf_ref = None
    d_model = x_ref.shape[-1]
    fc = wa_ref.shape[-1]
    m = n_streams * rows

    @pl.when(pl.program_id(1) == 0)
    def _():
        for c in range(n_chunks):
            carry_ref[:, c, 6:8, :] = hist_ref[:, :, c * fc:(c + 1) * fc]

    x = x_ref[...].reshape(m, d_model)
    xn = _rms(x, g_ref[...]).astype(BF16)
    acc = jnp.zeros((m, d_model), F32)
    for c in range(n_chunks):
        a = _dot(xn, wa_ref[c])
        val = _dot(xn, wv_ref[c])
        w = dw_ref[c]
        ys = []
        for s in range(n_streams):
            a_s = a[s * rows:(s + 1) * rows]
            ext = jnp.concatenate([carry_ref[s, c], a_s], axis=0)
            ys.append(w[0:1] * ext[6:6 + rows] + w[1:2] * ext[7:7 + rows] + w[2:3] * a_s)
            carry_ref[s, c] = a_s[rows - 8:rows]
            nh_ref[s, :, c * fc:(c + 1) * fc] = a_s[rows - 2:rows]
        y = (ys[0] if n_streams == 1 else jnp.concatenate(ys, axis=0)) + bdw_ref[c]
        acc = acc + _dot((_gelu_tanh(y) * val).astype(BF16), wd_ref[c])
    out = x + acc
    if final_norm:
        out = _rms(out, gf_ref[...])
    y_ref[...] = out.reshape(y_ref.shape)


def _conv_ffn(x, hist, g_norm, w_up, w_dw, b_dw, w_down, g_final, *, n_streams, rows, n_chunks):
    b, t, d = x.shape
    d_ff = w_down.shape[0]
    fc = d_ff // n_chunks
    assert fc * n_chunks == d_ff and fc % V7X_LANES == 0
    assert b % n_streams == 0 and t % rows == 0 and rows % V7X_SUBLANES == 0
    wa = w_up[:, :d_ff].astype(BF16).reshape(d, n_chunks, fc).transpose(1, 0, 2)
    wv = w_up[:, d_ff:].astype(BF16).reshape(d, n_chunks, fc).transpose(1, 0, 2)
    wd = w_down.astype(BF16).reshape(n_chunks, fc, d)
    dw = w_dw.reshape(K_FFN, n_chunks, fc).transpose(1, 0, 2)
    bdw = b_dw.reshape(n_chunks, 1, fc)
    final_norm = g_final is not None
    args = [x, hist, g_norm.reshape(1, d), wa, wv, wd, dw, bdw]
    in_specs = [
        pl.BlockSpec((n_streams, rows, d), lambda i, j: (i, j, 0)),
        pl.BlockSpec((n_streams, K_FFN - 1, d_ff), lambda i, j: (i, 0, 0)),
        _const_spec((1, d)),
        _const_spec(wa.shape), _const_spec(wv.shape), _const_spec(wd.shape),
        _const_spec(dw.shape), _const_spec(bdw.shape),
    ]
    if final_norm:
        args.append(g_final.reshape(1, d))
        in_specs.append(_const_spec((1, d)))
    kern = functools.partial(_ffn_kernel, n_streams=n_streams, rows=rows, n_chunks=n_chunks,
                             final_norm=final_norm)
    return pl.pallas_call(
        kern,
        grid=(b // n_streams, t // rows),
        in_specs=in_specs,
        out_specs=[
            pl.BlockSpec((n_streams, rows, d), lambda i, j: (i, j, 0)),
            pl.BlockSpec((n_streams, K_FFN - 1, d_ff), lambda i, j: (i, 0, 0)),
        ],
        out_shape=[jax.ShapeDtypeStruct((b, t, d), F32),
                   jax.ShapeDtypeStruct((b, K_FFN - 1, d_ff), F32)],
        scratch_shapes=[pltpu.VMEM((n_streams, n_chunks, 8, fc), F32)],
        compiler_params=pltpu.CompilerParams(
            dimension_semantics=("arbitrary", "arbitrary"),
            vmem_limit_bytes=V7X_VMEM_LIMIT_BYTES),
        name="conv_ffn",
    )(*args)


CONV_PAD = 32
CONV_ROW_BLOCK = 32


def _even_kernel(*refs, n_streams, rows, sgu_n, emit_v):
    (x_ref, hist_ref, g_ref, win_ref, lnvg_ref, lnvb_ref, ws_ref, bs_ref, wdw_ref, bdw_ref,
     lncg_ref, lncb_ref, wout_ref) = refs[:13]
    if emit_v:
        y_ref, av_ref, nb_ref, conv_ref, cout_ref = refs[13:]
    else:
        y_ref, nb_ref, conv_ref, cout_ref = refs[13:]
        av_ref = None
    d_model = x_ref.shape[-1]
    d_b = wdw_ref.shape[-1]
    d_a = d_b
    dg = d_a // G_A
    m = n_streams * rows
    hist_rows = K_B - 1
    off = CONV_PAD - hist_rows

    @pl.when(pl.program_id(1) == 0)
    def _():
        conv_ref[:, off:CONV_PAD, :] = hist_ref[...]

    x = x_ref[...].reshape(m, d_model)
    xn = _rms(x, g_ref[...]).astype(BF16)
    z = _dot(xn, win_ref[...])

    za = jax.nn.gelu(z[:, :2 * d_a], approximate=True)
    u = za[:, :d_a]
    lnvg = lnvg_ref[...]
    lnvb = lnvb_ref[...]
    v = jnp.concatenate(
        [_layer_norm(za[:, d_a + g * dg:d_a + (g + 1) * dg], lnvg[:, g * dg:(g + 1) * dg],
                     lnvb[:, g * dg:(g + 1) * dg]) for g in range(G_A)], axis=1)
    if emit_v:
        av_ref[...] = v.reshape(av_ref.shape)
    vb = v.astype(BF16)
    tril = (lax.broadcasted_iota(jnp.int32, (sgu_n, sgu_n), 0)
            >= lax.broadcasted_iota(jnp.int32, (sgu_n, sgu_n), 1))
    ws = [jnp.where(tril, ws_ref[g], 0.0).astype(BF16) for g in range(G_A)]
    bs = bs_ref[...]
    sg_rows = []
    for c in range(m // sgu_n):
        vc = vb[c * sgu_n:(c + 1) * sgu_n]
        sg_rows.append(jnp.concatenate(
            [_dot(ws[g], vc[:, g * dg:(g + 1) * dg]) for g in range(G_A)], axis=1) + bs)
    sg = sg_rows[0] if len(sg_rows) == 1 else jnp.concatenate(sg_rows, axis=0)
    y_a = u * sg

    glu = z[:, 2 * d_a:2 * d_a + d_b] * jax.nn.sigmoid(z[:, 2 * d_a + d_b:])
    wdw = wdw_ref[...]
    bdw = bdw_ref[...]
    for s in range(n_streams):
        conv_ref[s, CONV_PAD:CONV_PAD + rows, :] = glu[s * rows:(s + 1) * rows]
        for rb in range(rows // CONV_ROW_BLOCK):
            r0 = rb * CONV_ROW_BLOCK
            acc = jnp.zeros((CONV_ROW_BLOCK, d_b), F32) + bdw
            for k in range(K_B):
                acc = acc + wdw[k:k + 1] * conv_ref[s, r0 + off + k:r0 + off + k + CONV_ROW_BLOCK, :]
            cout_ref[s * rows + r0:s * rows + r0 + CONV_ROW_BLOCK, :] = acc
        nb_ref[s] = conv_ref[s, rows + off:rows + CONV_PAD, :]
        conv_ref[s, 0:CONV_PAD, :] = conv_ref[s, rows:rows + CONV_PAD, :]
    y_b = jax.nn.silu(_layer_norm(cout_ref[...], lncg_ref[...], lncb_ref[...]))

    y = _dot(jnp.concatenate([y_a, y_b], axis=1).astype(BF16), wout_ref[...])
    y_ref[...] = (x + y).reshape(y_ref.shape)


def _even_layer(x, conv_hist, g_norm, w_in, ln_v_g, ln_v_b, w_s, b_s, w_dw, b_dw, ln_c_g, ln_c_b,
                w_out, *, n_streams, rows, emit_v):
    b, t, d = x.shape
    d_b = w_dw.shape[-1]
    d_a = ln_v_g.shape[-1]
    dg = d_a // G_A
    sgu_n = min(t, SGU_CHUNK)
    assert b % n_streams == 0 and t % rows == 0 and rows % sgu_n == 0 and (n_streams * rows) % sgu_n == 0
    assert rows % CONV_ROW_BLOCK == 0 and rows >= CONV_PAD and d_a == d_b
    bs_full = jnp.repeat(b_s[:, :sgu_n].T, dg, axis=1)
    args = [x, conv_hist, g_norm.reshape(1, d), w_in.astype(BF16), ln_v_g.reshape(1, d_a),
            ln_v_b.reshape(1, d_a), w_s[:, :sgu_n, :sgu_n], bs_full, w_dw, b_dw.reshape(1, d_b),
            ln_c_g.reshape(1, d_b), ln_c_b.reshape(1, d_b), w_out.astype(BF16)]
    in_specs = [
        pl.BlockSpec((n_streams, rows, d), lambda i, j: (i, j, 0)),
        pl.BlockSpec((n_streams, K_B - 1, d_b), lambda i, j: (i, 0, 0)),
    ] + [_const_spec(a.shape) for a in args[2:]]
    out_specs = [pl.BlockSpec((n_streams, rows, d), lambda i, j: (i, j, 0))]
    out_shape = [jax.ShapeDtypeStruct((b, t, d), F32)]
    if emit_v:
        out_specs.append(pl.BlockSpec((n_streams, rows, d_a), lambda i, j: (i, j, 0)))
        out_shape.append(jax.ShapeDtypeStruct((b, t, d_a), F32))
    out_specs.append(pl.BlockSpec((n_streams, K_B - 1, d_b), lambda i, j: (i, 0, 0)))
    out_shape.append(jax.ShapeDtypeStruct((b, K_B - 1, d_b), F32))
    kern = functools.partial(_even_kernel, n_streams=n_streams, rows=rows, sgu_n=sgu_n, emit_v=emit_v)
    return pl.pallas_call(
        kern,
        grid=(b // n_streams, t // rows),
        in_specs=in_specs,
        out_specs=out_specs,
        out_shape=out_shape,
        scratch_shapes=[pltpu.VMEM((n_streams, CONV_PAD + rows, d_b), F32),
                        pltpu.VMEM((n_streams * rows, d_b), F32)],
        compiler_params=pltpu.CompilerParams(
            dimension_semantics=("arbitrary", "arbitrary"),
            vmem_limit_bytes=V7X_VMEM_LIMIT_BYTES),
        name="even_layer",
    )(*args)


POOL_PAD = 16
ATTN_HALF_ROWS = 256
SEARCH_GROUP = 16
SEARCH_ROWS = 16
INT_MIN = -2 ** 31
NEG_INF_KEY = -2 ** 31 + 0x7FFFFF
LOG2E = math.log2(math.e)
M_INIT = -3.0e38


def _t5_bucket(rel):
    nb = N_BUCKETS // 2
    exact = nb // 2
    side = jnp.where(rel > 0, nb, 0)
    n = jnp.abs(rel)
    large = exact + (jnp.log(jnp.maximum(n, 1).astype(jnp.float32) / exact)
                     / math.log(MAX_DIST / exact) * (nb - exact)).astype(jnp.int32)
    large = jnp.minimum(large, nb - 1)
    return side + jnp.where(n < exact, n, large)


def _sortable(score):
    bits = pltpu.bitcast(score, jnp.int32)
    return bits ^ ((bits >> 31) & 0x7FFFFFFF)


def _odd_kernel(*refs, rows, qb, hist_len, n_valid_hist, top, has_hist):
    refs = list(refs)
    x_ref = refs.pop(0)
    if has_hist:
        kkh_ref, kk2h_ref, vvh_ref = refs[:3]
        refs = refs[3:]
    (poolh_ref, g_ref, win_ref, wpool_ref, spool_ref, wout_ref, bucket_ref, rb_ref, tri_ref,
     y_ref, k_ref, v_ref, ki_ref, np_ref,
     kk_ref, kk2_ref, vv_ref, keys_ref, planes_ref, acc_ref, m_ref, off_ref, thr_ref, need_ref,
     bias_ref, pool_ref, yc_ref, qe_ref, qo_ref, ie_ref, io_ref, wb_ref) = refs
    d_model = x_ref.shape[-1]
    d_c = H_C * HD_C
    d_qi = H_I * D_I
    d_d = wpool_ref.shape[0]
    n_pairs = H_C // 2
    n_ipairs = H_I // 2
    kt = KEY_TILE
    lanes = V7X_LANES
    t = pl.program_id(1)
    t0 = t * rows
    hist_tiles = hist_len // kt
    col_xd = d_c + d_qi
    col_kv = col_xd + d_d
    col_kw = col_kv + lanes

    @pl.when(jnp.logical_and(pl.program_id(0) == 0, t == 0))
    def _():
        bucket = bucket_ref[...]
        for h in range(H_C):
            b_acc = jnp.zeros(bucket.shape, F32)
            for b in range(N_BUCKETS):
                b_acc = jnp.where(bucket == b, rb_ref[b, h] * LOG2E, b_acc)
            bias_ref[:, h] = b_acc

    @pl.when(t == 0)
    def _():
        pool_ref[POOL_PAD - POOL_HIST:POOL_PAD, :] = poolh_ref[0]
        if has_hist:
            for j in range(hist_tiles):
                kk_ref[j] = kkh_ref[0, j * kt:(j + 1) * kt, :]
                kk2_ref[j] = kk2h_ref[0, j * kt:(j + 1) * kt, :]
                vv_ref[j] = vvh_ref[0, j * kt:(j + 1) * kt, :]

    x = x_ref[0]
    xn = _rms(x, g_ref[...]).astype(BF16)
    z = _dot(xn, win_ref[...])

    g1 = z[:, col_kv:col_kv + lanes]
    g2 = z[:, col_kw:col_kw + lanes]
    k_ref[0] = g1[:, :HD_C]
    v_ref[0] = g1[:, HD_C:]
    ki_ref[0] = g2[:, :D_I]
    lane = lax.broadcasted_iota(jnp.int32, (rows, lanes), 1)
    low = lane < HD_C
    g1r = pltpu.roll(g1, HD_C, 1)
    g2r = pltpu.roll(g2, D_I, 1)
    kk_new = jnp.where(low, g1, g2r).astype(BF16)
    kk2_new = jnp.where(low, g2, g1r).astype(BF16)
    vv_new = jnp.where(low, g1r, jnp.where(lane == HD_C, 1.0, 0.0)).astype(BF16)
    if rows % kt == 0:
        base_tile = (hist_len + t0) // kt
        for i in range(rows // kt):
            kk_ref[base_tile + i] = kk_new[i * kt:(i + 1) * kt]
            kk2_ref[base_tile + i] = kk2_new[i * kt:(i + 1) * kt]
            vv_ref[base_tile + i] = vv_new[i * kt:(i + 1) * kt]
    else:
        zpad = jnp.zeros((kt - rows, lanes), BF16)
        kk_ref[hist_tiles] = jnp.concatenate([kk_new, zpad], axis=0)
        kk2_ref[hist_tiles] = jnp.concatenate([kk2_new, zpad], axis=0)
        vv_ref[hist_tiles] = jnp.concatenate([vv_new, zpad], axis=0)
    kv_len = hist_len + t0 + rows

    w_idx = g2[:, D_I:D_I + H_I] * ((H_I ** -0.5) * (D_I ** -0.5))
    tri = tri_ref[...]
    ones_rhs = jnp.ones((kt, lanes), BF16)
    hb = min(rows, ATTN_HALF_ROWS)
    n_half = rows // hb
    n_new = max(rows // kt, 1)
    low_h = lax.broadcasted_iota(jnp.int32, (hb, lanes), 1) < HD_C

    jb = (hist_len + t0) // kt
    n_tiles = jb + n_new

    for hf in range(n_half):
        zr = z[hf * hb:(hf + 1) * hb]
        for g in range(n_pairs):
            grp = zr[:, g * lanes:(g + 1) * lanes] * (HD_C ** -0.5 * LOG2E)
            qe_ref[(hf * n_pairs + g) * hb:(hf * n_pairs + g + 1) * hb] = (
                jnp.where(low_h, grp, 0.0).astype(BF16))
            qo_ref[(hf * n_pairs + g) * hb:(hf * n_pairs + g + 1) * hb] = (
                jnp.where(low_h, 0.0, grp).astype(BF16))
    for g in range(n_ipairs):
        grp = z[:, d_c + g * lanes:d_c + (g + 1) * lanes]
        ie_ref[g * rows:(g + 1) * rows] = jnp.where(low, grp, 0.0).astype(BF16)
        io_ref[g * rows:(g + 1) * rows] = jnp.where(low, 0.0, grp).astype(BF16)
    for h in range(H_I):
        wb_ref[h] = jnp.broadcast_to(w_idx[:, h:h + 1], (rows, lanes))
    qchunk = (hist_len + t0 + lax.broadcasted_iota(jnp.int32, (rows, 1), 0)) >> CHUNK_SHIFT
    kpos_rel = lax.broadcasted_iota(jnp.int32, (rows, kt), 1)

    def score_body(j, carry):
        se = jnp.maximum(_dot_nt(ie_ref[...], kk2_ref[j]), 0.0)
        so = jnp.maximum(_dot_nt(io_ref[...], kk_ref[j]), 0.0)
        score = jnp.zeros((rows, kt), F32)
        for g in range(n_ipairs):
            we = wb_ref[2 * g]
            wo = wb_ref[2 * g + 1]
            score = score + se[g * rows:(g + 1) * rows] * jnp.concatenate([we] * (kt // lanes), axis=1)
            score = score + so[g * rows:(g + 1) * rows] * jnp.concatenate([wo] * (kt // lanes), axis=1)
        kpos = kpos_rel + j * kt
        adm = jnp.logical_and((kpos >> CHUNK_SHIFT) <= qchunk, kpos < kv_len)
        keys_ref[j] = _sortable(jnp.where(adm, score, NEG_INF))
        return carry
    lax.fori_loop(0, n_tiles, score_body, 0)

    def fill_body(j, carry):
        keys_ref[j] = jnp.full((rows, kt), INT_MIN, jnp.int32)
        return carry
    lax.fori_loop(n_tiles, keys_ref.shape[0], fill_body, 0)

    n_groups = keys_ref.shape[0] * (kt // lanes) // SEARCH_GROUP

    def plane_body(rc, carry):
        r = pl.multiple_of(rc * SEARCH_ROWS, SEARCH_ROWS)
        for gi in range(n_groups):
            a = []
            for i in range(SEARCH_GROUP):
                lt = gi * SEARCH_GROUP + i
                a.append(keys_ref[lt // (kt // lanes), pl.ds(r, SEARCH_ROWS),
                                  (lt % (kt // lanes)) * lanes:(lt % (kt // lanes) + 1) * lanes])
            for sh, msk in ((8, 0x00FF00FF), (4, 0x0F0F0F0F), (2, 0x33333333), (1, 0x55555555)):
                for k in range(SEARCH_GROUP):
                    if k & sh == 0:
                        tmp = (a[k] ^ lax.shift_right_logical(a[k + sh], sh)) & msk
                        a[k] = a[k] ^ tmp
                        a[k + sh] = a[k + sh] ^ (tmp << sh)
            for w in range(SEARCH_GROUP):
                planes_ref[gi, w, pl.ds(r, SEARCH_ROWS), :] = a[w]
        return carry
    lax.fori_loop(0, rows // SEARCH_ROWS, plane_body, 0)

    alive = [jnp.full((rows, lanes), -65536, jnp.int32) for _ in range(n_groups)]
    above = jnp.zeros((rows, 1), F32)
    thr_u = jnp.zeros((rows, 1), jnp.int32)
    for b in range(31, -1, -1):
        w = (31 - b) if b >= 16 else (15 - b)
        if b == 15:
            alive = [lax.shift_right_logical(a, 16) for a in alive]
        ones = []
        for gi in range(n_groups):
            plane = planes_ref[gi, w]
            if b == 31:
                plane = ~plane
            ones.append(alive[gi] & plane)
        pc = lax.population_count(ones[0])
        for gi in range(1, n_groups):
            pc = pc + lax.population_count(ones[gi])
        cnt = jnp.sum(pc.astype(F32), axis=1, keepdims=True)
        take = (above + cnt) >= float(top)
        alive = [jnp.where(take, o, a ^ o) for a, o in zip(alive, ones)]
        above = jnp.where(take, above, above + cnt)
        thr_u = jnp.where(take, thr_u | jnp.int32(INT_MIN if b == 31 else (1 << b)), thr_u)
    thr_ref[...] = jnp.broadcast_to(thr_u ^ jnp.int32(INT_MIN), (rows, lanes))
    need_ref[...] = jnp.broadcast_to(float(top) - above, (rows, lanes))
    off_ref[...] = jnp.zeros((rows, lanes), F32)
    m_ref[...] = jnp.full(m_ref.shape, M_INIT, F32)
    acc_ref[...] = jnp.zeros(acc_ref.shape, F32)

    def attn_tile(j, halves, dj):
        for hf in halves:
            rs = slice(hf * hb, (hf + 1) * hb)
            kj = keys_ref[j, rs]
            thr_t = jnp.concatenate([thr_ref[rs]] * (kt // lanes), axis=1)
            need_t = jnp.concatenate([need_ref[rs]] * (kt // lanes), axis=1)
            eq = jnp.where(kj == thr_t, 1.0, 0.0)
            eq_b = eq.astype(BF16)
            rank = _dot(eq_b, tri) + jnp.concatenate([off_ref[rs]] * (kt // lanes), axis=1)
            self_ = jnp.where(kj > thr_t, 1.0, jnp.where(rank <= need_t, eq, 0.0))
            sel = jnp.where(kj > NEG_INF_KEY, self_, 0.0) > 0.5
            mask_add = jnp.where(sel, 0.0, NEG_INF)
            off_ref[rs] = off_ref[rs] + _dot(eq_b, ones_rhs)
            q_rows = slice(hf * n_pairs * hb, (hf + 1) * n_pairs * hb)
            lg = (_dot_nt(qe_ref[q_rows], kk_ref[j]), _dot_nt(qo_ref[q_rows], kk2_ref[j]))
            ps = []
            alphas = []
            for eo in range(2):
                for g in range(n_pairs):
                    h = 2 * g + eo
                    l = lg[eo][g * hb:(g + 1) * hb]
                    if dj is not None:
                        row_parts = []
                        for sub in range(hb // qb):
                            sblk = ((hf * (hb // qb) + sub) * qb) // lanes
                            parts = []
                            for c in range(kt // lanes):
                                part = l[sub * qb:(sub + 1) * qb, c * lanes:(c + 1) * lanes]
                                d = (kt // lanes) * dj + c - sblk
                                if d in (-1, 0):
                                    part = part + bias_ref[d + 1, h]
                                parts.append(part)
                            row_parts.append(jnp.concatenate(parts, axis=1))
                        l = row_parts[0] if len(row_parts) == 1 else jnp.concatenate(row_parts, axis=0)
                    l = l + mask_add
                    st = slice(((hf * 2 + eo) * n_pairs + g) * hb, ((hf * 2 + eo) * n_pairs + g + 1) * hb)
                    m_old = m_ref[st]
                    m_new = jnp.maximum(m_old, jnp.max(l, axis=1, keepdims=True))
                    alphas.append(jnp.exp2(m_old - m_new))
                    ps.append(jnp.exp2(l - jnp.concatenate([m_new] * (kt // lanes), axis=1)).astype(BF16))
                    m_ref[st] = m_new
            a_rows = slice(hf * H_C * hb, (hf + 1) * H_C * hb)
            acc_ref[a_rows] = (acc_ref[a_rows] * jnp.concatenate(alphas, axis=0)
                               + _dot(jnp.concatenate(ps, axis=0), vv_ref[j]))

    all_halves = tuple(range(n_half))

    def far_body(j, carry):
        attn_tile(j, all_halves, None)
        return carry
    lax.fori_loop(0, jb - 1, far_body, 0)

    @pl.when(jb >= 1)
    def _():
        attn_tile(jb - 1, all_halves, -1)
    for dj in range(n_new):
        halves = tuple(hf for hf in all_halves
                       if (((hf + 1) * hb - qb) // lanes) >= (kt // lanes) * dj)
        attn_tile(jb + dj, halves, dj)

    for hf in range(n_half):
        for g in range(n_pairs):
            oe = acc_ref[((hf * 2) * n_pairs + g) * hb:((hf * 2) * n_pairs + g + 1) * hb]
            oo = acc_ref[((hf * 2 + 1) * n_pairs + g) * hb:((hf * 2 + 1) * n_pairs + g + 1) * hb]
            oe = oe / oe[:, HD_C:HD_C + 1]
            oo = oo / oo[:, HD_C:HD_C + 1]
            yc_ref[hf * hb:(hf + 1) * hb, g * lanes:(g + 1) * lanes] = jnp.where(
                low_h, oe, pltpu.roll(oo, HD_C, 1))

    xd = z[:, col_xd:col_xd + d_d]
    pool_ref[POOL_PAD:POOL_PAD + rows, :] = xd
    np_ref[0] = pool_ref[rows + POOL_PAD - POOL_HIST:rows + POOL_PAD, :]
    run = xd
    wins = {}
    for dshift in range(1, POOL_WINDOWS[-1]):
        run = run + pool_ref[POOL_PAD - dshift:POOL_PAD - dshift + rows, :]
        if dshift + 1 in POOL_WINDOWS:
            wins[dshift + 1] = run
    pool_ref[0:POOL_PAD, :] = pool_ref[rows:rows + POOL_PAD, :]
    dg_d = d_d // len(POOL_WINDOWS)
    lane_d = lax.broadcasted_iota(jnp.int32, (rows, d_d), 1)
    tpos = n_valid_hist + t0 + 1 + lax.broadcasted_iota(jnp.int32, (rows, d_d), 0)
    win_sum = wins[POOL_WINDOWS[-1]]
    width = jnp.full((rows, d_d), POOL_WINDOWS[-1], jnp.int32)
    for gi in range(len(POOL_WINDOWS) - 2, -1, -1):
        in_g = lane_d < (gi + 1) * dg_d
        win_sum = jnp.where(in_g, wins[POOL_WINDOWS[gi]], win_sum)
        width = jnp.where(in_g, POOL_WINDOWS[gi], width)
    count = jnp.minimum(tpos, width).astype(F32)
    m_pool = win_sum / count - xd
    y_d = _dot(m_pool.astype(BF16), wpool_ref[...]) * spool_ref[...]

    y_cat = jnp.concatenate([yc_ref[...], y_d], axis=1).astype(BF16)
    y_ref[0] = x + _dot(y_cat, wout_ref[...])


def _odd_layer(x, k_hist, v_hist, ki_hist, pool_hist, n_valid_hist, g_norm, w_in, w_pool, s_pool,
               w_out, rel_bias, *, rows):
    b, t, d = x.shape
    hist_len = k_hist.shape[1]
    has_hist = hist_len > 0
    d_c = H_C * HD_C
    d_qi = H_I * D_I
    d_d = d - d_c
    qb = min(2 * CHUNK, rows)
    kt = KEY_TILE
    lanes = V7X_LANES
    s_total = hist_len + t
    top = min(TOPK_MAX, s_total // 4)
    assert t % rows == 0 and rows % qb == 0 and hist_len % kt == 0
    assert rows % kt == 0 or (rows == t and rows == qb and rows <= CHUNK)
    n_tiles = (s_total + kt - 1) // kt
    tiles_per_group = SEARCH_GROUP * lanes // kt
    n_groups = (n_tiles + tiles_per_group - 1) // tiles_per_group
    n_tiles_pad = n_groups * tiles_per_group
    assert rows % SEARCH_ROWS == 0 and rows % min(rows, ATTN_HALF_ROWS) == 0

    offs = np.cumsum([0, d_c, HD_C, HD_C, d_qi, D_I, H_I]).tolist()
    q_w, k_w, v_w, qi_w, ki_w, wi_w = (w_in[:, offs[i]:offs[i + 1]] for i in range(6))
    xd_w = w_in[:, offs[6]:]
    pad_w = jnp.zeros((d, lanes - D_I - H_I), w_in.dtype)
    w_all = jnp.concatenate([q_w, qi_w, xd_w, k_w, v_w, ki_w, wi_w, pad_w], axis=1).astype(BF16)
    wpool_bd = jax.scipy.linalg.block_diag(*[w_pool[g] for g in range(w_pool.shape[0])]).astype(BF16)

    rel = (lanes * jnp.arange(-1, 1, dtype=jnp.int32)[:, None, None]
           + jnp.arange(lanes, dtype=jnp.int32)[None, None, :]
           - jnp.arange(qb, dtype=jnp.int32)[None, :, None])
    bucket = _t5_bucket(rel)
    far_bucket = _t5_bucket(jnp.int32(-2 * lanes))
    rb_shift = rel_bias - rel_bias[far_bucket][None, :]
    tri = (jnp.arange(kt)[:, None] <= jnp.arange(kt)[None, :]).astype(BF16)

    args = [x]
    in_specs = [pl.BlockSpec((1, rows, d), lambda i, j: (i, j, 0))]
    if has_hist:
        ones = jnp.ones((b, hist_len, 1), F32)
        zeros = jnp.zeros((b, hist_len, lanes - HD_C - 1), F32)
        args += [jnp.concatenate([k_hist, ki_hist], axis=-1).astype(BF16),
                 jnp.concatenate([ki_hist, k_hist], axis=-1).astype(BF16),
                 jnp.concatenate([v_hist, ones, zeros], axis=-1).astype(BF16)]
        in_specs += [pl.BlockSpec((1, hist_len, lanes), lambda i, j: (i, 0, 0))] * 3
    consts = [g_norm.reshape(1, d), w_all, wpool_bd, s_pool.reshape(1, d_d), w_out.astype(BF16), bucket]
    args += [pool_hist] + consts + [rb_shift, tri]
    in_specs += ([pl.BlockSpec((1, POOL_HIST, d_d), lambda i, j: (i, 0, 0))]
                 + [_const_spec(a.shape) for a in consts]
                 + [pl.BlockSpec(memory_space=pltpu.SMEM), _const_spec(tri.shape)])
    kern = functools.partial(_odd_kernel, rows=rows, qb=qb, hist_len=hist_len,
                             n_valid_hist=n_valid_hist, top=top, has_hist=has_hist)
    return pl.pallas_call(
        kern,
        grid=(b, t // rows),
        in_specs=in_specs,
        out_specs=[
            pl.BlockSpec((1, rows, d), lambda i, j: (i, j, 0)),
            pl.BlockSpec((1, rows, HD_C), lambda i, j: (i, j, 0)),
            pl.BlockSpec((1, rows, HD_C), lambda i, j: (i, j, 0)),
            pl.BlockSpec((1, rows, D_I), lambda i, j: (i, j, 0)),
            pl.BlockSpec((1, POOL_HIST, d_d), lambda i, j: (i, 0, 0)),
        ],
        out_shape=[jax.ShapeDtypeStruct((b, t, d), F32),
                   jax.ShapeDtypeStruct((b, t, HD_C), F32),
                   jax.ShapeDtypeStruct((b, t, HD_C), F32),
                   jax.ShapeDtypeStruct((b, t, D_I), F32),
                   jax.ShapeDtypeStruct((b, POOL_HIST, d_d), F32)],
        scratch_shapes=[
            pltpu.VMEM((n_tiles, kt, lanes), BF16),
            pltpu.VMEM((n_tiles, kt, lanes), BF16),
            pltpu.VMEM((n_tiles, kt, lanes), BF16),
            pltpu.VMEM((n_tiles_pad, rows, kt), jnp.int32),
            pltpu.VMEM((n_groups, SEARCH_GROUP, rows, lanes), jnp.int32),
            pltpu.VMEM((H_C * rows, lanes), F32),
            pltpu.VMEM((H_C * rows, lanes), F32),
            pltpu.VMEM((rows, lanes), F32),
            pltpu.VMEM((rows, lanes), jnp.int32),
            pltpu.VMEM((rows, lanes), F32),
            pltpu.VMEM((2, H_C, qb, lanes), F32),
            pltpu.VMEM((POOL_PAD + rows, d_d), F32),
            pltpu.VMEM((rows, d_c), F32),
            pltpu.VMEM((H_C // 2 * rows, lanes), BF16),
            pltpu.VMEM((H_C // 2 * rows, lanes), BF16),
            pltpu.VMEM((H_I // 2 * rows, lanes), BF16),
            pltpu.VMEM((H_I // 2 * rows, lanes), BF16),
            pltpu.VMEM((H_I, rows, lanes), F32),
        ],
        compiler_params=pltpu.CompilerParams(
            dimension_semantics=("arbitrary", "arbitrary"),
            vmem_limit_bytes=V7X_VMEM_LIMIT_BYTES),
        name="odd_layer",
    )(*args)


def kernel(x_prompt, x_sample, cache_b_conv, cache_c_k, cache_c_v, cache_c_kidx, cache_d_pool, cache_ffn_conv, ln_mix, ln_ffn, ln_final, e_w_in, e_ln_v_g, e_ln_v_b, e_w_s, e_b_s, e_w_dw, e_b_dw, e_ln_c_g, e_ln_c_b, e_w_out, o_w_in, o_w_pool, o_s_pool, o_w_out, rel_bias, f_w_up, f_w_dw, f_b_dw, f_w_down):
    hp, hs = x_prompt, x_sample
    bp, bs = x_prompt.shape[0], x_sample.shape[0]
    ts = x_sample.shape[1]
    depth = ln_mix.shape[0]
    d_ff = f_w_down.shape[1]
    a_s_l, b_p_l, b_s_l = [], [], []
    ck_p_l, cv_p_l, cki_p_l, ck_s_l, cv_s_l, cki_s_l, d_p_l, d_s_l = [], [], [], [], [], [], [], []
    f_p_l, f_s_l = [], []
    for layer in range(depth):
        i = layer // 2
        if layer % 2 == 0:
            ew = (ln_mix[layer], e_w_in[i], e_ln_v_g[i], e_ln_v_b[i], e_w_s[i], e_b_s[i], e_w_dw[i],
                  e_b_dw[i], e_ln_c_g[i], e_ln_c_b[i], e_w_out[i])
            hp, b_p = _even_layer(hp, jnp.zeros((bp, K_B - 1, e_w_dw.shape[-1]), F32), *ew,
                                  n_streams=1, rows=PROMPT_ROWS, emit_v=False)
            hs, a_s, b_s = _even_layer(hs, cache_b_conv[i], *ew, n_streams=bs, rows=ts, emit_v=True)
            a_s_l.append(a_s); b_p_l.append(b_p); b_s_l.append(b_s)
        else:
            ow = (ln_mix[layer], o_w_in[i], o_w_pool[i], o_s_pool[i], o_w_out[i], rel_bias)
            d_d = o_w_pool.shape[1] * o_w_pool.shape[2]
            hp, k_p, v_p, ki_p, d_p = _odd_layer(
                hp, jnp.zeros((bp, 0, HD_C), F32), jnp.zeros((bp, 0, HD_C), F32),
                jnp.zeros((bp, 0, D_I), F32), jnp.zeros((bp, POOL_HIST, d_d), F32), 0, *ow,
                rows=PROMPT_ROWS)
            hs, k_s, v_s, ki_s, d_s = _odd_layer(
                hs, cache_c_k[i], cache_c_v[i], cache_c_kidx[i], cache_d_pool[i], POOL_HIST, *ow,
                rows=ts)
            ck_p_l.append(k_p); cv_p_l.append(v_p); cki_p_l.append(ki_p)
            ck_s_l.append(k_s); cv_s_l.append(v_s); cki_s_l.append(ki_s)
            d_p_l.append(d_p); d_s_l.append(d_s)
        g_final = ln_final if layer == depth - 1 else None
        fw = (ln_ffn[layer], f_w_up[layer], f_w_dw[layer], f_b_dw[layer], f_w_down[layer], g_final)
        hp, f_p = _conv_ffn(hp, jnp.zeros((bp, K_FFN - 1, d_ff), F32), *fw,
                            n_streams=1, rows=PROMPT_ROWS, n_chunks=FFN_CHUNKS)
        hs, f_s = _conv_ffn(hs, cache_ffn_conv[layer], *fw, n_streams=bs, rows=ts, n_chunks=FFN_CHUNKS)
        f_p_l.append(f_p); f_s_l.append(f_s)
    return (hp, hs,
            jnp.stack(a_s_l), jnp.stack(b_p_l), jnp.stack(b_s_l),
            jnp.stack(ck_p_l), jnp.stack(cv_p_l), jnp.stack(cki_p_l),
            jnp.stack(ck_s_l), jnp.stack(cv_s_l), jnp.stack(cki_s_l),
            jnp.stack(d_p_l), jnp.stack(d_s_l),
            jnp.stack(f_p_l), jnp.stack(f_s_l))
```

```python
import functools
import math

import jax
import jax.numpy as jnp
import numpy as np
from jax import lax
from jax.experimental import pallas as pl
from jax.experimental.pallas import tpu as pltpu

F32 = jnp.float32
BF16 = jnp.bfloat16

EPS = 1e-6
CHUNK = 64
CHUNK_SHIFT = CHUNK.bit_length() - 1
SGU_CHUNK = 128
G_A = 4
K_B = 31
H_C = 12
HD_C = 64
H_I = 8
D_I = 64
TOPK_MAX = 256
N_BUCKETS = 32
MAX_DIST = 128
POOL_WINDOWS = (2, 4, 8, 16)
POOL_HIST = 15
K_FFN = 3

V7X_LANES = 128
V7X_SUBLANES = 8
V7X_MXU_DIM = 256
V7X_VMEM_LIMIT_BYTES = 56 * 1024 * 1024

KEY_TILE = V7X_MXU_DIM
PROMPT_ROWS = 512
FFN_CHUNKS = 2
NEG_INF = float("-inf")


def _rms(x, g):
    return x * lax.rsqrt(jnp.mean(x * x, axis=-1, keepdims=True) + EPS) * g


def _layer_norm(x, g, b):
    mu = jnp.mean(x, axis=-1, keepdims=True)
    xc = x - mu
    return xc * lax.rsqrt(jnp.mean(xc * xc, axis=-1, keepdims=True) + EPS) * g + b


def _gelu_tanh(x):
    cdf = 0.5 * (1.0 + jnp.tanh(math.sqrt(2.0 / math.pi) * (x + 0.044715 * (x * x * x))))
    return x * cdf


def _dot(a, b):
    return jnp.dot(a, b, preferred_element_type=F32)


def _dot_nt(a, b):
    return lax.dot_general(a, b, (((1,), (1,)), ((), ())), preferred_element_type=F32)


def _const_spec(shape):
    nd = len(shape)
    return pl.BlockSpec(shape, lambda *_: (0,) * nd, pipeline_mode=pl.Buffered(1))


def _ffn_kernel(*refs, n_streams, rows, n_chunks, final_norm):
    if final_norm:
        (x_ref, hist_ref, g_ref, wu_ref, wd_ref, dw_ref, bdw_ref, gf_ref,
         y_ref, nh_ref, carry_ref) = refs
    else:
        (x_ref, hist_ref, g_ref, wu_ref, wd_ref, dw_ref, bdw_ref,
         y_ref, nh_ref, carry_ref) = refs
        gf_ref = None
    d_model = x_ref.shape[-1]
    d_ff = wd_ref.shape[0]
    fc = d_ff // n_chunks
    m = n_streams * rows

    @pl.when(pl.program_id(1) == 0)
    def _():
        for c in range(n_chunks):
            carry_ref[:, c, 6:8, :] = hist_ref[:, :, c * fc:(c + 1) * fc]

    x = x_ref[...].reshape(m, d_model)
    xn = _rms(x, g_ref[...]).astype(BF16)
    acc = jnp.zeros((m, d_model), F32)
    for c in range(n_chunks):
        cols = slice(c * fc, (c + 1) * fc)
        a = _dot(xn, wu_ref[:, cols])
        val = _dot(xn, wu_ref[:, d_ff + c * fc:d_ff + (c + 1) * fc])
        w = dw_ref[:, cols]
        ys = []
        for s in range(n_streams):
            a_s = a[s * rows:(s + 1) * rows]
            ext = jnp.concatenate([carry_ref[s, c], a_s], axis=0)
            ys.append(w[0:1] * ext[6:6 + rows] + w[1:2] * ext[7:7 + rows] + w[2:3] * a_s)
            carry_ref[s, c] = a_s[rows - 8:rows]
            nh_ref[s, :, c * fc:(c + 1) * fc] = a_s[rows - 2:rows]
        y = (ys[0] if n_streams == 1 else jnp.concatenate(ys, axis=0)) + bdw_ref[:, cols]
        acc = acc + _dot((_gelu_tanh(y) * val).astype(BF16), wd_ref[cols, :])
    out = x + acc
    if final_norm:
        out = _rms(out, gf_ref[...])
    y_ref[...] = out.reshape(y_ref.shape)


def _conv_ffn(x, hist, g_norm, w_up, w_dw, b_dw, w_down, g_final, *, n_streams, rows, n_chunks):
    b, t, d = x.shape
    d_ff = w_down.shape[0]
    fc = d_ff // n_chunks
    assert fc * n_chunks == d_ff and fc % V7X_LANES == 0
    assert b % n_streams == 0 and t % rows == 0 and rows % V7X_SUBLANES == 0
    final_norm = g_final is not None
    args = [x, hist, g_norm.reshape(1, d), w_up.astype(BF16), w_down.astype(BF16), w_dw,
            b_dw.reshape(1, d_ff)]
    in_specs = [
        pl.BlockSpec((n_streams, rows, d), lambda i, j: (i, j, 0)),
        pl.BlockSpec((n_streams, K_FFN - 1, d_ff), lambda i, j: (i, 0, 0)),
    ] + [_const_spec(a.shape) for a in args[2:]]
    if final_norm:
        args.append(g_final.reshape(1, d))
        in_specs.append(_const_spec((1, d)))
    kern = functools.partial(_ffn_kernel, n_streams=n_streams, rows=rows, n_chunks=n_chunks,
                             final_norm=final_norm)
    return pl.pallas_call(
        kern,
        grid=(b // n_streams, t // rows),
        in_specs=in_specs,
        out_specs=[
            pl.BlockSpec((n_streams, rows, d), lambda i, j: (i, j, 0)),
            pl.BlockSpec((n_streams, K_FFN - 1, d_ff), lambda i, j: (i, 0, 0)),
        ],
        out_shape=[jax.ShapeDtypeStruct((b, t, d), F32),
                   jax.ShapeDtypeStruct((b, K_FFN - 1, d_ff), F32)],
        scratch_shapes=[pltpu.VMEM((n_streams, n_chunks, 8, fc), F32)],
        compiler_params=pltpu.CompilerParams(
            dimension_semantics=("arbitrary", "arbitrary"),
            vmem_limit_bytes=V7X_VMEM_LIMIT_BYTES),
        name="conv_ffn",
    )(*args)


CONV_PAD = 32
CONV_ROW_BLOCK = 32
CONV_PITCH_PAD = 4


def _conv_rows_transposed(glu, s, hist_ref, wdw, bdw, stage_in_ref, stage_out_ref, prev_ref, cout_ref,
                          *, rows, first):
    sub = V7X_SUBLANES
    lanes = V7X_LANES
    nv = rows // sub
    pitch = nv + CONV_PITCH_PAD
    hist_rows = K_B - 1
    d_b = glu.shape[-1]
    sub_id = lax.broadcasted_iota(jnp.int32, (sub, lanes), 0)
    for lt in range(d_b // lanes):
        cols = slice(lt * lanes, (lt + 1) * lanes)

        @pl.when(first)
        def _():
            for e in range(hist_rows):
                prev_ref[s, lt, e, sub - 1:sub, :] = hist_ref[s, e:e + 1, cols]
        for q in range(sub):
            stage_in_ref[lt, q * pitch:q * pitch + nv, :] = (
                glu[s * rows + q * nv:s * rows + (q + 1) * nv, cols])
        cur = [stage_in_ref[lt, pl.ds(v, sub, stride=pitch), :] for v in range(nv)]
        head = []
        for e in range(hist_rows):
            merged = jnp.where(sub_id == sub - 1, prev_ref[s, lt, e], cur[nv - hist_rows + e])
            head.append(pltpu.roll(merged, 1, 0))
        for e in range(hist_rows):
            prev_ref[s, lt, e] = cur[nv - hist_rows + e]
        ext = head + cur
        wk = [jnp.broadcast_to(wdw[k:k + 1, cols], (sub, lanes)) for k in range(K_B)]
        bias = jnp.broadcast_to(bdw[:, cols], (sub, lanes))
        for v in range(nv):
            acc = bias
            for k in range(K_B):
                acc = acc + wk[k] * ext[v + k]
            stage_out_ref[lt, pl.ds(v, sub, stride=pitch), :] = acc
        for q in range(sub):
            cout_ref[s * rows + q * nv:s * rows + (q + 1) * nv, cols] = (
                stage_out_ref[lt, q * pitch:q * pitch + nv, :])


def _even_kernel(*refs, n_streams, rows, sgu_n, emit_v, transposed_conv):
    (x_ref, hist_ref, g_ref, win_ref, lnvg_ref, lnvb_ref, ws_ref, bs_ref, wdw_ref, bdw_ref,
     lncg_ref, lncb_ref, wout_ref) = refs[:13]
    n_out = 3 if emit_v else 2
    y_ref = refs[13]
    av_ref = refs[14] if emit_v else None
    nb_ref = refs[13 + n_out - 1]
    if transposed_conv:
        stage_in_ref, stage_out_ref, prev_ref, cout_ref = refs[13 + n_out:]
    else:
        conv_ref, cout_ref = refs[13 + n_out:]
    d_model = x_ref.shape[-1]
    d_b = wdw_ref.shape[-1]
    d_a = d_b
    dg = d_a // G_A
    m = n_streams * rows
    hist_rows = K_B - 1
    off = CONV_PAD - hist_rows

    if not transposed_conv:
        @pl.when(pl.program_id(1) == 0)
        def _():
            conv_ref[:, off:CONV_PAD, :] = hist_ref[...]

    x = x_ref[...].reshape(m, d_model)
    xn = _rms(x, g_ref[...]).astype(BF16)
    z = _dot(xn, win_ref[...])

    za = jax.nn.gelu(z[:, :2 * d_a], approximate=True)
    u = za[:, :d_a]
    lnvg = lnvg_ref[...]
    lnvb = lnvb_ref[...]
    v = jnp.concatenate(
        [_layer_norm(za[:, d_a + g * dg:d_a + (g + 1) * dg], lnvg[:, g * dg:(g + 1) * dg],
                     lnvb[:, g * dg:(g + 1) * dg]) for g in range(G_A)], axis=1)
    if emit_v:
        av_ref[...] = v.reshape(av_ref.shape)
    vb = v.astype(BF16)
    tril = (lax.broadcasted_iota(jnp.int32, (sgu_n, sgu_n), 0)
            >= lax.broadcasted_iota(jnp.int32, (sgu_n, sgu_n), 1))
    ws = [jnp.where(tril, ws_ref[g], 0.0).astype(BF16) for g in range(G_A)]
    bs = bs_ref[...]
    sg_rows = []
    for c in range(m // sgu_n):
        vc = vb[c * sgu_n:(c + 1) * sgu_n]
        sg_rows.append(jnp.concatenate(
            [_dot(ws[g], vc[:, g * dg:(g + 1) * dg]) for g in range(G_A)], axis=1) + bs)
    sg = sg_rows[0] if len(sg_rows) == 1 else jnp.concatenate(sg_rows, axis=0)
    y_a = u * sg

    glu = z[:, 2 * d_a:2 * d_a + d_b] * jax.nn.sigmoid(z[:, 2 * d_a + d_b:])
    wdw = wdw_ref[...]
    bdw = bdw_ref[...]
    for s in range(n_streams):
        if transposed_conv:
            _conv_rows_transposed(glu, s, hist_ref, wdw, bdw, stage_in_ref, stage_out_ref, prev_ref,
                                  cout_ref, rows=rows, first=pl.program_id(1) == 0)
            nb_ref[s] = glu[(s + 1) * rows - hist_rows:(s + 1) * rows]
            continue
        conv_ref[s, CONV_PAD:CONV_PAD + rows, :] = glu[s * rows:(s + 1) * rows]
        for rb in range(rows // CONV_ROW_BLOCK):
            r0 = rb * CONV_ROW_BLOCK
            acc = jnp.zeros((CONV_ROW_BLOCK, d_b), F32) + bdw
            for k in range(K_B):
                acc = acc + wdw[k:k + 1] * conv_ref[s, r0 + off + k:r0 + off + k + CONV_ROW_BLOCK, :]
            cout_ref[s * rows + r0:s * rows + r0 + CONV_ROW_BLOCK, :] = acc
        nb_ref[s] = conv_ref[s, rows + off:rows + CONV_PAD, :]
        conv_ref[s, 0:CONV_PAD, :] = conv_ref[s, rows:rows + CONV_PAD, :]
    y_b = jax.nn.silu(_layer_norm(cout_ref[...], lncg_ref[...], lncb_ref[...]))

    y = _dot(jnp.concatenate([y_a, y_b], axis=1).astype(BF16), wout_ref[...])
    y_ref[...] = (x + y).reshape(y_ref.shape)


def _even_layer(x, conv_hist, g_norm, w_in, ln_v_g, ln_v_b, w_s, b_s, w_dw, b_dw, ln_c_g, ln_c_b,
                w_out, *, n_streams, rows, emit_v):
    b, t, d = x.shape
    d_b = w_dw.shape[-1]
    d_a = ln_v_g.shape[-1]
    dg = d_a // G_A
    sgu_n = min(t, SGU_CHUNK)
    assert b % n_streams == 0 and t % rows == 0 and rows % sgu_n == 0 and (n_streams * rows) % sgu_n == 0
    assert rows % CONV_ROW_BLOCK == 0 and rows >= CONV_PAD and d_a == d_b
    bs_full = jnp.repeat(b_s[:, :sgu_n].T, dg, axis=1)
    args = [x, conv_hist, g_norm.reshape(1, d), w_in.astype(BF16), ln_v_g.reshape(1, d_a),
            ln_v_b.reshape(1, d_a), w_s[:, :sgu_n, :sgu_n], bs_full, w_dw, b_dw.reshape(1, d_b),
            ln_c_g.reshape(1, d_b), ln_c_b.reshape(1, d_b), w_out.astype(BF16)]
    in_specs = [
        pl.BlockSpec((n_streams, rows, d), lambda i, j: (i, j, 0)),
        pl.BlockSpec((n_streams, K_B - 1, d_b), lambda i, j: (i, 0, 0)),
    ] + [_const_spec(a.shape) for a in args[2:]]
    out_specs = [pl.BlockSpec((n_streams, rows, d), lambda i, j: (i, j, 0))]
    out_shape = [jax.ShapeDtypeStruct((b, t, d), F32)]
    if emit_v:
        out_specs.append(pl.BlockSpec((n_streams, rows, d_a), lambda i, j: (i, j, 0)))
        out_shape.append(jax.ShapeDtypeStruct((b, t, d_a), F32))
    out_specs.append(pl.BlockSpec((n_streams, K_B - 1, d_b), lambda i, j: (i, 0, 0)))
    out_shape.append(jax.ShapeDtypeStruct((b, K_B - 1, d_b), F32))
    transposed_conv = rows // V7X_SUBLANES >= K_B - 1
    kern = functools.partial(_even_kernel, n_streams=n_streams, rows=rows, sgu_n=sgu_n, emit_v=emit_v,
                             transposed_conv=transposed_conv)
    if transposed_conv:
        stage_shape = (d_b // V7X_LANES, V7X_SUBLANES * (rows // V7X_SUBLANES + CONV_PITCH_PAD), V7X_LANES)
        conv_scratch = [pltpu.VMEM(stage_shape, F32), pltpu.VMEM(stage_shape, F32),
                        pltpu.VMEM((n_streams, d_b // V7X_LANES, K_B - 1, V7X_SUBLANES, V7X_LANES), F32)]
    else:
        conv_scratch = [pltpu.VMEM((n_streams, CONV_PAD + rows, d_b), F32)]
    return pl.pallas_call(
        kern,
        grid=(b // n_streams, t // rows),
        in_specs=in_specs,
        out_specs=out_specs,
        out_shape=out_shape,
        scratch_shapes=conv_scratch + [pltpu.VMEM((n_streams * rows, d_b), F32)],
        compiler_params=pltpu.CompilerParams(
            dimension_semantics=("arbitrary", "arbitrary"),
            vmem_limit_bytes=V7X_VMEM_LIMIT_BYTES),
        name="even_layer",
    )(*args)


POOL_PAD = 16
ATTN_HALF_ROWS = 256
SEARCH_GROUP = 16
SEARCH_ROWS = 16
INT_MIN = -2 ** 31
NEG_INF_KEY = -2 ** 31 + 0x7FFFFF
LOG2E = math.log2(math.e)
M_INIT = -3.0e38


def _t5_bucket(rel):
    nb = N_BUCKETS // 2
    exact = nb // 2
    side = jnp.where(rel > 0, nb, 0)
    n = jnp.abs(rel)
    large = exact + (jnp.log(jnp.maximum(n, 1).astype(jnp.float32) / exact)
                     / math.log(MAX_DIST / exact) * (nb - exact)).astype(jnp.int32)
    large = jnp.minimum(large, nb - 1)
    return side + jnp.where(n < exact, n, large)


def _sortable(score):
    bits = pltpu.bitcast(score, jnp.int32)
    return bits ^ ((bits >> 31) & 0x7FFFFFFF)


def _odd_kernel(*refs, rows, qb, hist_len, n_valid_hist, top, has_hist):
    refs = list(refs)
    x_ref = refs.pop(0)
    if has_hist:
        kkh_ref, kk2h_ref, vvh_ref = refs[:3]
        refs = refs[3:]
    (poolh_ref, g_ref, win_ref, wpool_ref, spool_ref, wout_ref, bucket_ref, rb_ref, tri_ref,
     y_ref, k_ref, v_ref, ki_ref, np_ref,
     kk_ref, kk2_ref, vv_ref, keys_ref, planes_ref, acc_ref, m_ref, off_ref, thr_ref, need_ref,
     bias_ref, pool_ref, yc_ref, qe_ref, qo_ref, ie_ref, io_ref, wb_ref) = refs
    d_model = x_ref.shape[-1]
    d_c = H_C * HD_C
    d_qi = H_I * D_I
    d_d = wpool_ref.shape[0]
    n_pairs = H_C // 2
    n_ipairs = H_I // 2
    kt = KEY_TILE
    lanes = V7X_LANES
    t = pl.program_id(1)
    t0 = t * rows
    hist_tiles = hist_len // kt
    col_xd = d_c + d_qi
    col_kv = col_xd + d_d
    col_kw = col_kv + lanes

    @pl.when(jnp.logical_and(pl.program_id(0) == 0, t == 0))
    def _():
        bucket = bucket_ref[...]
        for h in range(H_C):
            b_acc = jnp.zeros(bucket.shape, F32)
            for b in range(N_BUCKETS):
                b_acc = jnp.where(bucket == b, rb_ref[b, h] * LOG2E, b_acc)
            bias_ref[0:2, h] = b_acc
            bias_ref[2, h] = jnp.zeros(bucket.shape[1:], F32)

    @pl.when(t == 0)
    def _():
        pool_ref[POOL_PAD - POOL_HIST:POOL_PAD, :] = poolh_ref[0]
        if has_hist:
            for j in range(hist_tiles):
                kk_ref[j] = kkh_ref[0, j * kt:(j + 1) * kt, :]
                kk2_ref[j] = kk2h_ref[0, j * kt:(j + 1) * kt, :]
                vv_ref[j] = vvh_ref[0, j * kt:(j + 1) * kt, :]

    x = x_ref[0]
    xn = _rms(x, g_ref[...]).astype(BF16)
    z = _dot(xn, win_ref[...])

    g1 = z[:, col_kv:col_kv + lanes]
    g2 = z[:, col_kw:col_kw + lanes]
    k_ref[0] = g1[:, :HD_C]
    v_ref[0] = g1[:, HD_C:]
    ki_ref[0] = g2[:, :D_I]
    lane = lax.broadcasted_iota(jnp.int32, (rows, lanes), 1)
    low = lane < HD_C
    g1r = pltpu.roll(g1, HD_C, 1)
    g2r = pltpu.roll(g2, D_I, 1)
    kk_new = jnp.where(low, g1, g2r).astype(BF16)
    kk2_new = jnp.where(low, g2, g1r).astype(BF16)
    vv_new = jnp.where(low, g1r, jnp.where(lane == HD_C, 1.0, 0.0)).astype(BF16)
    if rows % kt == 0:
        base_tile = (hist_len + t0) // kt
        for i in range(rows // kt):
            kk_ref[base_tile + i] = kk_new[i * kt:(i + 1) * kt]
            kk2_ref[base_tile + i] = kk2_new[i * kt:(i + 1) * kt]
            vv_ref[base_tile + i] = vv_new[i * kt:(i + 1) * kt]
    else:
        zpad = jnp.zeros((kt - rows, lanes), BF16)
        kk_ref[hist_tiles] = jnp.concatenate([kk_new, zpad], axis=0)
        kk2_ref[hist_tiles] = jnp.concatenate([kk2_new, zpad], axis=0)
        vv_ref[hist_tiles] = jnp.concatenate([vv_new, zpad], axis=0)
    kv_len = hist_len + t0 + rows

    w_idx = g2[:, D_I:D_I + H_I] * ((H_I ** -0.5) * (D_I ** -0.5))
    tri = tri_ref[...]
    ones_rhs = jnp.ones((kt, lanes), BF16)
    hb = min(rows, ATTN_HALF_ROWS)
    n_half = rows // hb
    n_new = max(rows // kt, 1)
    low_h = lax.broadcasted_iota(jnp.int32, (hb, lanes), 1) < HD_C

    jb = (hist_len + t0) // kt
    n_tiles = jb + n_new

    for hf in range(n_half):
        zr = z[hf * hb:(hf + 1) * hb]
        for g in range(n_pairs):
            grp = zr[:, g * lanes:(g + 1) * lanes] * (HD_C ** -0.5 * LOG2E)
            qe_ref[(hf * n_pairs + g) * hb:(hf * n_pairs + g + 1) * hb] = (
                jnp.where(low_h, grp, 0.0).astype(BF16))
            qo_ref[(hf * n_pairs + g) * hb:(hf * n_pairs + g + 1) * hb] = (
                jnp.where(low_h, 0.0, grp).astype(BF16))
    for g in range(n_ipairs):
        grp = z[:, d_c + g * lanes:d_c + (g + 1) * lanes]
        ie_ref[g * rows:(g + 1) * rows] = jnp.where(low, grp, 0.0).astype(BF16)
        io_ref[g * rows:(g + 1) * rows] = jnp.where(low, 0.0, grp).astype(BF16)
    for h in range(H_I):
        wb_ref[h] = jnp.broadcast_to(w_idx[:, h:h + 1], (rows, lanes))
    qchunk = (hist_len + t0 + lax.broadcasted_iota(jnp.int32, (rows, 1), 0)) >> CHUNK_SHIFT
    kpos_rel = lax.broadcasted_iota(jnp.int32, (rows, kt), 1)

    def score_body(j, carry):
        se = jnp.maximum(_dot_nt(ie_ref[...], kk2_ref[j]), 0.0)
        so = jnp.maximum(_dot_nt(io_ref[...], kk_ref[j]), 0.0)
        score = jnp.zeros((rows, kt), F32)
        for g in range(n_ipairs):
            we = wb_ref[2 * g]
            wo = wb_ref[2 * g + 1]
            score = score + se[g * rows:(g + 1) * rows] * jnp.concatenate([we] * (kt // lanes), axis=1)
            score = score + so[g * rows:(g + 1) * rows] * jnp.concatenate([wo] * (kt // lanes), axis=1)
        kpos = kpos_rel + j * kt
        adm = jnp.logical_and((kpos >> CHUNK_SHIFT) <= qchunk, kpos < kv_len)
        keys_ref[j] = _sortable(jnp.where(adm, score, NEG_INF))
        return carry
    lax.fori_loop(0, n_tiles, score_body, 0)

    def fill_body(j, carry):
        keys_ref[j] = jnp.full((rows, kt), INT_MIN, jnp.int32)
        return carry
    lax.fori_loop(n_tiles, keys_ref.shape[0], fill_body, 0)

    n_groups = keys_ref.shape[0] * (kt // lanes) // SEARCH_GROUP

    def plane_body(rc, carry):
        r = pl.multiple_of(rc * SEARCH_ROWS, SEARCH_ROWS)
        for gi in range(n_groups):
            a = []
            for i in range(SEARCH_GROUP):
                lt = gi * SEARCH_GROUP + i
                a.append(keys_ref[lt // (kt // lanes), pl.ds(r, SEARCH_ROWS),
                                  (lt % (kt // lanes)) * lanes:(lt % (kt // lanes) + 1) * lanes])
            for sh, msk in ((8, 0x00FF00FF), (4, 0x0F0F0F0F), (2, 0x33333333), (1, 0x55555555)):
                for k in range(SEARCH_GROUP):
                    if k & sh == 0:
                        tmp = (a[k] ^ lax.shift_right_logical(a[k + sh], sh)) & msk
                        a[k] = a[k] ^ tmp
                        a[k + sh] = a[k + sh] ^ (tmp << sh)
            for w in range(SEARCH_GROUP):
                planes_ref[gi, w, pl.ds(r, SEARCH_ROWS), :] = a[w]
        return carry
    lax.fori_loop(0, rows // SEARCH_ROWS, plane_body, 0)

    alive = [jnp.full((rows, lanes), -65536, jnp.int32) for _ in range(n_groups)]
    above = jnp.zeros((rows, 1), F32)
    thr_u = jnp.zeros((rows, 1), jnp.int32)
    for b in range(31, -1, -1):
        w = (31 - b) if b >= 16 else (15 - b)
        if b == 15:
            alive = [lax.shift_right_logical(a, 16) for a in alive]
        ones = []
        for gi in range(n_groups):
            plane = planes_ref[gi, w]
            if b == 31:
                plane = ~plane
            ones.append(alive[gi] & plane)
        pc = lax.population_count(ones[0])
        for gi in range(1, n_groups):
            pc = pc + lax.population_count(ones[gi])
        cnt = jnp.sum(pc.astype(F32), axis=1, keepdims=True)
        take = (above + cnt) >= float(top)
        alive = [jnp.where(take, o, a ^ o) for a, o in zip(alive, ones)]
        above = jnp.where(take, above, above + cnt)
        thr_u = jnp.where(take, thr_u | jnp.int32(INT_MIN if b == 31 else (1 << b)), thr_u)
    thr_ref[...] = jnp.broadcast_to(thr_u ^ jnp.int32(INT_MIN), (rows, lanes))
    need_ref[...] = jnp.broadcast_to(float(top) - above, (rows, lanes))
    off_ref[...] = jnp.zeros((rows, lanes), F32)
    m_ref[...] = jnp.full(m_ref.shape, M_INIT, F32)
    acc_ref[...] = jnp.zeros(acc_ref.shape, F32)

    def attn_half(j, hf, dj):
        rs = slice(hf * hb, (hf + 1) * hb)
        kj = keys_ref[j, rs]
        thr_t = jnp.concatenate([thr_ref[rs]] * (kt // lanes), axis=1)
        need_t = jnp.concatenate([need_ref[rs]] * (kt // lanes), axis=1)
        eq = jnp.where(kj == thr_t, 1.0, 0.0)
        eq_b = eq.astype(BF16)
        rank = _dot(eq_b, tri) + jnp.concatenate([off_ref[rs]] * (kt // lanes), axis=1)
        self_ = jnp.where(kj > thr_t, 1.0, jnp.where(rank <= need_t, eq, 0.0))
        sel = jnp.where(kj > NEG_INF_KEY, self_, 0.0) > 0.5
        mask_add = jnp.where(sel, 0.0, NEG_INF)
        off_ref[rs] = off_ref[rs] + _dot(eq_b, ones_rhs)
        q_rows = slice(hf * n_pairs * hb, (hf + 1) * n_pairs * hb)
        lg = (_dot_nt(qe_ref[q_rows], kk_ref[j]), _dot_nt(qo_ref[q_rows], kk2_ref[j]))
        ps = []
        alphas = []
        for eo in range(2):
            for g in range(n_pairs):
                h = 2 * g + eo
                l = lg[eo][g * hb:(g + 1) * hb]
                if dj is not None:
                    row_parts = []
                    for sub in range(hb // qb):
                        sblk = ((hf * (hb // qb) + sub) * qb) // lanes
                        parts = []
                        for c in range(kt // lanes):
                            part = l[sub * qb:(sub + 1) * qb, c * lanes:(c + 1) * lanes]
                            d = (kt // lanes) * dj + c - sblk
                            if isinstance(d, int):
                                if d in (-1, 0):
                                    part = part + bias_ref[d + 1, h]
                            else:
                                entry = jnp.where(d == -1, 0, jnp.where(d == 0, 1, 2))
                                part = part + bias_ref[entry, h]
                            parts.append(part)
                        row_parts.append(jnp.concatenate(parts, axis=1))
                    l = row_parts[0] if len(row_parts) == 1 else jnp.concatenate(row_parts, axis=0)
                l = l + mask_add
                st = slice(((hf * 2 + eo) * n_pairs + g) * hb, ((hf * 2 + eo) * n_pairs + g + 1) * hb)
                m_old = m_ref[st]
                m_new = jnp.maximum(m_old, jnp.max(l, axis=1, keepdims=True))
                alphas.append(jnp.exp2(m_old - m_new))
                ps.append(jnp.exp2(l - jnp.concatenate([m_new] * (kt // lanes), axis=1)).astype(BF16))
                m_ref[st] = m_new
        a_rows = slice(hf * H_C * hb, (hf + 1) * H_C * hb)
        acc_ref[a_rows] = (acc_ref[a_rows] * jnp.concatenate(alphas, axis=0)
                           + _dot(jnp.concatenate(ps, axis=0), vv_ref[j]))

    def far_body(j, carry):
        for hf in range(n_half):
            attn_half(j, hf, None)
        return carry
    lax.fori_loop(0, jb - 1, far_body, 0)

    def near_body(j, carry):
        for hf in range(n_half):
            attn_half(j, hf, j - jb)
        return carry
    lax.fori_loop(jnp.maximum(jb - 1, 0), jb + 1, near_body, 0)

    for dj in range(1, n_new):
        for hf in range(n_half):
            if ((hf + 1) * hb - qb) // lanes >= (kt // lanes) * dj:
                attn_half(jb + dj, hf, dj)

    for hf in range(n_half):
        for g in range(n_pairs):
            oe = acc_ref[((hf * 2) * n_pairs + g) * hb:((hf * 2) * n_pairs + g + 1) * hb]
            oo = acc_ref[((hf * 2 + 1) * n_pairs + g) * hb:((hf * 2 + 1) * n_pairs + g + 1) * hb]
            oe = oe / oe[:, HD_C:HD_C + 1]
            oo = oo / oo[:, HD_C:HD_C + 1]
            yc_ref[hf * hb:(hf + 1) * hb, g * lanes:(g + 1) * lanes] = jnp.where(
                low_h, oe, pltpu.roll(oo, HD_C, 1))

    xd = z[:, col_xd:col_xd + d_d]
    pool_ref[POOL_PAD:POOL_PAD + rows, :] = xd
    np_ref[0] = pool_ref[rows + POOL_PAD - POOL_HIST:rows + POOL_PAD, :]
    run = xd
    wins = {}
    for dshift in range(1, POOL_WINDOWS[-1]):
        run = run + pool_ref[POOL_PAD - dshift:POOL_PAD - dshift + rows, :]
        if dshift + 1 in POOL_WINDOWS:
            wins[dshift + 1] = run
    pool_ref[0:POOL_PAD, :] = pool_ref[rows:rows + POOL_PAD, :]
    dg_d = d_d // len(POOL_WINDOWS)
    lane_d = lax.broadcasted_iota(jnp.int32, (rows, d_d), 1)
    tpos = n_valid_hist + t0 + 1 + lax.broadcasted_iota(jnp.int32, (rows, d_d), 0)
    win_sum = wins[POOL_WINDOWS[-1]]
    width = jnp.full((rows, d_d), POOL_WINDOWS[-1], jnp.int32)
    for gi in range(len(POOL_WINDOWS) - 2, -1, -1):
        in_g = lane_d < (gi + 1) * dg_d
        win_sum = jnp.where(in_g, wins[POOL_WINDOWS[gi]], win_sum)
        width = jnp.where(in_g, POOL_WINDOWS[gi], width)
    count = jnp.minimum(tpos, width).astype(F32)
    m_pool = win_sum / count - xd
    y_d = _dot(m_pool.astype(BF16), wpool_ref[...]) * spool_ref[...]

    y_cat = jnp.concatenate([yc_ref[...], y_d], axis=1).astype(BF16)
    y_ref[0] = x + _dot(y_cat, wout_ref[...])


def _odd_layer(x, k_hist, v_hist, ki_hist, pool_hist, n_valid_hist, g_norm, w_in, w_pool, s_pool,
               w_out, rel_bias, *, rows):
    b, t, d = x.shape
    hist_len = k_hist.shape[1]
    has_hist = hist_len > 0
    d_c = H_C * HD_C
    d_qi = H_I * D_I
    d_d = d - d_c
    qb = min(2 * CHUNK, rows)
    kt = KEY_TILE
    lanes = V7X_LANES
    s_total = hist_len + t
    top = min(TOPK_MAX, s_total // 4)
    assert t % rows == 0 and rows % qb == 0 and hist_len % kt == 0
    assert rows % kt == 0 or (rows == t and rows == qb and rows <= CHUNK)
    n_tiles = (s_total + kt - 1) // kt
    tiles_per_group = SEARCH_GROUP * lanes // kt
    n_groups = (n_tiles + tiles_per_group - 1) // tiles_per_group
    n_tiles_pad = n_groups * tiles_per_group
    assert rows % SEARCH_ROWS == 0 and rows % min(rows, ATTN_HALF_ROWS) == 0

    offs = np.cumsum([0, d_c, HD_C, HD_C, d_qi, D_I, H_I]).tolist()
    q_w, k_w, v_w, qi_w, ki_w, wi_w = (w_in[:, offs[i]:offs[i + 1]] for i in range(6))
    xd_w = w_in[:, offs[6]:]
    pad_w = jnp.zeros((d, lanes - D_I - H_I), w_in.dtype)
    w_all = jnp.concatenate([q_w, qi_w, xd_w, k_w, v_w, ki_w, wi_w, pad_w], axis=1).astype(BF16)
    wpool_bd = jax.scipy.linalg.block_diag(*[w_pool[g] for g in range(w_pool.shape[0])]).astype(BF16)

    rel = (lanes * jnp.arange(-1, 1, dtype=jnp.int32)[:, None, None]
           + jnp.arange(lanes, dtype=jnp.int32)[None, None, :]
           - jnp.arange(qb, dtype=jnp.int32)[None, :, None])
    bucket = _t5_bucket(rel)
    far_bucket = _t5_bucket(jnp.int32(-2 * lanes))
    rb_shift = rel_bias - rel_bias[far_bucket][None, :]
    tri = (jnp.arange(kt)[:, None] <= jnp.arange(kt)[None, :]).astype(BF16)

    args = [x]
    in_specs = [pl.BlockSpec((1, rows, d), lambda i, j: (i, j, 0))]
    if has_hist:
        ones = jnp.ones((b, hist_len, 1), F32)
        zeros = jnp.zeros((b, hist_len, lanes - HD_C - 1), F32)
        args += [jnp.concatenate([k_hist, ki_hist], axis=-1).astype(BF16),
                 jnp.concatenate([ki_hist, k_hist], axis=-1).astype(BF16),
                 jnp.concatenate([v_hist, ones, zeros], axis=-1).astype(BF16)]
        in_specs += [pl.BlockSpec((1, hist_len, lanes), lambda i, j: (i, 0, 0))] * 3
    consts = [g_norm.reshape(1, d), w_all, wpool_bd, s_pool.reshape(1, d_d), w_out.astype(BF16), bucket]
    args += [pool_hist] + consts + [rb_shift, tri]
    in_specs += ([pl.BlockSpec((1, POOL_HIST, d_d), lambda i, j: (i, 0, 0))]
                 + [_const_spec(a.shape) for a in consts]
                 + [pl.BlockSpec(memory_space=pltpu.SMEM), _const_spec(tri.shape)])
    kern = functools.partial(_odd_kernel, rows=rows, qb=qb, hist_len=hist_len,
                             n_valid_hist=n_valid_hist, top=top, has_hist=has_hist)
    return pl.pallas_call(
        kern,
        grid=(b, t // rows),
        in_specs=in_specs,
        out_specs=[
            pl.BlockSpec((1, rows, d), lambda i, j: (i, j, 0)),
            pl.BlockSpec((1, rows, HD_C), lambda i, j: (i, j, 0)),
            pl.BlockSpec((1, rows, HD_C), lambda i, j: (i, j, 0)),
            pl.BlockSpec((1, rows, D_I), lambda i, j: (i, j, 0)),
            pl.BlockSpec((1, POOL_HIST, d_d), lambda i, j: (i, 0, 0)),
        ],
        out_shape=[jax.ShapeDtypeStruct((b, t, d), F32),
                   jax.ShapeDtypeStruct((b, t, HD_C), F32),
                   jax.ShapeDtypeStruct((b, t, HD_C), F32),
                   jax.ShapeDtypeStruct((b, t, D_I), F32),
                   jax.ShapeDtypeStruct((b, POOL_HIST, d_d), F32)],
        scratch_shapes=[
            pltpu.VMEM((n_tiles, kt, lanes), BF16),
            pltpu.VMEM((n_tiles, kt, lanes), BF16),
            pltpu.VMEM((n_tiles, kt, lanes), BF16),
            pltpu.VMEM((n_tiles_pad, rows, kt), jnp.int32),
            pltpu.VMEM((n_groups, SEARCH_GROUP, rows, lanes), jnp.int32),
            pltpu.VMEM((H_C * rows, lanes), F32),
            pltpu.VMEM((H_C * rows, lanes), F32),
            pltpu.VMEM((rows, lanes), F32),
            pltpu.VMEM((rows, lanes), jnp.int32),
            pltpu.VMEM((rows, lanes), F32),
            pltpu.VMEM((3, H_C, qb, lanes), F32),
            pltpu.VMEM((POOL_PAD + rows, d_d), F32),
            pltpu.VMEM((rows, d_c), F32),
            pltpu.VMEM((H_C // 2 * rows, lanes), BF16),
            pltpu.VMEM((H_C // 2 * rows, lanes), BF16),
            pltpu.VMEM((H_I // 2 * rows, lanes), BF16),
            pltpu.VMEM((H_I // 2 * rows, lanes), BF16),
            pltpu.VMEM((H_I, rows, lanes), F32),
        ],
        compiler_params=pltpu.CompilerParams(
            dimension_semantics=("arbitrary", "arbitrary"),
            vmem_limit_bytes=V7X_VMEM_LIMIT_BYTES),
        name="odd_layer",
    )(*args)


def kernel(x_prompt, x_sample, cache_b_conv, cache_c_k, cache_c_v, cache_c_kidx, cache_d_pool, cache_ffn_conv, ln_mix, ln_ffn, ln_final, e_w_in, e_ln_v_g, e_ln_v_b, e_w_s, e_b_s, e_w_dw, e_b_dw, e_ln_c_g, e_ln_c_b, e_w_out, o_w_in, o_w_pool, o_s_pool, o_w_out, rel_bias, f_w_up, f_w_dw, f_b_dw, f_w_down):
    hp, hs = x_prompt, x_sample
    bp, bs = x_prompt.shape[0], x_sample.shape[0]
    ts = x_sample.shape[1]
    depth = ln_mix.shape[0]
    d_ff = f_w_down.shape[1]
    a_s_l, b_p_l, b_s_l = [], [], []
    ck_p_l, cv_p_l, cki_p_l, ck_s_l, cv_s_l, cki_s_l, d_p_l, d_s_l = [], [], [], [], [], [], [], []
    f_p_l, f_s_l = [], []
    for layer in range(depth):
        i = layer // 2
        if layer % 2 == 0:
            ew = (ln_mix[layer], e_w_in[i], e_ln_v_g[i], e_ln_v_b[i], e_w_s[i], e_b_s[i], e_w_dw[i],
                  e_b_dw[i], e_ln_c_g[i], e_ln_c_b[i], e_w_out[i])
            hp, b_p = _even_layer(hp, jnp.zeros((bp, K_B - 1, e_w_dw.shape[-1]), F32), *ew,
                                  n_streams=1, rows=PROMPT_ROWS, emit_v=False)
            hs, a_s, b_s = _even_layer(hs, cache_b_conv[i], *ew, n_streams=bs, rows=ts, emit_v=True)
            a_s_l.append(a_s); b_p_l.append(b_p); b_s_l.append(b_s)
        else:
            ow = (ln_mix[layer], o_w_in[i], o_w_pool[i], o_s_pool[i], o_w_out[i], rel_bias)
            d_d = o_w_pool.shape[1] * o_w_pool.shape[2]
            hp, k_p, v_p, ki_p, d_p = _odd_layer(
                hp, jnp.zeros((bp, 0, HD_C), F32), jnp.zeros((bp, 0, HD_C), F32),
                jnp.zeros((bp, 0, D_I), F32), jnp.zeros((bp, POOL_HIST, d_d), F32), 0, *ow,
                rows=PROMPT_ROWS)
            hs, k_s, v_s, ki_s, d_s = _odd_layer(
                hs, cache_c_k[i], cache_c_v[i], cache_c_kidx[i], cache_d_pool[i], POOL_HIST, *ow,
                rows=ts)
            ck_p_l.append(k_p); cv_p_l.append(v_p); cki_p_l.append(ki_p)
            ck_s_l.append(k_s); cv_s_l.append(v_s); cki_s_l.append(ki_s)
            d_p_l.append(d_p); d_s_l.append(d_s)
        g_final = ln_final if layer == depth - 1 else None
        fw = (ln_ffn[layer], f_w_up[layer], f_w_dw[layer], f_b_dw[layer], f_w_down[layer], g_final)
        hp, f_p = _conv_ffn(hp, jnp.zeros((bp, K_FFN - 1, d_ff), F32), *fw,
                            n_streams=1, rows=PROMPT_ROWS, n_chunks=FFN_CHUNKS)
        hs, f_s = _conv_ffn(hs, cache_ffn_conv[layer], *fw, n_streams=bs, rows=ts, n_chunks=FFN_CHUNKS)
        f_p_l.append(f_p); f_s_l.append(f_s)
    return (hp, hs,
            jnp.stack(a_s_l), jnp.stack(b_p_l), jnp.stack(b_s_l),
            jnp.stack(ck_p_l), jnp.stack(cv_p_l), jnp.stack(cki_p_l),
            jnp.stack(ck_s_l), jnp.stack(cv_s_l), jnp.stack(cki_s_l),
            jnp.stack(d_p_l), jnp.stack(d_s_l),
            jnp.stack(f_p_l), jnp.stack(f_s_l))
```

```python
import functools
import math

import jax
import jax.numpy as jnp
import numpy as np
from jax import lax
from jax.experimental import pallas as pl
from jax.experimental.pallas import tpu as pltpu

F32 = jnp.float32
BF16 = jnp.bfloat16

EPS = 1e-6
CHUNK = 64
CHUNK_SHIFT = CHUNK.bit_length() - 1
SGU_CHUNK = 128
G_A = 4
K_B = 31
H_C = 12
HD_C = 64
H_I = 8
D_I = 64
TOPK_MAX = 256
N_BUCKETS = 32
MAX_DIST = 128
POOL_WINDOWS = (2, 4, 8, 16)
POOL_HIST = 15
K_FFN = 3

V7X_LANES = 128
V7X_SUBLANES = 8
V7X_MXU_DIM = 256
V7X_VMEM_LIMIT_BYTES = 56 * 1024 * 1024

KEY_TILE = V7X_MXU_DIM
PROMPT_ROWS = 512
FFN_CHUNKS = 1
NEG_INF = float("-inf")


def _rms(x, g):
    return x * lax.rsqrt(jnp.mean(x * x, axis=-1, keepdims=True) + EPS) * g


def _layer_norm(x, g, b):
    mu = jnp.mean(x, axis=-1, keepdims=True)
    xc = x - mu
    return xc * lax.rsqrt(jnp.mean(xc * xc, axis=-1, keepdims=True) + EPS) * g + b


def _gelu_tanh(x):
    cdf = 0.5 * (1.0 + jnp.tanh(math.sqrt(2.0 / math.pi) * (x + 0.044715 * (x * x * x))))
    return x * cdf


def _dot(a, b):
    return jnp.dot(a, b, preferred_element_type=F32)


def _dot_nt(a, b):
    return lax.dot_general(a, b, (((1,), (1,)), ((), ())), preferred_element_type=F32)


def _const_spec(shape):
    nd = len(shape)
    return pl.BlockSpec(shape, lambda *_: (0,) * nd, pipeline_mode=pl.Buffered(1))


def _ffn_kernel(*refs, n_streams, rows, n_chunks, final_norm):
    if final_norm:
        (x_ref, hist_ref, g_ref, wu_ref, wd_ref, dw_ref, bdw_ref, gf_ref,
         y_ref, nh_ref, carry_ref) = refs
    else:
        (x_ref, hist_ref, g_ref, wu_ref, wd_ref, dw_ref, bdw_ref,
         y_ref, nh_ref, carry_ref) = refs
        gf_ref = None
    d_model = x_ref.shape[-1]
    d_ff = wd_ref.shape[0]
    fc = d_ff // n_chunks
    m = n_streams * rows

    @pl.when(pl.program_id(1) == 0)
    def _():
        for c in range(n_chunks):
            carry_ref[:, c, 6:8, :] = hist_ref[:, :, c * fc:(c + 1) * fc]

    x = x_ref[...].reshape(m, d_model)
    xn = _rms(x, g_ref[...]).astype(BF16)
    acc = jnp.zeros((m, d_model), F32)
    for c in range(n_chunks):
        cols = slice(c * fc, (c + 1) * fc)
        a = _dot(xn, wu_ref[:, cols])
        val = _dot(xn, wu_ref[:, d_ff + c * fc:d_ff + (c + 1) * fc])
        w = dw_ref[:, cols]
        ys = []
        for s in range(n_streams):
            a_s = a[s * rows:(s + 1) * rows]
            ext = jnp.concatenate([carry_ref[s, c], a_s], axis=0)
            ys.append(w[0:1] * ext[6:6 + rows] + w[1:2] * ext[7:7 + rows] + w[2:3] * a_s)
            carry_ref[s, c] = a_s[rows - 8:rows]
            nh_ref[s, :, c * fc:(c + 1) * fc] = a_s[rows - 2:rows]
        y = (ys[0] if n_streams == 1 else jnp.concatenate(ys, axis=0)) + bdw_ref[:, cols]
        acc = acc + _dot((_gelu_tanh(y) * val).astype(BF16), wd_ref[cols, :])
    out = x + acc
    if final_norm:
        out = _rms(out, gf_ref[...])
    y_ref[...] = out.reshape(y_ref.shape)


def _conv_ffn(x, hist, g_norm, w_up, w_dw, b_dw, w_down, g_final, *, n_streams, rows, n_chunks):
    b, t, d = x.shape
    d_ff = w_down.shape[0]
    fc = d_ff // n_chunks
    assert fc * n_chunks == d_ff and fc % V7X_LANES == 0
    assert b % n_streams == 0 and t % rows == 0 and rows % V7X_SUBLANES == 0
    final_norm = g_final is not None
    args = [x, hist, g_norm.reshape(1, d), w_up.astype(BF16), w_down.astype(BF16), w_dw,
            b_dw.reshape(1, d_ff)]
    in_specs = [
        pl.BlockSpec((n_streams, rows, d), lambda i, j: (i, j, 0)),
        pl.BlockSpec((n_streams, K_FFN - 1, d_ff), lambda i, j: (i, 0, 0)),
    ] + [_const_spec(a.shape) for a in args[2:]]
    if final_norm:
        args.append(g_final.reshape(1, d))
        in_specs.append(_const_spec((1, d)))
    kern = functools.partial(_ffn_kernel, n_streams=n_streams, rows=rows, n_chunks=n_chunks,
                             final_norm=final_norm)
    return pl.pallas_call(
        kern,
        grid=(b // n_streams, t // rows),
        in_specs=in_specs,
        out_specs=[
            pl.BlockSpec((n_streams, rows, d), lambda i, j: (i, j, 0)),
            pl.BlockSpec((n_streams, K_FFN - 1, d_ff), lambda i, j: (i, 0, 0)),
        ],
        out_shape=[jax.ShapeDtypeStruct((b, t, d), F32),
                   jax.ShapeDtypeStruct((b, K_FFN - 1, d_ff), F32)],
        scratch_shapes=[pltpu.VMEM((n_streams, n_chunks, 8, fc), F32)],
        compiler_params=pltpu.CompilerParams(
            dimension_semantics=("arbitrary", "arbitrary"),
            vmem_limit_bytes=V7X_VMEM_LIMIT_BYTES),
        name="conv_ffn",
    )(*args)


CONV_PAD = 32
CONV_ROW_BLOCK = 32
CONV_PITCH_PAD = 4


def _conv_rows_transposed(glu, s, hist_ref, wdw, bdw, stage_in_ref, stage_out_ref, prev_ref, cout_ref,
                          *, rows, first):
    sub = V7X_SUBLANES
    lanes = V7X_LANES
    nv = rows // sub
    pitch = nv + CONV_PITCH_PAD
    hist_rows = K_B - 1
    d_b = glu.shape[-1]
    sub_id = lax.broadcasted_iota(jnp.int32, (sub, lanes), 0)
    for lt in range(d_b // lanes):
        cols = slice(lt * lanes, (lt + 1) * lanes)

        @pl.when(first)
        def _():
            for e in range(hist_rows):
                prev_ref[s, lt, e, sub - 1:sub, :] = hist_ref[s, e:e + 1, cols]
        for q in range(sub):
            stage_in_ref[lt, q * pitch:q * pitch + nv, :] = (
                glu[s * rows + q * nv:s * rows + (q + 1) * nv, cols])
        cur = [stage_in_ref[lt, pl.ds(v, sub, stride=pitch), :] for v in range(nv)]
        head = []
        for e in range(hist_rows):
            merged = jnp.where(sub_id == sub - 1, prev_ref[s, lt, e], cur[nv - hist_rows + e])
            head.append(pltpu.roll(merged, 1, 0))
        for e in range(hist_rows):
            prev_ref[s, lt, e] = cur[nv - hist_rows + e]
        ext = head + cur
        wk = [jnp.broadcast_to(wdw[k:k + 1, cols], (sub, lanes)) for k in range(K_B)]
        bias = jnp.broadcast_to(bdw[:, cols], (sub, lanes))
        for v in range(nv):
            acc = bias
            for k in range(K_B):
                acc = acc + wk[k] * ext[v + k]
            stage_out_ref[lt, pl.ds(v, sub, stride=pitch), :] = acc
        for q in range(sub):
            cout_ref[s * rows + q * nv:s * rows + (q + 1) * nv, cols] = (
                stage_out_ref[lt, q * pitch:q * pitch + nv, :])


def _even_kernel(*refs, n_streams, rows, sgu_n, emit_v, transposed_conv):
    (x_ref, hist_ref, g_ref, win_ref, lnvg_ref, lnvb_ref, ws_ref, bs_ref, wdw_ref, bdw_ref,
     lncg_ref, lncb_ref, wout_ref) = refs[:13]
    n_out = 3 if emit_v else 2
    y_ref = refs[13]
    av_ref = refs[14] if emit_v else None
    nb_ref = refs[13 + n_out - 1]
    if transposed_conv:
        stage_in_ref, stage_out_ref, prev_ref, cout_ref = refs[13 + n_out:]
    else:
        conv_ref, cout_ref = refs[13 + n_out:]
    d_model = x_ref.shape[-1]
    d_b = wdw_ref.shape[-1]
    d_a = d_b
    dg = d_a // G_A
    m = n_streams * rows
    hist_rows = K_B - 1
    off = CONV_PAD - hist_rows

    if not transposed_conv:
        @pl.when(pl.program_id(1) == 0)
        def _():
            conv_ref[:, off:CONV_PAD, :] = hist_ref[...]

    x = x_ref[...].reshape(m, d_model)
    xn = _rms(x, g_ref[...]).astype(BF16)
    z = _dot(xn, win_ref[...])

    za = jax.nn.gelu(z[:, :2 * d_a], approximate=True)
    u = za[:, :d_a]
    lnvg = lnvg_ref[...]
    lnvb = lnvb_ref[...]
    v = jnp.concatenate(
        [_layer_norm(za[:, d_a + g * dg:d_a + (g + 1) * dg], lnvg[:, g * dg:(g + 1) * dg],
                     lnvb[:, g * dg:(g + 1) * dg]) for g in range(G_A)], axis=1)
    if emit_v:
        av_ref[...] = v.reshape(av_ref.shape)
    vb = v.astype(BF16)
    tril = (lax.broadcasted_iota(jnp.int32, (sgu_n, sgu_n), 0)
            >= lax.broadcasted_iota(jnp.int32, (sgu_n, sgu_n), 1))
    ws = [jnp.where(tril, ws_ref[g], 0.0).astype(BF16) for g in range(G_A)]
    bs = bs_ref[...]
    sg_rows = []
    for c in range(m // sgu_n):
        vc = vb[c * sgu_n:(c + 1) * sgu_n]
        sg_rows.append(jnp.concatenate(
            [_dot(ws[g], vc[:, g * dg:(g + 1) * dg]) for g in range(G_A)], axis=1) + bs)
    sg = sg_rows[0] if len(sg_rows) == 1 else jnp.concatenate(sg_rows, axis=0)
    y_a = u * sg

    glu = z[:, 2 * d_a:2 * d_a + d_b] * jax.nn.sigmoid(z[:, 2 * d_a + d_b:])
    wdw = wdw_ref[...]
    bdw = bdw_ref[...]
    for s in range(n_streams):
        if transposed_conv:
            _conv_rows_transposed(glu, s, hist_ref, wdw, bdw, stage_in_ref, stage_out_ref, prev_ref,
                                  cout_ref, rows=rows, first=pl.program_id(1) == 0)
            nb_ref[s] = glu[(s + 1) * rows - hist_rows:(s + 1) * rows]
            continue
        conv_ref[s, CONV_PAD:CONV_PAD + rows, :] = glu[s * rows:(s + 1) * rows]
        for rb in range(rows // CONV_ROW_BLOCK):
            r0 = rb * CONV_ROW_BLOCK
            acc = jnp.zeros((CONV_ROW_BLOCK, d_b), F32) + bdw
            for k in range(K_B):
                acc = acc + wdw[k:k + 1] * conv_ref[s, r0 + off + k:r0 + off + k + CONV_ROW_BLOCK, :]
            cout_ref[s * rows + r0:s * rows + r0 + CONV_ROW_BLOCK, :] = acc
        nb_ref[s] = conv_ref[s, rows + off:rows + CONV_PAD, :]
        conv_ref[s, 0:CONV_PAD, :] = conv_ref[s, rows:rows + CONV_PAD, :]
    y_b = jax.nn.silu(_layer_norm(cout_ref[...], lncg_ref[...], lncb_ref[...]))

    y = _dot(jnp.concatenate([y_a, y_b], axis=1).astype(BF16), wout_ref[...])
    y_ref[...] = (x + y).reshape(y_ref.shape)


def _even_layer(x, conv_hist, g_norm, w_in, ln_v_g, ln_v_b, w_s, b_s, w_dw, b_dw, ln_c_g, ln_c_b,
                w_out, *, n_streams, rows, emit_v):
    b, t, d = x.shape
    d_b = w_dw.shape[-1]
    d_a = ln_v_g.shape[-1]
    dg = d_a // G_A
    sgu_n = min(t, SGU_CHUNK)
    assert b % n_streams == 0 and t % rows == 0 and rows % sgu_n == 0 and (n_streams * rows) % sgu_n == 0
    assert rows % CONV_ROW_BLOCK == 0 and rows >= CONV_PAD and d_a == d_b
    bs_full = jnp.repeat(b_s[:, :sgu_n].T, dg, axis=1)
    args = [x, conv_hist, g_norm.reshape(1, d), w_in.astype(BF16), ln_v_g.reshape(1, d_a),
            ln_v_b.reshape(1, d_a), w_s[:, :sgu_n, :sgu_n], bs_full, w_dw, b_dw.reshape(1, d_b),
            ln_c_g.reshape(1, d_b), ln_c_b.reshape(1, d_b), w_out.astype(BF16)]
    in_specs = [
        pl.BlockSpec((n_streams, rows, d), lambda i, j: (i, j, 0)),
        pl.BlockSpec((n_streams, K_B - 1, d_b), lambda i, j: (i, 0, 0)),
    ] + [_const_spec(a.shape) for a in args[2:]]
    out_specs = [pl.BlockSpec((n_streams, rows, d), lambda i, j: (i, j, 0))]
    out_shape = [jax.ShapeDtypeStruct((b, t, d), F32)]
    if emit_v:
        out_specs.append(pl.BlockSpec((n_streams, rows, d_a), lambda i, j: (i, j, 0)))
        out_shape.append(jax.ShapeDtypeStruct((b, t, d_a), F32))
    out_specs.append(pl.BlockSpec((n_streams, K_B - 1, d_b), lambda i, j: (i, 0, 0)))
    out_shape.append(jax.ShapeDtypeStruct((b, K_B - 1, d_b), F32))
    transposed_conv = rows // V7X_SUBLANES >= K_B - 1
    kern = functools.partial(_even_kernel, n_streams=n_streams, rows=rows, sgu_n=sgu_n, emit_v=emit_v,
                             transposed_conv=transposed_conv)
    if transposed_conv:
        stage_shape = (d_b // V7X_LANES, V7X_SUBLANES * (rows // V7X_SUBLANES + CONV_PITCH_PAD), V7X_LANES)
        conv_scratch = [pltpu.VMEM(stage_shape, F32), pltpu.VMEM(stage_shape, F32),
                        pltpu.VMEM((n_streams, d_b // V7X_LANES, K_B - 1, V7X_SUBLANES, V7X_LANES), F32)]
    else:
        conv_scratch = [pltpu.VMEM((n_streams, CONV_PAD + rows, d_b), F32)]
    return pl.pallas_call(
        kern,
        grid=(b // n_streams, t // rows),
        in_specs=in_specs,
        out_specs=out_specs,
        out_shape=out_shape,
        scratch_shapes=conv_scratch + [pltpu.VMEM((n_streams * rows, d_b), F32)],
        compiler_params=pltpu.CompilerParams(
            dimension_semantics=("arbitrary", "arbitrary"),
            vmem_limit_bytes=V7X_VMEM_LIMIT_BYTES),
        name="even_layer",
    )(*args)


POOL_PAD = 16
ATTN_HALF_ROWS = 256
SEARCH_GROUP = 16
SEARCH_ROWS = 16
INT_MIN = -2 ** 31
NEG_INF_KEY = -2 ** 31 + 0x7FFFFF
LOG2E = math.log2(math.e)
M_INIT = -3.0e38


def _t5_bucket(rel):
    nb = N_BUCKETS // 2
    exact = nb // 2
    side = jnp.where(rel > 0, nb, 0)
    n = jnp.abs(rel)
    large = exact + (jnp.log(jnp.maximum(n, 1).astype(jnp.float32) / exact)
                     / math.log(MAX_DIST / exact) * (nb - exact)).astype(jnp.int32)
    large = jnp.minimum(large, nb - 1)
    return side + jnp.where(n < exact, n, large)


def _sortable(score):
    bits = pltpu.bitcast(score, jnp.int32)
    return bits ^ ((bits >> 31) & 0x7FFFFFFF)


def _for_range(lo, hi, body):
    if isinstance(lo, int) and isinstance(hi, int):
        for j in range(lo, hi):
            body(j)
    else:
        def step(j, carry):
            body(j)
            return carry
        lax.fori_loop(lo, hi, step, 0)


def _odd_kernel(*refs, rows, qb, hist_len, n_valid_hist, top, has_hist, single_step):
    refs = list(refs)
    x_ref = refs.pop(0)
    if has_hist:
        kh_ref, vh_ref, kih_ref = refs[:3]
        refs = refs[3:]
    (poolh_ref, g_ref, win_ref, wpool_ref, spool_ref, wout_ref, bucket_ref, rb_ref, tri_ref,
     y_ref, k_ref, v_ref, ki_ref, np_ref,
     kk_ref, kk2_ref, vv_ref, keys_ref, planes_ref, acc_ref, m_ref, off_ref, thr_ref, need_ref,
     bias_ref, pool_ref, yc_ref, qe_ref, qo_ref, ie_ref, io_ref, wb_ref) = refs
    d_model = x_ref.shape[-1]
    d_c = H_C * HD_C
    d_qi = H_I * D_I
    d_d = wpool_ref.shape[0]
    n_pairs = H_C // 2
    n_ipairs = H_I // 2
    kt = KEY_TILE
    lanes = V7X_LANES
    t = pl.program_id(1)
    t0 = 0 if single_step else t * rows
    hist_tiles = hist_len // kt
    col_xd = d_c + d_qi
    col_kv = col_xd + d_d
    col_kw = col_kv + lanes

    @pl.when(jnp.logical_and(pl.program_id(0) == 0, t == 0))
    def _():
        bucket = bucket_ref[...]
        for h in range(H_C):
            b_acc = jnp.zeros(bucket.shape, F32)
            for b in range(N_BUCKETS):
                b_acc = jnp.where(bucket == b, rb_ref[b, h] * LOG2E, b_acc)
            bias_ref[0:2, h] = b_acc
            bias_ref[2, h] = jnp.zeros(bucket.shape[1:], F32)

    @pl.when(t == 0)
    def _():
        pool_ref[POOL_PAD - POOL_HIST:POOL_PAD, :] = poolh_ref[0]
        if has_hist:
            ones_half = jnp.ones((kt, lanes - HD_C), F32)
            for j in range(hist_tiles):
                kj = kh_ref[0, j * kt:(j + 1) * kt, :]
                kij = kih_ref[0, j * kt:(j + 1) * kt, :]
                vj = vh_ref[0, j * kt:(j + 1) * kt, :]
                kk_ref[j] = jnp.concatenate([kj, kij], axis=1).astype(BF16)
                kk2_ref[j] = jnp.concatenate([kij, kj], axis=1).astype(BF16)
                vv_ref[j] = jnp.concatenate([vj, ones_half], axis=1).astype(BF16)

    x = x_ref[0]
    xn = _rms(x, g_ref[...]).astype(BF16)
    z = _dot(xn, win_ref[...])

    g1 = z[:, col_kv:col_kv + lanes]
    g2 = z[:, col_kw:col_kw + lanes]
    k_ref[0] = g1[:, :HD_C]
    v_ref[0] = g1[:, HD_C:]
    ki_ref[0] = g2[:, :D_I]
    lane = lax.broadcasted_iota(jnp.int32, (rows, lanes), 1)
    low = lane < HD_C
    g1r = pltpu.roll(g1, HD_C, 1)
    g2r = pltpu.roll(g2, D_I, 1)
    kk_new = jnp.where(low, g1, g2r).astype(BF16)
    kk2_new = jnp.where(low, g2, g1r).astype(BF16)
    vv_new = jnp.where(low, g1r, 1.0).astype(BF16)
    if rows % kt == 0:
        base_tile = (hist_len + t0) // kt
        for i in range(rows // kt):
            kk_ref[base_tile + i] = kk_new[i * kt:(i + 1) * kt]
            kk2_ref[base_tile + i] = kk2_new[i * kt:(i + 1) * kt]
            vv_ref[base_tile + i] = vv_new[i * kt:(i + 1) * kt]
    else:
        zpad = jnp.zeros((kt - rows, lanes), BF16)
        kk_ref[hist_tiles] = jnp.concatenate([kk_new, zpad], axis=0)
        kk2_ref[hist_tiles] = jnp.concatenate([kk2_new, zpad], axis=0)
        vv_ref[hist_tiles] = jnp.concatenate([vv_new, zpad], axis=0)
    kv_len = hist_len + t0 + rows

    w_idx = g2[:, D_I:D_I + H_I] * ((H_I ** -0.5) * (D_I ** -0.5))
    tri = tri_ref[...]
    ones_rhs = jnp.ones((kt, lanes), BF16)
    hb = min(rows, ATTN_HALF_ROWS)
    n_half = rows // hb
    n_new = max(rows // kt, 1)
    low_h = lax.broadcasted_iota(jnp.int32, (hb, lanes), 1) < HD_C

    jb = (hist_len + t0) // kt
    n_tiles = jb + n_new

    for hf in range(n_half):
        zr = z[hf * hb:(hf + 1) * hb]
        for g in range(n_pairs):
            grp = zr[:, g * lanes:(g + 1) * lanes] * (HD_C ** -0.5 * LOG2E)
            qe_ref[(hf * n_pairs + g) * hb:(hf * n_pairs + g + 1) * hb] = (
                jnp.where(low_h, grp, 0.0).astype(BF16))
            qo_ref[(hf * n_pairs + g) * hb:(hf * n_pairs + g + 1) * hb] = (
                jnp.where(low_h, 0.0, grp).astype(BF16))
    for g in range(n_ipairs):
        grp = z[:, d_c + g * lanes:d_c + (g + 1) * lanes]
        ie_ref[g * rows:(g + 1) * rows] = jnp.where(low, grp, 0.0).astype(BF16)
        io_ref[g * rows:(g + 1) * rows] = jnp.where(low, 0.0, grp).astype(BF16)
    for h in range(H_I):
        wb_ref[h] = jnp.broadcast_to(w_idx[:, h:h + 1], (rows, lanes))
    qchunk = (hist_len + t0 + lax.broadcasted_iota(jnp.int32, (rows, 1), 0)) >> CHUNK_SHIFT
    kpos_rel = lax.broadcasted_iota(jnp.int32, (rows, kt), 1)

    def score_body(j):
        se = jnp.maximum(_dot_nt(ie_ref[...], kk2_ref[j]), 0.0)
        so = jnp.maximum(_dot_nt(io_ref[...], kk_ref[j]), 0.0)
        score = jnp.zeros((rows, kt), F32)
        for g in range(n_ipairs):
            we = wb_ref[2 * g]
            wo = wb_ref[2 * g + 1]
            score = score + se[g * rows:(g + 1) * rows] * jnp.concatenate([we] * (kt // lanes), axis=1)
            score = score + so[g * rows:(g + 1) * rows] * jnp.concatenate([wo] * (kt // lanes), axis=1)
        kpos = kpos_rel + j * kt
        adm = jnp.logical_and((kpos >> CHUNK_SHIFT) <= qchunk, kpos < kv_len)
        keys_ref[j] = _sortable(jnp.where(adm, score, NEG_INF))
    _for_range(0, n_tiles, score_body)

    def fill_body(j):
        keys_ref[j] = jnp.full((rows, kt), INT_MIN, jnp.int32)
    _for_range(n_tiles, keys_ref.shape[0], fill_body)

    n_groups = keys_ref.shape[0] * (kt // lanes) // SEARCH_GROUP

    def plane_body(rc, carry):
        r = pl.multiple_of(rc * SEARCH_ROWS, SEARCH_ROWS)
        for gi in range(n_groups):
            a = []
            for i in range(SEARCH_GROUP):
                lt = gi * SEARCH_GROUP + i
                a.append(keys_ref[lt // (kt // lanes), pl.ds(r, SEARCH_ROWS),
                                  (lt % (kt // lanes)) * lanes:(lt % (kt // lanes) + 1) * lanes])
            for sh, msk in ((8, 0x00FF00FF), (4, 0x0F0F0F0F), (2, 0x33333333), (1, 0x55555555)):
                for k in range(SEARCH_GROUP):
                    if k & sh == 0:
                        tmp = (a[k] ^ lax.shift_right_logical(a[k + sh], sh)) & msk
                        a[k] = a[k] ^ tmp
                        a[k + sh] = a[k + sh] ^ (tmp << sh)
            for w in range(SEARCH_GROUP):
                planes_ref[gi, w, pl.ds(r, SEARCH_ROWS), :] = a[w]
        return carry
    lax.fori_loop(0, rows // SEARCH_ROWS, plane_body, 0)

    alive = [jnp.full((rows, lanes), -65536, jnp.int32) for _ in range(n_groups)]
    above = jnp.zeros((rows, 1), F32)
    thr_u = jnp.zeros((rows, 1), jnp.int32)
    for b in range(31, -1, -1):
        w = (31 - b) if b >= 16 else (15 - b)
        if b == 15:
            alive = [lax.shift_right_logical(a, 16) for a in alive]
        ones = []
        for gi in range(n_groups):
            plane = planes_ref[gi, w]
            if b == 31:
                plane = ~plane
            ones.append(alive[gi] & plane)
        pc = lax.population_count(ones[0])
        for gi in range(1, n_groups):
            pc = pc + lax.population_count(ones[gi])
        cnt = jnp.sum(pc.astype(F32), axis=1, keepdims=True)
        take = (above + cnt) >= float(top)
        alive = [jnp.where(take, o, a ^ o) for a, o in zip(alive, ones)]
        above = jnp.where(take, above, above + cnt)
        thr_u = jnp.where(take, thr_u | jnp.int32(INT_MIN if b == 31 else (1 << b)), thr_u)
    thr_ref[...] = jnp.broadcast_to(thr_u ^ jnp.int32(INT_MIN), (rows, lanes))
    need_ref[...] = jnp.broadcast_to(float(top) - above, (rows, lanes))
    off_ref[...] = jnp.zeros((rows, lanes), F32)
    m_ref[...] = jnp.full(m_ref.shape, M_INIT, F32)
    acc_ref[...] = jnp.zeros(acc_ref.shape, F32)

    def attn_half(j, hf, dj):
        rs = slice(hf * hb, (hf + 1) * hb)
        kj = keys_ref[j, rs]
        thr_t = jnp.concatenate([thr_ref[rs]] * (kt // lanes), axis=1)
        need_t = jnp.concatenate([need_ref[rs]] * (kt // lanes), axis=1)
        eq = jnp.where(kj == thr_t, 1.0, 0.0)
        eq_b = eq.astype(BF16)
        rank = _dot(eq_b, tri) + jnp.concatenate([off_ref[rs]] * (kt // lanes), axis=1)
        self_ = jnp.where(kj > thr_t, 1.0, jnp.where(rank <= need_t, eq, 0.0))
        sel = jnp.where(kj > NEG_INF_KEY, self_, 0.0) > 0.5
        mask_add = jnp.where(sel, 0.0, NEG_INF)
        off_ref[rs] = off_ref[rs] + _dot(eq_b, ones_rhs)
        q_rows = slice(hf * n_pairs * hb, (hf + 1) * n_pairs * hb)
        lg = (_dot_nt(qe_ref[q_rows], kk_ref[j]), _dot_nt(qo_ref[q_rows], kk2_ref[j]))
        ps = []
        alphas = []
        for eo in range(2):
            for g in range(n_pairs):
                h = 2 * g + eo
                l = lg[eo][g * hb:(g + 1) * hb]
                if dj is not None:
                    row_parts = []
                    for sub in range(hb // qb):
                        sblk = ((hf * (hb // qb) + sub) * qb) // lanes
                        parts = []
                        for c in range(kt // lanes):
                            part = l[sub * qb:(sub + 1) * qb, c * lanes:(c + 1) * lanes]
                            d = (kt // lanes) * dj + c - sblk
                            if isinstance(d, int):
                                if d in (-1, 0):
                                    part = part + bias_ref[d + 1, h]
                            else:
                                entry = jnp.where(d == -1, 0, jnp.where(d == 0, 1, 2))
                                part = part + bias_ref[entry, h]
                            parts.append(part)
                        row_parts.append(jnp.concatenate(parts, axis=1))
                    l = row_parts[0] if len(row_parts) == 1 else jnp.concatenate(row_parts, axis=0)
                l = l + mask_add
                st = slice(((hf * 2 + eo) * n_pairs + g) * hb, ((hf * 2 + eo) * n_pairs + g + 1) * hb)
                m_old = m_ref[st]
                m_new = jnp.maximum(m_old, jnp.max(l, axis=1, keepdims=True))
                alphas.append(jnp.exp2(m_old - m_new))
                ps.append(jnp.exp2(l - jnp.concatenate([m_new] * (kt // lanes), axis=1)).astype(BF16))
                m_ref[st] = m_new
        a_rows = slice(hf * H_C * hb, (hf + 1) * H_C * hb)
        acc_ref[a_rows] = (acc_ref[a_rows] * jnp.concatenate(alphas, axis=0)
                           + _dot(jnp.concatenate(ps, axis=0), vv_ref[j]))

    def far_body(j):
        for hf in range(n_half):
            attn_half(j, hf, None)
    _for_range(0, jb - 1, far_body)

    def near_body(j):
        for hf in range(n_half):
            attn_half(j, hf, j - jb)
    _for_range(max(jb - 1, 0) if single_step else jnp.maximum(jb - 1, 0), jb + 1, near_body)

    for dj in range(1, n_new):
        for hf in range(n_half):
            if ((hf + 1) * hb - qb) // lanes >= (kt // lanes) * dj:
                attn_half(jb + dj, hf, dj)

    for hf in range(n_half):
        for g in range(n_pairs):
            oe = acc_ref[((hf * 2) * n_pairs + g) * hb:((hf * 2) * n_pairs + g + 1) * hb]
            oo = acc_ref[((hf * 2 + 1) * n_pairs + g) * hb:((hf * 2 + 1) * n_pairs + g + 1) * hb]
            num = jnp.where(low_h, oe, pltpu.roll(oo, HD_C, 1))
            den = jnp.where(low_h, pltpu.roll(oe, HD_C, 1), oo)
            yc_ref[hf * hb:(hf + 1) * hb, g * lanes:(g + 1) * lanes] = num / den

    xd = z[:, col_xd:col_xd + d_d]
    pool_ref[POOL_PAD:POOL_PAD + rows, :] = xd
    np_ref[0] = pool_ref[rows + POOL_PAD - POOL_HIST:rows + POOL_PAD, :]
    run = xd
    wins = {}
    for dshift in range(1, POOL_WINDOWS[-1]):
        run = run + pool_ref[POOL_PAD - dshift:POOL_PAD - dshift + rows, :]
        if dshift + 1 in POOL_WINDOWS:
            wins[dshift + 1] = run
    pool_ref[0:POOL_PAD, :] = pool_ref[rows:rows + POOL_PAD, :]
    dg_d = d_d // len(POOL_WINDOWS)
    lane_d = lax.broadcasted_iota(jnp.int32, (rows, d_d), 1)
    tpos = n_valid_hist + t0 + 1 + lax.broadcasted_iota(jnp.int32, (rows, d_d), 0)
    win_sum = wins[POOL_WINDOWS[-1]]
    width = jnp.full((rows, d_d), POOL_WINDOWS[-1], jnp.int32)
    for gi in range(len(POOL_WINDOWS) - 2, -1, -1):
        in_g = lane_d < (gi + 1) * dg_d
        win_sum = jnp.where(in_g, wins[POOL_WINDOWS[gi]], win_sum)
        width = jnp.where(in_g, POOL_WINDOWS[gi], width)
    count = jnp.minimum(tpos, width).astype(F32)
    m_pool = win_sum / count - xd
    y_d = _dot(m_pool.astype(BF16), wpool_ref[...]) * spool_ref[...]

    y_cat = jnp.concatenate([yc_ref[...], y_d], axis=1).astype(BF16)
    y_ref[0] = x + _dot(y_cat, wout_ref[...])


def _odd_layer(x, k_hist, v_hist, ki_hist, pool_hist, n_valid_hist, g_norm, w_in, w_pool, s_pool,
               w_out, rel_bias, *, rows):
    b, t, d = x.shape
    hist_len = k_hist.shape[1]
    has_hist = hist_len > 0
    d_c = H_C * HD_C
    d_qi = H_I * D_I
    d_d = d - d_c
    qb = min(2 * CHUNK, rows)
    kt = KEY_TILE
    lanes = V7X_LANES
    s_total = hist_len + t
    top = min(TOPK_MAX, s_total // 4)
    assert t % rows == 0 and rows % qb == 0 and hist_len % kt == 0
    assert rows % kt == 0 or (rows == t and rows == qb and rows <= CHUNK)
    n_tiles = (s_total + kt - 1) // kt
    tiles_per_group = SEARCH_GROUP * lanes // kt
    n_groups = (n_tiles + tiles_per_group - 1) // tiles_per_group
    n_tiles_pad = n_groups * tiles_per_group
    assert rows % SEARCH_ROWS == 0 and rows % min(rows, ATTN_HALF_ROWS) == 0

    offs = np.cumsum([0, d_c, HD_C, HD_C, d_qi, D_I, H_I]).tolist()
    q_w, k_w, v_w, qi_w, ki_w, wi_w = (w_in[:, offs[i]:offs[i + 1]] for i in range(6))
    xd_w = w_in[:, offs[6]:]
    pad_w = jnp.zeros((d, lanes - D_I - H_I), w_in.dtype)
    w_all = jnp.concatenate([q_w, qi_w, xd_w, k_w, v_w, ki_w, wi_w, pad_w], axis=1).astype(BF16)
    wpool_bd = jax.scipy.linalg.block_diag(*[w_pool[g] for g in range(w_pool.shape[0])]).astype(BF16)

    rel = (lanes * jnp.arange(-1, 1, dtype=jnp.int32)[:, None, None]
           + jnp.arange(lanes, dtype=jnp.int32)[None, None, :]
           - jnp.arange(qb, dtype=jnp.int32)[None, :, None])
    bucket = _t5_bucket(rel)
    far_bucket = _t5_bucket(jnp.int32(-2 * lanes))
    rb_shift = rel_bias - rel_bias[far_bucket][None, :]
    tri = (jnp.arange(kt)[:, None] <= jnp.arange(kt)[None, :]).astype(BF16)

    args = [x]
    in_specs = [pl.BlockSpec((1, rows, d), lambda i, j: (i, j, 0))]
    if has_hist:
        args += [k_hist, v_hist, ki_hist]
        in_specs += [pl.BlockSpec((1, hist_len, HD_C), lambda i, j: (i, 0, 0))] * 2
        in_specs += [pl.BlockSpec((1, hist_len, D_I), lambda i, j: (i, 0, 0))]
    consts = [g_norm.reshape(1, d), w_all, wpool_bd, s_pool.reshape(1, d_d), w_out.astype(BF16), bucket]
    args += [pool_hist] + consts + [rb_shift, tri]
    in_specs += ([pl.BlockSpec((1, POOL_HIST, d_d), lambda i, j: (i, 0, 0))]
                 + [_const_spec(a.shape) for a in consts]
                 + [pl.BlockSpec(memory_space=pltpu.SMEM), _const_spec(tri.shape)])
    kern = functools.partial(_odd_kernel, rows=rows, qb=qb, hist_len=hist_len,
                             n_valid_hist=n_valid_hist, top=top, has_hist=has_hist,
                             single_step=(rows == t))
    return pl.pallas_call(
        kern,
        grid=(b, t // rows),
        in_specs=in_specs,
        out_specs=[
            pl.BlockSpec((1, rows, d), lambda i, j: (i, j, 0)),
            pl.BlockSpec((1, rows, HD_C), lambda i, j: (i, j, 0)),
            pl.BlockSpec((1, rows, HD_C), lambda i, j: (i, j, 0)),
            pl.BlockSpec((1, rows, D_I), lambda i, j: (i, j, 0)),
            pl.BlockSpec((1, POOL_HIST, d_d), lambda i, j: (i, 0, 0)),
        ],
        out_shape=[jax.ShapeDtypeStruct((b, t, d), F32),
                   jax.ShapeDtypeStruct((b, t, HD_C), F32),
                   jax.ShapeDtypeStruct((b, t, HD_C), F32),
                   jax.ShapeDtypeStruct((b, t, D_I), F32),
                   jax.ShapeDtypeStruct((b, POOL_HIST, d_d), F32)],
        scratch_shapes=[
            pltpu.VMEM((n_tiles, kt, lanes), BF16),
            pltpu.VMEM((n_tiles, kt, lanes), BF16),
            pltpu.VMEM((n_tiles, kt, lanes), BF16),
            pltpu.VMEM((n_tiles_pad, rows, kt), jnp.int32),
            pltpu.VMEM((n_groups, SEARCH_GROUP, rows, lanes), jnp.int32),
            pltpu.VMEM((H_C * rows, lanes), F32),
            pltpu.VMEM((H_C * rows, lanes), F32),
            pltpu.VMEM((rows, lanes), F32),
            pltpu.VMEM((rows, lanes), jnp.int32),
            pltpu.VMEM((rows, lanes), F32),
            pltpu.VMEM((3, H_C, qb, lanes), F32),
            pltpu.VMEM((POOL_PAD + rows, d_d), F32),
            pltpu.VMEM((rows, d_c), F32),
            pltpu.VMEM((H_C // 2 * rows, lanes), BF16),
            pltpu.VMEM((H_C // 2 * rows, lanes), BF16),
            pltpu.VMEM((H_I // 2 * rows, lanes), BF16),
            pltpu.VMEM((H_I // 2 * rows, lanes), BF16),
            pltpu.VMEM((H_I, rows, lanes), F32),
        ],
        compiler_params=pltpu.CompilerParams(
            dimension_semantics=("arbitrary", "arbitrary"),
            vmem_limit_bytes=V7X_VMEM_LIMIT_BYTES),
        name="odd_layer",
    )(*args)


def kernel(x_prompt, x_sample, cache_b_conv, cache_c_k, cache_c_v, cache_c_kidx, cache_d_pool, cache_ffn_conv, ln_mix, ln_ffn, ln_final, e_w_in, e_ln_v_g, e_ln_v_b, e_w_s, e_b_s, e_w_dw, e_b_dw, e_ln_c_g, e_ln_c_b, e_w_out, o_w_in, o_w_pool, o_s_pool, o_w_out, rel_bias, f_w_up, f_w_dw, f_b_dw, f_w_down):
    hp, hs = x_prompt, x_sample
    bp, bs = x_prompt.shape[0], x_sample.shape[0]
    ts = x_sample.shape[1]
    depth = ln_mix.shape[0]
    d_ff = f_w_down.shape[1]
    a_s_l, b_p_l, b_s_l = [], [], []
    ck_p_l, cv_p_l, cki_p_l, ck_s_l, cv_s_l, cki_s_l, d_p_l, d_s_l = [], [], [], [], [], [], [], []
    f_p_l, f_s_l = [], []
    for layer in range(depth):
        i = layer // 2
        if layer % 2 == 0:
            ew = (ln_mix[layer], e_w_in[i], e_ln_v_g[i], e_ln_v_b[i], e_w_s[i], e_b_s[i], e_w_dw[i],
                  e_b_dw[i], e_ln_c_g[i], e_ln_c_b[i], e_w_out[i])
            hp, b_p = _even_layer(hp, jnp.zeros((bp, K_B - 1, e_w_dw.shape[-1]), F32), *ew,
                                  n_streams=1, rows=PROMPT_ROWS, emit_v=False)
            hs, a_s, b_s = _even_layer(hs, cache_b_conv[i], *ew, n_streams=bs, rows=ts, emit_v=True)
            a_s_l.append(a_s); b_p_l.append(b_p); b_s_l.append(b_s)
        else:
            ow = (ln_mix[layer], o_w_in[i], o_w_pool[i], o_s_pool[i], o_w_out[i], rel_bias)
            d_d = o_w_pool.shape[1] * o_w_pool.shape[2]
            hp, k_p, v_p, ki_p, d_p = _odd_layer(
                hp, jnp.zeros((bp, 0, HD_C), F32), jnp.zeros((bp, 0, HD_C), F32),
                jnp.zeros((bp, 0, D_I), F32), jnp.zeros((bp, POOL_HIST, d_d), F32), 0, *ow,
                rows=PROMPT_ROWS)
            hs, k_s, v_s, ki_s, d_s = _odd_layer(
                hs, cache_c_k[i], cache_c_v[i], cache_c_kidx[i], cache_d_pool[i], POOL_HIST, *ow,
                rows=ts)
            ck_p_l.append(k_p); cv_p_l.append(v_p); cki_p_l.append(ki_p)
            ck_s_l.append(k_s); cv_s_l.append(v_s); cki_s_l.append(ki_s)
            d_p_l.append(d_p); d_s_l.append(d_s)
        g_final = ln_final if layer == depth - 1 else None
        fw = (ln_ffn[layer], f_w_up[layer], f_w_dw[layer], f_b_dw[layer], f_w_down[layer], g_final)
        hp, f_p = _conv_ffn(hp, jnp.zeros((bp, K_FFN - 1, d_ff), F32), *fw,
                            n_streams=1, rows=PROMPT_ROWS, n_chunks=FFN_CHUNKS)
        hs, f_s = _conv_ffn(hs, cache_ffn_conv[layer], *fw, n_streams=bs, rows=ts, n_chunks=FFN_CHUNKS)
        f_p_l.append(f_p); f_s_l.append(f_s)
    return (hp, hs,
            jnp.stack(a_s_l), jnp.stack(b_p_l), jnp.stack(b_s_l),
            jnp.stack(ck_p_l), jnp.stack(cv_p_l), jnp.stack(cki_p_l),
            jnp.stack(ck_s_l), jnp.stack(cv_s_l), jnp.stack(cki_s_l),
            jnp.stack(d_p_l), jnp.stack(d_s_l),
            jnp.stack(f_p_l), jnp.stack(f_s_l))
```

```python
import functools
import math

import jax
import jax.numpy as jnp
import numpy as np
from jax import lax
from jax.experimental import pallas as pl
from jax.experimental.pallas import tpu as pltpu

F32 = jnp.float32
BF16 = jnp.bfloat16

EPS = 1e-6
CHUNK = 64
CHUNK_SHIFT = CHUNK.bit_length() - 1
SGU_CHUNK = 128
G_A = 4
K_B = 31
H_C = 12
HD_C = 64
H_I = 8
D_I = 64
TOPK_MAX = 256
N_BUCKETS = 32
MAX_DIST = 128
POOL_WINDOWS = (2, 4, 8, 16)
POOL_HIST = 15
K_FFN = 3

V7X_LANES = 128
V7X_SUBLANES = 8
V7X_MXU_DIM = 256
V7X_VMEM_LIMIT_BYTES = 56 * 1024 * 1024

KEY_TILE = V7X_MXU_DIM
PROMPT_ROWS = 512
FFN_CHUNKS = 1
NEG_INF = float("-inf")


def _rms(x, g):
    return x * lax.rsqrt(jnp.mean(x * x, axis=-1, keepdims=True) + EPS) * g


def _layer_norm(x, g, b):
    mu = jnp.mean(x, axis=-1, keepdims=True)
    xc = x - mu
    return xc * lax.rsqrt(jnp.mean(xc * xc, axis=-1, keepdims=True) + EPS) * g + b


def _gelu_tanh(x):
    cdf = 0.5 * (1.0 + jnp.tanh(math.sqrt(2.0 / math.pi) * (x + 0.044715 * (x * x * x))))
    return x * cdf


def _dot(a, b):
    return jnp.dot(a, b, preferred_element_type=F32)


def _dot_nt(a, b):
    return lax.dot_general(a, b, (((1,), (1,)), ((), ())), preferred_element_type=F32)


def _const_spec(shape):
    nd = len(shape)
    return pl.BlockSpec(shape, lambda *_: (0,) * nd, pipeline_mode=pl.Buffered(1))


def _ffn_kernel(*refs, n_streams, rows, n_chunks, final_norm):
    if final_norm:
        (x_ref, hist_ref, g_ref, wu_ref, wd_ref, dw_ref, bdw_ref, gf_ref,
         y_ref, nh_ref, carry_ref) = refs
    else:
        (x_ref, hist_ref, g_ref, wu_ref, wd_ref, dw_ref, bdw_ref,
         y_ref, nh_ref, carry_ref) = refs
        gf_ref = None
    d_model = x_ref.shape[-1]
    d_ff = wd_ref.shape[0]
    fc = d_ff // n_chunks
    m = n_streams * rows

    @pl.when(pl.program_id(1) == 0)
    def _():
        for c in range(n_chunks):
            carry_ref[:, c, 6:8, :] = hist_ref[:, :, c * fc:(c + 1) * fc]

    x = x_ref[...].reshape(m, d_model)
    xn = _rms(x, g_ref[...]).astype(BF16)
    acc = jnp.zeros((m, d_model), F32)
    for c in range(n_chunks):
        cols = slice(c * fc, (c + 1) * fc)
        a = _dot(xn, wu_ref[:, cols])
        val = _dot(xn, wu_ref[:, d_ff + c * fc:d_ff + (c + 1) * fc])
        w = dw_ref[:, cols]
        ys = []
        for s in range(n_streams):
            a_s = a[s * rows:(s + 1) * rows]
            ext = jnp.concatenate([carry_ref[s, c], a_s], axis=0)
            ys.append(w[0:1] * ext[6:6 + rows] + w[1:2] * ext[7:7 + rows] + w[2:3] * a_s)
            carry_ref[s, c] = a_s[rows - 8:rows]
            nh_ref[s, :, c * fc:(c + 1) * fc] = a_s[rows - 2:rows]
        y = (ys[0] if n_streams == 1 else jnp.concatenate(ys, axis=0)) + bdw_ref[:, cols]
        acc = acc + _dot((_gelu_tanh(y) * val).astype(BF16), wd_ref[cols, :])
    out = x + acc
    if final_norm:
        out = _rms(out, gf_ref[...])
    y_ref[...] = out.reshape(y_ref.shape)


def _layer_spec(shape, layer):
    nd = len(shape)
    return pl.BlockSpec((None,) + tuple(shape[1:]), lambda *_: (layer,) + (0,) * (nd - 1),
                        pipeline_mode=pl.Buffered(1))


def _conv_ffn(x, hist, g_norm, w_up_all, w_dw, b_dw, w_down_all, g_final, layer, *, n_streams, rows,
              n_chunks):
    b, t, d = x.shape
    d_ff = w_down_all.shape[1]
    fc = d_ff // n_chunks
    assert fc * n_chunks == d_ff and fc % V7X_LANES == 0
    assert b % n_streams == 0 and t % rows == 0 and rows % V7X_SUBLANES == 0
    final_norm = g_final is not None
    args = [x, hist, g_norm.reshape(1, d), w_up_all, w_down_all, w_dw, b_dw.reshape(1, d_ff)]
    in_specs = [
        pl.BlockSpec((n_streams, rows, d), lambda i, j: (i, j, 0)),
        pl.BlockSpec((n_streams, K_FFN - 1, d_ff), lambda i, j: (i, 0, 0)),
        _const_spec((1, d)),
        _layer_spec(w_up_all.shape, layer), _layer_spec(w_down_all.shape, layer),
        _const_spec(w_dw.shape), _const_spec((1, d_ff)),
    ]
    if final_norm:
        args.append(g_final.reshape(1, d))
        in_specs.append(_const_spec((1, d)))
    kern = functools.partial(_ffn_kernel, n_streams=n_streams, rows=rows, n_chunks=n_chunks,
                             final_norm=final_norm)
    return pl.pallas_call(
        kern,
        grid=(b // n_streams, t // rows),
        in_specs=in_specs,
        out_specs=[
            pl.BlockSpec((n_streams, rows, d), lambda i, j: (i, j, 0)),
            pl.BlockSpec((n_streams, K_FFN - 1, d_ff), lambda i, j: (i, 0, 0)),
        ],
        out_shape=[jax.ShapeDtypeStruct((b, t, d), F32),
                   jax.ShapeDtypeStruct((b, K_FFN - 1, d_ff), F32)],
        scratch_shapes=[pltpu.VMEM((n_streams, n_chunks, 8, fc), F32)],
        compiler_params=pltpu.CompilerParams(
            dimension_semantics=("arbitrary", "arbitrary"),
            vmem_limit_bytes=V7X_VMEM_LIMIT_BYTES),
        name="conv_ffn",
    )(*args)


CONV_PAD = 32
CONV_ROW_BLOCK = 32
CONV_PITCH_PAD = 4
EVEN_SUB_ROWS = 512


def _conv_rows_transposed(glu, s, hist_ref, wdw, bdw, stage_in_ref, stage_out_ref, prev_ref, cout_ref,
                          *, rows, first):
    sub = V7X_SUBLANES
    lanes = V7X_LANES
    nv = rows // sub
    pitch = nv + CONV_PITCH_PAD
    hist_rows = K_B - 1
    d_b = glu.shape[-1]
    sub_id = lax.broadcasted_iota(jnp.int32, (sub, lanes), 0)
    for lt in range(d_b // lanes):
        cols = slice(lt * lanes, (lt + 1) * lanes)

        @pl.when(first)
        def _():
            for e in range(hist_rows):
                prev_ref[s, lt, e, sub - 1:sub, :] = hist_ref[s, e:e + 1, cols]
        for q in range(sub):
            stage_in_ref[lt, q * pitch:q * pitch + nv, :] = (
                glu[s * rows + q * nv:s * rows + (q + 1) * nv, cols])
        cur = [stage_in_ref[lt, pl.ds(v, sub, stride=pitch), :] for v in range(nv)]
        head = []
        for e in range(hist_rows):
            merged = jnp.where(sub_id == sub - 1, prev_ref[s, lt, e], cur[nv - hist_rows + e])
            head.append(pltpu.roll(merged, 1, 0))
        for e in range(hist_rows):
            prev_ref[s, lt, e] = cur[nv - hist_rows + e]
        ext = head + cur
        wk = [jnp.broadcast_to(wdw[k:k + 1, cols], (sub, lanes)) for k in range(K_B)]
        bias = jnp.broadcast_to(bdw[:, cols], (sub, lanes))
        for v in range(nv):
            acc = bias
            for k in range(K_B):
                acc = acc + wk[k] * ext[v + k]
            stage_out_ref[lt, pl.ds(v, sub, stride=pitch), :] = acc
        for q in range(sub):
            cout_ref[s * rows + q * nv:s * rows + (q + 1) * nv, cols] = (
                stage_out_ref[lt, q * pitch:q * pitch + nv, :])


def _even_kernel(*refs, n_streams, rows, sub_rows, sgu_n, emit_v, transposed_conv):
    (x_ref, hist_ref, g_ref, win_ref, lnvg_ref, lnvb_ref, ws_ref, bs_ref, wdw_ref, bdw_ref,
     lncg_ref, lncb_ref, wout_ref) = refs[:13]
    n_out = 3 if emit_v else 2
    y_ref = refs[13]
    av_ref = refs[14] if emit_v else None
    nb_ref = refs[13 + n_out - 1]
    if transposed_conv:
        stage_in_ref, stage_out_ref, prev_ref, cout_ref = refs[13 + n_out:]
    else:
        conv_ref, cout_ref = refs[13 + n_out:]
    d_model = x_ref.shape[-1]
    d_b = wdw_ref.shape[-1]
    d_a = d_b
    dg = d_a // G_A
    hist_rows = K_B - 1
    off = CONV_PAD - hist_rows

    if not transposed_conv:
        @pl.when(pl.program_id(1) == 0)
        def _():
            conv_ref[:, off:CONV_PAD, :] = hist_ref[...]

    lnvg = lnvg_ref[...]
    lnvb = lnvb_ref[...]
    tril = (lax.broadcasted_iota(jnp.int32, (sgu_n, sgu_n), 0)
            >= lax.broadcasted_iota(jnp.int32, (sgu_n, sgu_n), 1))
    ws = [jnp.where(tril, ws_ref[g], 0.0).astype(BF16) for g in range(G_A)]
    bs = bs_ref[...]
    wdw = wdw_ref[...]
    bdw = bdw_ref[...]

    for sub in range(rows // sub_rows):
        r0 = sub * sub_rows
        ms = n_streams * sub_rows
        x = x_ref[:, r0:r0 + sub_rows, :].reshape(ms, d_model)
        xn = _rms(x, g_ref[...]).astype(BF16)
        z = _dot(xn, win_ref[...])

        za = jax.nn.gelu(z[:, :2 * d_a], approximate=True)
        u = za[:, :d_a]
        v = jnp.concatenate(
            [_layer_norm(za[:, d_a + g * dg:d_a + (g + 1) * dg], lnvg[:, g * dg:(g + 1) * dg],
                         lnvb[:, g * dg:(g + 1) * dg]) for g in range(G_A)], axis=1)
        if emit_v:
            av_ref[:, r0:r0 + sub_rows, :] = v.reshape(n_streams, sub_rows, d_a)
        vb = v.astype(BF16)
        sg_rows = []
        for c in range(ms // sgu_n):
            vc = vb[c * sgu_n:(c + 1) * sgu_n]
            sg_rows.append(jnp.concatenate(
                [_dot(ws[g], vc[:, g * dg:(g + 1) * dg]) for g in range(G_A)], axis=1) + bs)
        sg = sg_rows[0] if len(sg_rows) == 1 else jnp.concatenate(sg_rows, axis=0)
        y_a = u * sg

        glu = z[:, 2 * d_a:2 * d_a + d_b] * jax.nn.sigmoid(z[:, 2 * d_a + d_b:])
        for s in range(n_streams):
            if transposed_conv:
                first = pl.program_id(1) == 0 if sub == 0 else False
                _conv_rows_transposed(glu, s, hist_ref, wdw, bdw, stage_in_ref.at[sub],
                                      stage_out_ref.at[sub], prev_ref, cout_ref.at[sub],
                                      rows=sub_rows, first=first)
                nb_ref[s] = glu[(s + 1) * sub_rows - hist_rows:(s + 1) * sub_rows]
                continue
            conv_ref[s, CONV_PAD:CONV_PAD + rows, :] = glu[s * rows:(s + 1) * rows]
            for rb in range(rows // CONV_ROW_BLOCK):
                rr = rb * CONV_ROW_BLOCK
                acc = jnp.zeros((CONV_ROW_BLOCK, d_b), F32) + bdw
                for k in range(K_B):
                    acc = acc + wdw[k:k + 1] * conv_ref[s, rr + off + k:rr + off + k + CONV_ROW_BLOCK, :]
                cout_ref[sub, s * rows + rr:s * rows + rr + CONV_ROW_BLOCK, :] = acc
            nb_ref[s] = conv_ref[s, rows + off:rows + CONV_PAD, :]
            conv_ref[s, 0:CONV_PAD, :] = conv_ref[s, rows:rows + CONV_PAD, :]
        y_b = jax.nn.silu(_layer_norm(cout_ref[sub], lncg_ref[...], lncb_ref[...]))

        y = _dot(jnp.concatenate([y_a, y_b], axis=1).astype(BF16), wout_ref[...])
        y_ref[:, r0:r0 + sub_rows, :] = (x + y).reshape(n_streams, sub_rows, d_model)


def _even_layer(x, conv_hist, g_norm, w_in, ln_v_g, ln_v_b, w_s, b_s, w_dw, b_dw, ln_c_g, ln_c_b,
                w_out, *, n_streams, rows, emit_v):
    b, t, d = x.shape
    d_b = w_dw.shape[-1]
    d_a = ln_v_g.shape[-1]
    dg = d_a // G_A
    sgu_n = min(t, SGU_CHUNK)
    assert b % n_streams == 0 and t % rows == 0 and rows % sgu_n == 0 and (n_streams * rows) % sgu_n == 0
    assert rows % CONV_ROW_BLOCK == 0 and rows >= CONV_PAD and d_a == d_b
    bs_full = jnp.repeat(b_s[:, :sgu_n].T, dg, axis=1)
    args = [x, conv_hist, g_norm.reshape(1, d), w_in.astype(BF16), ln_v_g.reshape(1, d_a),
            ln_v_b.reshape(1, d_a), w_s[:, :sgu_n, :sgu_n], bs_full, w_dw, b_dw.reshape(1, d_b),
            ln_c_g.reshape(1, d_b), ln_c_b.reshape(1, d_b), w_out.astype(BF16)]
    in_specs = [
        pl.BlockSpec((n_streams, rows, d), lambda i, j: (i, j, 0)),
        pl.BlockSpec((n_streams, K_B - 1, d_b), lambda i, j: (i, 0, 0)),
    ] + [_const_spec(a.shape) for a in args[2:]]
    out_specs = [pl.BlockSpec((n_streams, rows, d), lambda i, j: (i, j, 0))]
    out_shape = [jax.ShapeDtypeStruct((b, t, d), F32)]
    if emit_v:
        out_specs.append(pl.BlockSpec((n_streams, rows, d_a), lambda i, j: (i, j, 0)))
        out_shape.append(jax.ShapeDtypeStruct((b, t, d_a), F32))
    out_specs.append(pl.BlockSpec((n_streams, K_B - 1, d_b), lambda i, j: (i, 0, 0)))
    out_shape.append(jax.ShapeDtypeStruct((b, K_B - 1, d_b), F32))
    sub_rows = min(rows, EVEN_SUB_ROWS)
    transposed_conv = sub_rows // V7X_SUBLANES >= K_B - 1
    if not transposed_conv:
        sub_rows = rows
    n_sub = rows // sub_rows
    assert rows % sub_rows == 0 and (n_streams * sub_rows) % sgu_n == 0
    kern = functools.partial(_even_kernel, n_streams=n_streams, rows=rows, sub_rows=sub_rows,
                             sgu_n=sgu_n, emit_v=emit_v, transposed_conv=transposed_conv)
    if transposed_conv:
        stage_shape = (n_sub, d_b // V7X_LANES,
                       V7X_SUBLANES * (sub_rows // V7X_SUBLANES + CONV_PITCH_PAD), V7X_LANES)
        conv_scratch = [pltpu.VMEM(stage_shape, F32), pltpu.VMEM(stage_shape, F32),
                        pltpu.VMEM((n_streams, d_b // V7X_LANES, K_B - 1, V7X_SUBLANES, V7X_LANES), F32)]
    else:
        conv_scratch = [pltpu.VMEM((n_streams, CONV_PAD + rows, d_b), F32)]
    return pl.pallas_call(
        kern,
        grid=(b // n_streams, t // rows),
        in_specs=in_specs,
        out_specs=out_specs,
        out_shape=out_shape,
        scratch_shapes=conv_scratch + [pltpu.VMEM((n_sub, n_streams * sub_rows, d_b), F32)],
        compiler_params=pltpu.CompilerParams(
            dimension_semantics=("arbitrary", "arbitrary"),
            vmem_limit_bytes=V7X_VMEM_LIMIT_BYTES),
        name="even_layer",
    )(*args)


POOL_PAD = 16
ATTN_HALF_ROWS = 256
SEARCH_GROUP = 16
SEARCH_ROWS = 16
INT_MIN = -2 ** 31
NEG_INF_KEY = -2 ** 31 + 0x7FFFFF
LOG2E = math.log2(math.e)
M_INIT = -3.0e38


def _t5_bucket(rel):
    nb = N_BUCKETS // 2
    exact = nb // 2
    side = jnp.where(rel > 0, nb, 0)
    n = jnp.abs(rel)
    large = exact + (jnp.log(jnp.maximum(n, 1).astype(jnp.float32) / exact)
                     / math.log(MAX_DIST / exact) * (nb - exact)).astype(jnp.int32)
    large = jnp.minimum(large, nb - 1)
    return side + jnp.where(n < exact, n, large)


def _sortable(score):
    bits = pltpu.bitcast(score, jnp.int32)
    return bits ^ ((bits >> 31) & 0x7FFFFFFF)


def _for_range(lo, hi, body):
    if isinstance(lo, int) and isinstance(hi, int):
        for j in range(lo, hi):
            body(j)
    else:
        def step(j, carry):
            body(j)
            return carry
        lax.fori_loop(lo, hi, step, 0)


def _odd_kernel(*refs, rows, qb, hist_len, n_valid_hist, top, has_hist, single_step):
    refs = list(refs)
    x_ref = refs.pop(0)
    if has_hist:
        kht_ref, vht_ref, kiht_ref = refs[:3]
        kkt_ref, kk2t_ref, vvt_ref = refs[-3:]
        refs = refs[3:-3]
    (poolh_ref, g_ref, win_ref, wpool_ref, spool_ref, wout_ref, bucket_ref, rb_ref, tri_ref,
     y_ref, k_ref, v_ref, ki_ref, np_ref,
     kk_ref, kk2_ref, vv_ref, keys_ref, planes_ref, acc_ref, m_ref, off_ref, thr_ref, need_ref,
     bias_ref, pool_ref, yc_ref, yd_ref, qe_ref, qo_ref, ie_ref, io_ref, wb_ref) = refs

    def dot_keys_t(lhs, nat_ref, t_ref, j):
        if has_hist and j < hist_len // KEY_TILE:
            return _dot(lhs, t_ref[j])
        return _dot_nt(lhs, nat_ref[j])

    def dot_values(p, j):
        if has_hist and j < hist_len // KEY_TILE:
            return _dot_nt(p, vvt_ref[j])
        return _dot(p, vv_ref[j])
    d_model = x_ref.shape[-1]
    d_c = H_C * HD_C
    d_qi = H_I * D_I
    d_d = wpool_ref.shape[0]
    n_pairs = H_C // 2
    n_ipairs = H_I // 2
    kt = KEY_TILE
    lanes = V7X_LANES
    t = pl.program_id(1)
    t0 = 0 if single_step else t * rows
    hist_tiles = hist_len // kt
    col_xd = d_c + d_qi

    @pl.when(jnp.logical_and(pl.program_id(0) == 0, t == 0))
    def _():
        bucket = bucket_ref[...]
        for h in range(H_C):
            b_acc = jnp.zeros(bucket.shape, F32)
            for b in range(N_BUCKETS):
                b_acc = jnp.where(bucket == b, rb_ref[b, h] * LOG2E, b_acc)
            bias_ref[0:2, h] = b_acc
            bias_ref[2, h] = jnp.zeros(bucket.shape[1:], F32)

    @pl.when(t == 0)
    def _():
        pool_ref[POOL_PAD - POOL_HIST:POOL_PAD, :] = poolh_ref[0]
        if has_hist:
            ones_half = jnp.ones((lanes - HD_C, kt), F32)
            for j in range(hist_tiles):
                kj = kht_ref[0, :, j * kt:(j + 1) * kt]
                kij = kiht_ref[0, :, j * kt:(j + 1) * kt]
                vj = vht_ref[0, :, j * kt:(j + 1) * kt]
                kkt_ref[j] = jnp.concatenate([kj, kij], axis=0).astype(BF16)
                kk2t_ref[j] = jnp.concatenate([kij, kj], axis=0).astype(BF16)
                vvt_ref[j] = jnp.concatenate([vj, ones_half], axis=0).astype(BF16)

    x = x_ref[0]
    xn = _rms(x, g_ref[...]).astype(BF16)
    z_tail = _dot(xn, win_ref[:, col_xd:])
    z = _dot(xn, win_ref[:, :col_xd])

    xd = z_tail[:, :d_d]
    pool_ref[POOL_PAD:POOL_PAD + rows, :] = xd
    np_ref[0] = pool_ref[rows + POOL_PAD - POOL_HIST:rows + POOL_PAD, :]
    run = xd
    wins = {}
    for dshift in range(1, POOL_WINDOWS[-1]):
        run = run + pool_ref[POOL_PAD - dshift:POOL_PAD - dshift + rows, :]
        if dshift + 1 in POOL_WINDOWS:
            wins[dshift + 1] = run
    pool_ref[0:POOL_PAD, :] = pool_ref[rows:rows + POOL_PAD, :]
    dg_d = d_d // len(POOL_WINDOWS)
    lane_d = lax.broadcasted_iota(jnp.int32, (rows, d_d), 1)
    tpos = n_valid_hist + t0 + 1 + lax.broadcasted_iota(jnp.int32, (rows, d_d), 0)
    win_sum = wins[POOL_WINDOWS[-1]]
    width = jnp.full((rows, d_d), POOL_WINDOWS[-1], jnp.int32)
    for gi in range(len(POOL_WINDOWS) - 2, -1, -1):
        in_g = lane_d < (gi + 1) * dg_d
        win_sum = jnp.where(in_g, wins[POOL_WINDOWS[gi]], win_sum)
        width = jnp.where(in_g, POOL_WINDOWS[gi], width)
    count = jnp.minimum(tpos, width).astype(F32)
    m_pool = win_sum / count - xd
    yd_ref[...] = _dot(m_pool.astype(BF16), wpool_ref[...]) * spool_ref[...]

    g1 = z_tail[:, d_d:d_d + lanes]
    g2 = z_tail[:, d_d + lanes:]
    k_ref[0] = g1[:, :HD_C]
    v_ref[0] = g1[:, HD_C:]
    ki_ref[0] = g2[:, :D_I]
    lane = lax.broadcasted_iota(jnp.int32, (rows, lanes), 1)
    low = lane < HD_C
    g1r = pltpu.roll(g1, HD_C, 1)
    g2r = pltpu.roll(g2, D_I, 1)
    kk_new = jnp.where(low, g1, g2r).astype(BF16)
    kk2_new = jnp.where(low, g2, g1r).astype(BF16)
    vv_new = jnp.where(low, g1r, 1.0).astype(BF16)
    if rows % kt == 0:
        base_tile = (hist_len + t0) // kt
        for i in range(rows // kt):
            kk_ref[base_tile + i] = kk_new[i * kt:(i + 1) * kt]
            kk2_ref[base_tile + i] = kk2_new[i * kt:(i + 1) * kt]
            vv_ref[base_tile + i] = vv_new[i * kt:(i + 1) * kt]
    else:
        zpad = jnp.zeros((kt - rows, lanes), BF16)
        kk_ref[hist_tiles] = jnp.concatenate([kk_new, zpad], axis=0)
        kk2_ref[hist_tiles] = jnp.concatenate([kk2_new, zpad], axis=0)
        vv_ref[hist_tiles] = jnp.concatenate([vv_new, zpad], axis=0)
    kv_len = hist_len + t0 + rows

    w_idx = g2[:, D_I:D_I + H_I] * ((H_I ** -0.5) * (D_I ** -0.5))
    tri = tri_ref[...]
    ones_rhs = jnp.ones((kt, lanes), BF16)
    hb = min(rows, ATTN_HALF_ROWS)
    n_half = rows // hb
    n_new = max(rows // kt, 1)
    low_h = lax.broadcasted_iota(jnp.int32, (hb, lanes), 1) < HD_C

    jb = (hist_len + t0) // kt
    n_tiles = jb + n_new

    for hf in range(n_half):
        zr = z[hf * hb:(hf + 1) * hb]
        for g in range(n_pairs):
            grp = zr[:, g * lanes:(g + 1) * lanes] * (HD_C ** -0.5 * LOG2E)
            qe_ref[(hf * n_pairs + g) * hb:(hf * n_pairs + g + 1) * hb] = (
                jnp.where(low_h, grp, 0.0).astype(BF16))
            qo_ref[(hf * n_pairs + g) * hb:(hf * n_pairs + g + 1) * hb] = (
                jnp.where(low_h, 0.0, grp).astype(BF16))
    for g in range(n_ipairs):
        grp = z[:, d_c + g * lanes:d_c + (g + 1) * lanes]
        ie_ref[g * rows:(g + 1) * rows] = jnp.where(low, grp, 0.0).astype(BF16)
        io_ref[g * rows:(g + 1) * rows] = jnp.where(low, 0.0, grp).astype(BF16)
    for h in range(H_I):
        wb_ref[h] = jnp.broadcast_to(w_idx[:, h:h + 1], (rows, lanes))
    qchunk = (hist_len + t0 + lax.broadcasted_iota(jnp.int32, (rows, 1), 0)) >> CHUNK_SHIFT
    kpos_rel = lax.broadcasted_iota(jnp.int32, (rows, kt), 1)

    def score_body(j):
        se = jnp.maximum(dot_keys_t(ie_ref[...], kk2_ref, kk2t_ref if has_hist else None, j), 0.0)
        so = jnp.maximum(dot_keys_t(io_ref[...], kk_ref, kkt_ref if has_hist else None, j), 0.0)
        score = jnp.zeros((rows, kt), F32)
        for g in range(n_ipairs):
            we = wb_ref[2 * g]
            wo = wb_ref[2 * g + 1]
            score = score + se[g * rows:(g + 1) * rows] * jnp.concatenate([we] * (kt // lanes), axis=1)
            score = score + so[g * rows:(g + 1) * rows] * jnp.concatenate([wo] * (kt // lanes), axis=1)
        kpos = kpos_rel + j * kt
        adm = jnp.logical_and((kpos >> CHUNK_SHIFT) <= qchunk, kpos < kv_len)
        keys_ref[j] = _sortable(jnp.where(adm, score, NEG_INF))
    _for_range(0, n_tiles, score_body)

    def fill_body(j):
        keys_ref[j] = jnp.full((rows, kt), INT_MIN, jnp.int32)
    _for_range(n_tiles, keys_ref.shape[0], fill_body)

    n_groups = keys_ref.shape[0] * (kt // lanes) // SEARCH_GROUP

    def plane_body(rc, carry):
        r = pl.multiple_of(rc * SEARCH_ROWS, SEARCH_ROWS)
        for gi in range(n_groups):
            a = []
            for i in range(SEARCH_GROUP):
                lt = gi * SEARCH_GROUP + i
                a.append(keys_ref[lt // (kt // lanes), pl.ds(r, SEARCH_ROWS),
                                  (lt % (kt // lanes)) * lanes:(lt % (kt // lanes) + 1) * lanes])
            for sh, msk in ((8, 0x00FF00FF), (4, 0x0F0F0F0F), (2, 0x33333333), (1, 0x55555555)):
                for k in range(SEARCH_GROUP):
                    if k & sh == 0:
                        tmp = (a[k] ^ lax.shift_right_logical(a[k + sh], sh)) & msk
                        a[k] = a[k] ^ tmp
                        a[k + sh] = a[k + sh] ^ (tmp << sh)
            for w in range(SEARCH_GROUP):
                planes_ref[gi, w, pl.ds(r, SEARCH_ROWS), :] = a[w]
        return carry
    lax.fori_loop(0, rows // SEARCH_ROWS, plane_body, 0)

    alive = [jnp.full((rows, lanes), -65536, jnp.int32) for _ in range(n_groups)]
    above = jnp.zeros((rows, 1), F32)
    thr_u = jnp.zeros((rows, 1), jnp.int32)
    for b in range(31, -1, -1):
        w = (31 - b) if b >= 16 else (15 - b)
        if b == 15:
            alive = [lax.shift_right_logical(a, 16) for a in alive]
        ones = []
        for gi in range(n_groups):
            plane = planes_ref[gi, w]
            if b == 31:
                plane = ~plane
            ones.append(alive[gi] & plane)
        pc = lax.population_count(ones[0])
        for gi in range(1, n_groups):
            pc = pc + lax.population_count(ones[gi])
        cnt = jnp.sum(pc.astype(F32), axis=1, keepdims=True)
        take = (above + cnt) >= float(top)
        alive = [jnp.where(take, o, a ^ o) for a, o in zip(alive, ones)]
        above = jnp.where(take, above, above + cnt)
        thr_u = jnp.where(take, thr_u | jnp.int32(INT_MIN if b == 31 else (1 << b)), thr_u)
    thr_ref[...] = jnp.broadcast_to(thr_u ^ jnp.int32(INT_MIN), (rows, lanes))
    need_ref[...] = jnp.broadcast_to(float(top) - above, (rows, lanes))
    off_ref[...] = jnp.zeros((rows, lanes), F32)
    m_ref[...] = jnp.full(m_ref.shape, M_INIT, F32)
    acc_ref[...] = jnp.zeros(acc_ref.shape, F32)

    def attn_half(j, hf, dj):
        rs = slice(hf * hb, (hf + 1) * hb)
        kj = keys_ref[j, rs]
        thr_t = jnp.concatenate([thr_ref[rs]] * (kt // lanes), axis=1)
        need_t = jnp.concatenate([need_ref[rs]] * (kt // lanes), axis=1)
        eq = jnp.where(kj == thr_t, 1.0, 0.0)
        eq_b = eq.astype(BF16)
        rank = _dot(eq_b, tri) + jnp.concatenate([off_ref[rs]] * (kt // lanes), axis=1)
        self_ = jnp.where(kj > thr_t, 1.0, jnp.where(rank <= need_t, eq, 0.0))
        sel = jnp.where(kj > NEG_INF_KEY, self_, 0.0) > 0.5
        mask_add = jnp.where(sel, 0.0, NEG_INF)
        off_ref[rs] = off_ref[rs] + _dot(eq_b, ones_rhs)
        q_rows = slice(hf * n_pairs * hb, (hf + 1) * n_pairs * hb)
        lg = (dot_keys_t(qe_ref[q_rows], kk_ref, kkt_ref if has_hist else None, j),
              dot_keys_t(qo_ref[q_rows], kk2_ref, kk2t_ref if has_hist else None, j))
        ps = []
        alphas = []
        for eo in range(2):
            for g in range(n_pairs):
                h = 2 * g + eo
                l = lg[eo][g * hb:(g + 1) * hb]
                if dj is not None:
                    row_parts = []
                    for sub in range(hb // qb):
                        sblk = ((hf * (hb // qb) + sub) * qb) // lanes
                        parts = []
                        for c in range(kt // lanes):
                            part = l[sub * qb:(sub + 1) * qb, c * lanes:(c + 1) * lanes]
                            d = (kt // lanes) * dj + c - sblk
                            if isinstance(d, int):
                                if d in (-1, 0):
                                    part = part + bias_ref[d + 1, h]
                            else:
                                entry = jnp.where(d == -1, 0, jnp.where(d == 0, 1, 2))
                                part = part + bias_ref[entry, h]
                            parts.append(part)
                        row_parts.append(jnp.concatenate(parts, axis=1))
                    l = row_parts[0] if len(row_parts) == 1 else jnp.concatenate(row_parts, axis=0)
                l = l + mask_add
                st = slice(((hf * 2 + eo) * n_pairs + g) * hb, ((hf * 2 + eo) * n_pairs + g + 1) * hb)
                m_old = m_ref[st]
                m_new = jnp.maximum(m_old, jnp.max(l, axis=1, keepdims=True))
                alphas.append(jnp.exp2(m_old - m_new))
                ps.append(jnp.exp2(l - jnp.concatenate([m_new] * (kt // lanes), axis=1)).astype(BF16))
                m_ref[st] = m_new
        a_rows = slice(hf * H_C * hb, (hf + 1) * H_C * hb)
        acc_ref[a_rows] = (acc_ref[a_rows] * jnp.concatenate(alphas, axis=0)
                           + dot_values(jnp.concatenate(ps, axis=0), j))

    def far_body(j):
        for hf in range(n_half):
            attn_half(j, hf, None)
    _for_range(0, jb - 1, far_body)

    def near_body(j):
        for hf in range(n_half):
            attn_half(j, hf, j - jb)
    _for_range(max(jb - 1, 0) if single_step else jnp.maximum(jb - 1, 0), jb + 1, near_body)

    for dj in range(1, n_new):
        for hf in range(n_half):
            if ((hf + 1) * hb - qb) // lanes >= (kt // lanes) * dj:
                attn_half(jb + dj, hf, dj)

    for hf in range(n_half):
        for g in range(n_pairs):
            oe = acc_ref[((hf * 2) * n_pairs + g) * hb:((hf * 2) * n_pairs + g + 1) * hb]
            oo = acc_ref[((hf * 2 + 1) * n_pairs + g) * hb:((hf * 2 + 1) * n_pairs + g + 1) * hb]
            num = jnp.where(low_h, oe, pltpu.roll(oo, HD_C, 1))
            den = jnp.where(low_h, pltpu.roll(oe, HD_C, 1), oo)
            yc_ref[hf * hb:(hf + 1) * hb, g * lanes:(g + 1) * lanes] = num / den


    y_cat = jnp.concatenate([yc_ref[...], yd_ref[...]], axis=1).astype(BF16)
    y_ref[0] = x + _dot(y_cat, wout_ref[...])


def _odd_layer(x, k_hist, v_hist, ki_hist, pool_hist, n_valid_hist, g_norm, w_in, w_pool, s_pool,
               w_out, rel_bias, *, rows):
    b, t, d = x.shape
    hist_len = k_hist.shape[1]
    has_hist = hist_len > 0
    d_c = H_C * HD_C
    d_qi = H_I * D_I
    d_d = d - d_c
    qb = min(2 * CHUNK, rows)
    kt = KEY_TILE
    lanes = V7X_LANES
    s_total = hist_len + t
    top = min(TOPK_MAX, s_total // 4)
    assert t % rows == 0 and rows % qb == 0 and hist_len % kt == 0
    assert rows % kt == 0 or (rows == t and rows == qb and rows <= CHUNK)
    n_tiles = (s_total + kt - 1) // kt
    tiles_per_group = SEARCH_GROUP * lanes // kt
    n_groups = (n_tiles + tiles_per_group - 1) // tiles_per_group
    n_tiles_pad = n_groups * tiles_per_group
    assert rows % SEARCH_ROWS == 0 and rows % min(rows, ATTN_HALF_ROWS) == 0

    offs = np.cumsum([0, d_c, HD_C, HD_C, d_qi, D_I, H_I]).tolist()
    q_w, k_w, v_w, qi_w, ki_w, wi_w = (w_in[:, offs[i]:offs[i + 1]] for i in range(6))
    xd_w = w_in[:, offs[6]:]
    pad_w = jnp.zeros((d, lanes - D_I - H_I), w_in.dtype)
    w_all = jnp.concatenate([q_w, qi_w, xd_w, k_w, v_w, ki_w, wi_w, pad_w], axis=1).astype(BF16)
    wpool_bd = jax.scipy.linalg.block_diag(*[w_pool[g] for g in range(w_pool.shape[0])]).astype(BF16)

    rel = (lanes * jnp.arange(-1, 1, dtype=jnp.int32)[:, None, None]
           + jnp.arange(lanes, dtype=jnp.int32)[None, None, :]
           - jnp.arange(qb, dtype=jnp.int32)[None, :, None])
    bucket = _t5_bucket(rel)
    far_bucket = _t5_bucket(jnp.int32(-2 * lanes))
    rb_shift = rel_bias - rel_bias[far_bucket][None, :]
    tri = (jnp.arange(kt)[:, None] <= jnp.arange(kt)[None, :]).astype(BF16)

    args = [x]
    in_specs = [pl.BlockSpec((1, rows, d), lambda i, j: (i, j, 0))]
    if has_hist:
        assert rows == t, "history tiles are addressed statically"
        args += [jnp.swapaxes(k_hist, 1, 2), jnp.swapaxes(v_hist, 1, 2), jnp.swapaxes(ki_hist, 1, 2)]
        in_specs += [pl.BlockSpec((1, HD_C, hist_len), lambda i, j: (i, 0, 0))] * 2
        in_specs += [pl.BlockSpec((1, D_I, hist_len), lambda i, j: (i, 0, 0))]
    consts = [g_norm.reshape(1, d), w_all, wpool_bd, s_pool.reshape(1, d_d), w_out.astype(BF16), bucket]
    args += [pool_hist] + consts + [rb_shift, tri]
    in_specs += ([pl.BlockSpec((1, POOL_HIST, d_d), lambda i, j: (i, 0, 0))]
                 + [_const_spec(a.shape) for a in consts]
                 + [pl.BlockSpec(memory_space=pltpu.SMEM), _const_spec(tri.shape)])
    kern = functools.partial(_odd_kernel, rows=rows, qb=qb, hist_len=hist_len,
                             n_valid_hist=n_valid_hist, top=top, has_hist=has_hist,
                             single_step=(rows == t))
    return pl.pallas_call(
        kern,
        grid=(b, t // rows),
        in_specs=in_specs,
        out_specs=[
            pl.BlockSpec((1, rows, d), lambda i, j: (i, j, 0)),
            pl.BlockSpec((1, rows, HD_C), lambda i, j: (i, j, 0)),
            pl.BlockSpec((1, rows, HD_C), lambda i, j: (i, j, 0)),
            pl.BlockSpec((1, rows, D_I), lambda i, j: (i, j, 0)),
            pl.BlockSpec((1, POOL_HIST, d_d), lambda i, j: (i, 0, 0)),
        ],
        out_shape=[jax.ShapeDtypeStruct((b, t, d), F32),
                   jax.ShapeDtypeStruct((b, t, HD_C), F32),
                   jax.ShapeDtypeStruct((b, t, HD_C), F32),
                   jax.ShapeDtypeStruct((b, t, D_I), F32),
                   jax.ShapeDtypeStruct((b, POOL_HIST, d_d), F32)],
        scratch_shapes=[
            pltpu.VMEM((n_tiles, kt, lanes), BF16),
            pltpu.VMEM((n_tiles, kt, lanes), BF16),
            pltpu.VMEM((n_tiles, kt, lanes), BF16),
            pltpu.VMEM((n_tiles_pad, rows, kt), jnp.int32),
            pltpu.VMEM((n_groups, SEARCH_GROUP, rows, lanes), jnp.int32),
            pltpu.VMEM((H_C * rows, lanes), F32),
            pltpu.VMEM((H_C * rows, lanes), F32),
            pltpu.VMEM((rows, lanes), F32),
            pltpu.VMEM((rows, lanes), jnp.int32),
            pltpu.VMEM((rows, lanes), F32),
            pltpu.VMEM((3, H_C, qb, lanes), F32),
            pltpu.VMEM((POOL_PAD + rows, d_d), F32),
            pltpu.VMEM((rows, d_c), F32),
            pltpu.VMEM((rows, d_d), F32),
            pltpu.VMEM((H_C // 2 * rows, lanes), BF16),
            pltpu.VMEM((H_C // 2 * rows, lanes), BF16),
            pltpu.VMEM((H_I // 2 * rows, lanes), BF16),
            pltpu.VMEM((H_I // 2 * rows, lanes), BF16),
            pltpu.VMEM((H_I, rows, lanes), F32),
        ] + ([pltpu.VMEM((hist_len // kt, lanes, kt), BF16)] * 3 if has_hist else []),
        compiler_params=pltpu.CompilerParams(
            dimension_semantics=("arbitrary", "arbitrary"),
            vmem_limit_bytes=V7X_VMEM_LIMIT_BYTES),
        name="odd_layer",
    )(*args)


def kernel(x_prompt, x_sample, cache_b_conv, cache_c_k, cache_c_v, cache_c_kidx, cache_d_pool, cache_ffn_conv, ln_mix, ln_ffn, ln_final, e_w_in, e_ln_v_g, e_ln_v_b, e_w_s, e_b_s, e_w_dw, e_b_dw, e_ln_c_g, e_ln_c_b, e_w_out, o_w_in, o_w_pool, o_s_pool, o_w_out, rel_bias, f_w_up, f_w_dw, f_b_dw, f_w_down):
    hp, hs = x_prompt, x_sample
    bp, bs = x_prompt.shape[0], x_sample.shape[0]
    ts = x_sample.shape[1]
    depth = ln_mix.shape[0]
    d_ff = f_w_down.shape[1]
    a_s_l, b_p_l, b_s_l = [], [], []
    ck_p_l, cv_p_l, cki_p_l, ck_s_l, cv_s_l, cki_s_l, d_p_l, d_s_l = [], [], [], [], [], [], [], []
    f_p_l, f_s_l = [], []
    f_w_up_b = f_w_up.astype(BF16)
    f_w_down_b = f_w_down.astype(BF16)
    for layer in range(depth):
        i = layer // 2
        if layer % 2 == 0:
            ew = (ln_mix[layer], e_w_in[i], e_ln_v_g[i], e_ln_v_b[i], e_w_s[i], e_b_s[i], e_w_dw[i],
                  e_b_dw[i], e_ln_c_g[i], e_ln_c_b[i], e_w_out[i])
            hp, b_p = _even_layer(hp, jnp.zeros((bp, K_B - 1, e_w_dw.shape[-1]), F32), *ew,
                                  n_streams=1, rows=PROMPT_ROWS, emit_v=False)
            hs, a_s, b_s = _even_layer(hs, cache_b_conv[i], *ew, n_streams=bs, rows=ts, emit_v=True)
            a_s_l.append(a_s); b_p_l.append(b_p); b_s_l.append(b_s)
        else:
            ow = (ln_mix[layer], o_w_in[i], o_w_pool[i], o_s_pool[i], o_w_out[i], rel_bias)
            d_d = o_w_pool.shape[1] * o_w_pool.shape[2]
            hp, k_p, v_p, ki_p, d_p = _odd_layer(
                hp, jnp.zeros((bp, 0, HD_C), F32), jnp.zeros((bp, 0, HD_C), F32),
                jnp.zeros((bp, 0, D_I), F32), jnp.zeros((bp, POOL_HIST, d_d), F32), 0, *ow,
                rows=PROMPT_ROWS)
            hs, k_s, v_s, ki_s, d_s = _odd_layer(
                hs, cache_c_k[i], cache_c_v[i], cache_c_kidx[i], cache_d_pool[i], POOL_HIST, *ow,
                rows=ts)
            ck_p_l.append(k_p); cv_p_l.append(v_p); cki_p_l.append(ki_p)
            ck_s_l.append(k_s); cv_s_l.append(v_s); cki_s_l.append(ki_s)
            d_p_l.append(d_p); d_s_l.append(d_s)
        g_final = ln_final if layer == depth - 1 else None
        fw = (ln_ffn[layer], f_w_up_b, f_w_dw[layer], f_b_dw[layer], f_w_down_b, g_final, layer)
        hp, f_p = _conv_ffn(hp, jnp.zeros((bp, K_FFN - 1, d_ff), F32), *fw,
                            n_streams=1, rows=PROMPT_ROWS, n_chunks=FFN_CHUNKS)
        hs, f_s = _conv_ffn(hs, cache_ffn_conv[layer], *fw, n_streams=bs, rows=ts, n_chunks=FFN_CHUNKS)
        f_p_l.append(f_p); f_s_l.append(f_s)
    return (hp, hs,
            jnp.stack(a_s_l), jnp.stack(b_p_l), jnp.stack(b_s_l),
            jnp.stack(ck_p_l), jnp.stack(cv_p_l), jnp.stack(cki_p_l),
            jnp.stack(ck_s_l), jnp.stack(cv_s_l), jnp.stack(cki_s_l),
            jnp.stack(d_p_l), jnp.stack(d_s_l),
            jnp.stack(f_p_l), jnp.stack(f_s_l))
```

```python
import functools
import math

import jax
import jax.numpy as jnp
import numpy as np
from jax import lax
from jax.experimental import pallas as pl
from jax.experimental.pallas import tpu as pltpu

F32 = jnp.float32
BF16 = jnp.bfloat16

EPS = 1e-6
CHUNK = 64
CHUNK_SHIFT = CHUNK.bit_length() - 1
SGU_CHUNK = 128
G_A = 4
K_B = 31
H_C = 12
HD_C = 64
H_I = 8
D_I = 64
TOPK_MAX = 256
N_BUCKETS = 32
MAX_DIST = 128
POOL_WINDOWS = (2, 4, 8, 16)
POOL_HIST = 15
K_FFN = 3

V7X_LANES = 128
V7X_SUBLANES = 8
V7X_MXU_DIM = 256
V7X_VMEM_LIMIT_BYTES = 56 * 1024 * 1024

KEY_TILE = V7X_MXU_DIM
PROMPT_ROWS = 512
NEG_INF = float("-inf")


def _rms(x, g):
    return x * lax.rsqrt(jnp.mean(x * x, axis=-1, keepdims=True) + EPS) * g


def _layer_norm(x, g, b):
    mu = jnp.mean(x, axis=-1, keepdims=True)
    xc = x - mu
    return xc * lax.rsqrt(jnp.mean(xc * xc, axis=-1, keepdims=True) + EPS) * g + b


def _gelu_tanh(x):
    cdf = 0.5 * (1.0 + jnp.tanh(math.sqrt(2.0 / math.pi) * (x + 0.044715 * (x * x * x))))
    return x * cdf


def _dot(a, b):
    return jnp.dot(a, b, preferred_element_type=F32)


def _dot_nt(a, b):
    return lax.dot_general(a, b, (((1,), (1,)), ((), ())), preferred_element_type=F32)


def _const_spec(shape):
    nd = len(shape)
    return pl.BlockSpec(shape, lambda *_: (0,) * nd, pipeline_mode=pl.Buffered(1))


def _ffn_kernel(*refs, n_streams, rows, final_norm):
    if final_norm:
        (x_ref, hist_ref, g_ref, wu_ref, wd_ref, dw_ref, bdw_ref, gf_ref,
         y_ref, nh_ref, carry_ref) = refs
    else:
        (x_ref, hist_ref, g_ref, wu_ref, wd_ref, dw_ref, bdw_ref,
         y_ref, nh_ref, carry_ref) = refs
        gf_ref = None
    d_model = x_ref.shape[-1]
    d_ff = wd_ref.shape[0]
    m = n_streams * rows
    sub = V7X_SUBLANES
    n_hist = K_FFN - 1

    @pl.when(pl.program_id(1) == 0)
    def _():
        carry_ref[:, sub - n_hist:sub, :] = hist_ref[...]

    x = x_ref[...].reshape(m, d_model)
    xn = _rms(x, g_ref[...]).astype(BF16)
    a = _dot(xn, wu_ref[:, :d_ff])
    val = _dot(xn, wu_ref[:, d_ff:])
    w = dw_ref[...]
    ys = []
    for s in range(n_streams):
        a_s = a[s * rows:(s + 1) * rows]
        ext = jnp.concatenate([carry_ref[s], a_s], axis=0)
        y_s = w[n_hist:K_FFN] * a_s
        for k in range(n_hist):
            y_s = y_s + w[k:k + 1] * ext[sub - n_hist + k:sub - n_hist + k + rows]
        ys.append(y_s)
        carry_ref[s] = a_s[rows - sub:rows]
        nh_ref[s] = a_s[rows - n_hist:rows]
    y = (ys[0] if n_streams == 1 else jnp.concatenate(ys, axis=0)) + bdw_ref[...]
    out = x + _dot((_gelu_tanh(y) * val).astype(BF16), wd_ref[...])
    if final_norm:
        out = _rms(out, gf_ref[...])
    y_ref[...] = out.reshape(y_ref.shape)


def _layer_spec(shape, layer):
    nd = len(shape)
    return pl.BlockSpec((None,) + tuple(shape[1:]), lambda *_: (layer,) + (0,) * (nd - 1),
                        pipeline_mode=pl.Buffered(1))


def _conv_ffn(x, hist, g_norm, w_up_all, w_dw, b_dw, w_down_all, g_final, layer, *, n_streams, rows):
    b, t, d = x.shape
    d_ff = w_down_all.shape[1]
    assert d_ff % V7X_LANES == 0
    assert b % n_streams == 0 and t % rows == 0 and rows % V7X_SUBLANES == 0
    final_norm = g_final is not None
    args = [x, hist, g_norm.reshape(1, d), w_up_all, w_down_all, w_dw, b_dw.reshape(1, d_ff)]
    in_specs = [
        pl.BlockSpec((n_streams, rows, d), lambda i, j: (i, j, 0)),
        pl.BlockSpec((n_streams, K_FFN - 1, d_ff), lambda i, j: (i, 0, 0)),
        _const_spec((1, d)),
        _layer_spec(w_up_all.shape, layer), _layer_spec(w_down_all.shape, layer),
        _const_spec(w_dw.shape), _const_spec((1, d_ff)),
    ]
    if final_norm:
        args.append(g_final.reshape(1, d))
        in_specs.append(_const_spec((1, d)))
    kern = functools.partial(_ffn_kernel, n_streams=n_streams, rows=rows, final_norm=final_norm)
    return pl.pallas_call(
        kern,
        grid=(b // n_streams, t // rows),
        in_specs=in_specs,
        out_specs=[
            pl.BlockSpec((n_streams, rows, d), lambda i, j: (i, j, 0)),
            pl.BlockSpec((n_streams, K_FFN - 1, d_ff), lambda i, j: (i, 0, 0)),
        ],
        out_shape=[jax.ShapeDtypeStruct((b, t, d), F32),
                   jax.ShapeDtypeStruct((b, K_FFN - 1, d_ff), F32)],
        scratch_shapes=[pltpu.VMEM((n_streams, V7X_SUBLANES, d_ff), F32)],
        compiler_params=pltpu.CompilerParams(
            dimension_semantics=("arbitrary", "arbitrary"),
            vmem_limit_bytes=V7X_VMEM_LIMIT_BYTES),
        name="conv_ffn",
    )(*args)


CONV_PAD = 32
CONV_ROW_BLOCK = 32
CONV_PITCH_PAD = 4


def _conv_rows_transposed(glu, s, hist_ref, wdw, bdw, stage_in_ref, stage_out_ref, prev_ref, cout_ref,
                          *, rows, first):
    sub = V7X_SUBLANES
    lanes = V7X_LANES
    nv = rows // sub
    pitch = nv + CONV_PITCH_PAD
    hist_rows = K_B - 1
    d_b = glu.shape[-1]
    sub_id = lax.broadcasted_iota(jnp.int32, (sub, lanes), 0)
    for lt in range(d_b // lanes):
        cols = slice(lt * lanes, (lt + 1) * lanes)

        @pl.when(first)
        def _():
            for e in range(hist_rows):
                prev_ref[s, lt, e, sub - 1:sub, :] = hist_ref[s, e:e + 1, cols]
        for q in range(sub):
            stage_in_ref[lt, q * pitch:q * pitch + nv, :] = (
                glu[s * rows + q * nv:s * rows + (q + 1) * nv, cols])
        cur = [stage_in_ref[lt, pl.ds(v, sub, stride=pitch), :] for v in range(nv)]
        head = []
        for e in range(hist_rows):
            merged = jnp.where(sub_id == sub - 1, prev_ref[s, lt, e], cur[nv - hist_rows + e])
            head.append(pltpu.roll(merged, 1, 0))
        for e in range(hist_rows):
            prev_ref[s, lt, e] = cur[nv - hist_rows + e]
        ext = head + cur
        wk = [jnp.broadcast_to(wdw[k:k + 1, cols], (sub, lanes)) for k in range(K_B)]
        bias = jnp.broadcast_to(bdw[:, cols], (sub, lanes))
        for v in range(nv):
            acc = bias
            for k in range(K_B):
                acc = acc + wk[k] * ext[v + k]
            stage_out_ref[lt, pl.ds(v, sub, stride=pitch), :] = acc
        for q in range(sub):
            cout_ref[s * rows + q * nv:s * rows + (q + 1) * nv, cols] = (
                stage_out_ref[lt, q * pitch:q * pitch + nv, :])


def _even_kernel(*refs, n_streams, rows, sgu_n, emit_v, transposed_conv):
    (x_ref, hist_ref, g_ref, win_ref, lnvg_ref, lnvb_ref, ws_ref, bs_ref, wdw_ref, bdw_ref,
     lncg_ref, lncb_ref, wout_ref) = refs[:13]
    n_out = 3 if emit_v else 2
    y_ref = refs[13]
    av_ref = refs[14] if emit_v else None
    nb_ref = refs[13 + n_out - 1]
    if transposed_conv:
        stage_in_ref, stage_out_ref, prev_ref, cout_ref = refs[13 + n_out:]
    else:
        conv_ref, cout_ref = refs[13 + n_out:]
    d_model = x_ref.shape[-1]
    d_b = wdw_ref.shape[-1]
    d_a = d_b
    dg = d_a // G_A
    hist_rows = K_B - 1
    off = CONV_PAD - hist_rows

    if not transposed_conv:
        @pl.when(pl.program_id(1) == 0)
        def _():
            conv_ref[:, off:CONV_PAD, :] = hist_ref[...]

    lnvg = lnvg_ref[...]
    lnvb = lnvb_ref[...]
    tril = (lax.broadcasted_iota(jnp.int32, (sgu_n, sgu_n), 0)
            >= lax.broadcasted_iota(jnp.int32, (sgu_n, sgu_n), 1))
    ws = [jnp.where(tril, ws_ref[g], 0.0).astype(BF16) for g in range(G_A)]
    bs = bs_ref[...]
    wdw = wdw_ref[...]
    bdw = bdw_ref[...]

    m = n_streams * rows
    x = x_ref[...].reshape(m, d_model)
    xn = _rms(x, g_ref[...]).astype(BF16)
    z = _dot(xn, win_ref[...])

    za = _gelu_tanh(z[:, :2 * d_a])
    u = za[:, :d_a]
    v = jnp.concatenate(
        [_layer_norm(za[:, d_a + g * dg:d_a + (g + 1) * dg], lnvg[:, g * dg:(g + 1) * dg],
                     lnvb[:, g * dg:(g + 1) * dg]) for g in range(G_A)], axis=1)
    if emit_v:
        av_ref[...] = v.reshape(av_ref.shape)
    vb = v.astype(BF16)
    sg_rows = []
    for c in range(m // sgu_n):
        vc = vb[c * sgu_n:(c + 1) * sgu_n]
        sg_rows.append(jnp.concatenate(
            [_dot(ws[g], vc[:, g * dg:(g + 1) * dg]) for g in range(G_A)], axis=1) + bs)
    sg = sg_rows[0] if len(sg_rows) == 1 else jnp.concatenate(sg_rows, axis=0)
    y_a = u * sg

    glu = z[:, 2 * d_a:2 * d_a + d_b] * jax.nn.sigmoid(z[:, 2 * d_a + d_b:])
    for s in range(n_streams):
        if transposed_conv:
            _conv_rows_transposed(glu, s, hist_ref, wdw, bdw, stage_in_ref, stage_out_ref, prev_ref,
                                  cout_ref, rows=rows, first=pl.program_id(1) == 0)
            nb_ref[s] = glu[(s + 1) * rows - hist_rows:(s + 1) * rows]
            continue
        conv_ref[s, CONV_PAD:CONV_PAD + rows, :] = glu[s * rows:(s + 1) * rows]
        for rb in range(rows // CONV_ROW_BLOCK):
            r0 = rb * CONV_ROW_BLOCK
            acc = jnp.zeros((CONV_ROW_BLOCK, d_b), F32) + bdw
            for k in range(K_B):
                acc = acc + wdw[k:k + 1] * conv_ref[s, r0 + off + k:r0 + off + k + CONV_ROW_BLOCK, :]
            cout_ref[s * rows + r0:s * rows + r0 + CONV_ROW_BLOCK, :] = acc
        nb_ref[s] = conv_ref[s, rows + off:rows + CONV_PAD, :]
        conv_ref[s, 0:CONV_PAD, :] = conv_ref[s, rows:rows + CONV_PAD, :]
    y_b = jax.nn.silu(_layer_norm(cout_ref[...], lncg_ref[...], lncb_ref[...]))

    y = _dot(jnp.concatenate([y_a, y_b], axis=1).astype(BF16), wout_ref[...])
    y_ref[...] = (x + y).reshape(y_ref.shape)


def _even_layer(x, conv_hist, g_norm, w_in, ln_v_g, ln_v_b, w_s, b_s, w_dw, b_dw, ln_c_g, ln_c_b,
                w_out, *, n_streams, rows, emit_v):
    b, t, d = x.shape
    d_b = w_dw.shape[-1]
    d_a = ln_v_g.shape[-1]
    dg = d_a // G_A
    sgu_n = min(t, SGU_CHUNK)
    assert b % n_streams == 0 and t % rows == 0 and rows % sgu_n == 0 and (n_streams * rows) % sgu_n == 0
    assert rows % CONV_ROW_BLOCK == 0 and rows >= CONV_PAD and d_a == d_b
    bs_full = jnp.repeat(b_s[:, :sgu_n].T, dg, axis=1)
    args = [x, conv_hist, g_norm.reshape(1, d), w_in.astype(BF16), ln_v_g.reshape(1, d_a),
            ln_v_b.reshape(1, d_a), w_s[:, :sgu_n, :sgu_n], bs_full, w_dw, b_dw.reshape(1, d_b),
            ln_c_g.reshape(1, d_b), ln_c_b.reshape(1, d_b), w_out.astype(BF16)]
    in_specs = [
        pl.BlockSpec((n_streams, rows, d), lambda i, j: (i, j, 0)),
        pl.BlockSpec((n_streams, K_B - 1, d_b), lambda i, j: (i, 0, 0)),
    ] + [_const_spec(a.shape) for a in args[2:]]
    out_specs = [pl.BlockSpec((n_streams, rows, d), lambda i, j: (i, j, 0))]
    out_shape = [jax.ShapeDtypeStruct((b, t, d), F32)]
    if emit_v:
        out_specs.append(pl.BlockSpec((n_streams, rows, d_a), lambda i, j: (i, j, 0)))
        out_shape.append(jax.ShapeDtypeStruct((b, t, d_a), F32))
    out_specs.append(pl.BlockSpec((n_streams, K_B - 1, d_b), lambda i, j: (i, 0, 0)))
    out_shape.append(jax.ShapeDtypeStruct((b, K_B - 1, d_b), F32))
    transposed_conv = rows // V7X_SUBLANES >= K_B - 1
    kern = functools.partial(_even_kernel, n_streams=n_streams, rows=rows, sgu_n=sgu_n, emit_v=emit_v,
                             transposed_conv=transposed_conv)
    if transposed_conv:
        stage_shape = (d_b // V7X_LANES, V7X_SUBLANES * (rows // V7X_SUBLANES + CONV_PITCH_PAD), V7X_LANES)
        conv_scratch = [pltpu.VMEM(stage_shape, F32), pltpu.VMEM(stage_shape, F32),
                        pltpu.VMEM((n_streams, d_b // V7X_LANES, K_B - 1, V7X_SUBLANES, V7X_LANES), F32)]
    else:
        conv_scratch = [pltpu.VMEM((n_streams, CONV_PAD + rows, d_b), F32)]
    return pl.pallas_call(
        kern,
        grid=(b // n_streams, t // rows),
        in_specs=in_specs,
        out_specs=out_specs,
        out_shape=out_shape,
        scratch_shapes=conv_scratch + [pltpu.VMEM((n_streams * rows, d_b), F32)],
        compiler_params=pltpu.CompilerParams(
            dimension_semantics=("arbitrary", "arbitrary"),
            vmem_limit_bytes=V7X_VMEM_LIMIT_BYTES),
        name="even_layer",
    )(*args)


POOL_PAD = 16
ATTN_HALF_ROWS = 256
SEARCH_GROUP = 16
SEARCH_ROWS = 16
INT_MIN = -2 ** 31
NEG_INF_KEY = -2 ** 31 + 0x7FFFFF
LOG2E = math.log2(math.e)
M_INIT = -3.0e38


def _t5_bucket(rel):
    nb = N_BUCKETS // 2
    exact = nb // 2
    side = jnp.where(rel > 0, nb, 0)
    n = jnp.abs(rel)
    large = exact + (jnp.log(jnp.maximum(n, 1).astype(jnp.float32) / exact)
                     / math.log(MAX_DIST / exact) * (nb - exact)).astype(jnp.int32)
    large = jnp.minimum(large, nb - 1)
    return side + jnp.where(n < exact, n, large)


def _sortable(score):
    bits = pltpu.bitcast(score, jnp.int32)
    return bits ^ ((bits >> 31) & 0x7FFFFFFF)


def _for_range(lo, hi, body):
    if isinstance(lo, int) and isinstance(hi, int):
        for j in range(lo, hi):
            body(j)
    else:
        def step(j, carry):
            body(j)
            return carry
        lax.fori_loop(lo, hi, step, 0)


def _odd_kernel(*refs, rows, qb, hist_len, n_valid_hist, top, has_hist, single_step):
    refs = list(refs)
    x_ref = refs.pop(0)
    if has_hist:
        kht_ref, vht_ref, kiht_ref = refs[:3]
        kkt_ref, kk2t_ref, vvt_ref = refs[-3:]
        refs = refs[3:-3]
    (poolh_ref, g_ref, win_ref, wpool_ref, spool_ref, wout_ref, bucket_ref, rb_ref, tri_ref,
     y_ref, k_ref, v_ref, ki_ref, np_ref,
     kk_ref, kk2_ref, vv_ref, keys_ref, planes_ref, acc_ref, m_ref, off_ref, thr_ref, need_ref,
     bias_ref, pool_ref, yc_ref, yd_ref, qe_ref, qo_ref, ie_ref, io_ref, wb_ref) = refs
    d_c = H_C * HD_C
    d_qi = H_I * D_I
    d_d = wpool_ref.shape[0]
    n_pairs = H_C // 2
    n_ipairs = H_I // 2
    kt = KEY_TILE
    lanes = V7X_LANES
    t = pl.program_id(1)
    t0 = 0 if single_step else t * rows
    hist_tiles = hist_len // kt
    col_xd = d_c + d_qi

    def dot_keys_t(lhs, nat_ref, t_ref, j):
        if has_hist and j < hist_tiles:
            return _dot(lhs, t_ref[j])
        return _dot_nt(lhs, nat_ref[j])

    def dot_values(p, j):
        if has_hist and j < hist_tiles:
            return _dot_nt(p, vvt_ref[j])
        return _dot(p, vv_ref[j])

    @pl.when(jnp.logical_and(pl.program_id(0) == 0, t == 0))
    def _():
        bucket = bucket_ref[...]
        for h in range(H_C):
            b_acc = jnp.zeros(bucket.shape, F32)
            for b in range(N_BUCKETS):
                b_acc = jnp.where(bucket == b, rb_ref[b, h] * LOG2E, b_acc)
            bias_ref[0:2, h] = b_acc
            bias_ref[2, h] = jnp.zeros(bucket.shape[1:], F32)

    @pl.when(t == 0)
    def _():
        pool_ref[POOL_PAD - POOL_HIST:POOL_PAD, :] = poolh_ref[0]
        if has_hist:
            ones_half = jnp.ones((lanes - HD_C, kt), F32)
            for j in range(hist_tiles):
                kj = kht_ref[0, :, j * kt:(j + 1) * kt]
                kij = kiht_ref[0, :, j * kt:(j + 1) * kt]
                vj = vht_ref[0, :, j * kt:(j + 1) * kt]
                kkt_ref[j] = jnp.concatenate([kj, kij], axis=0).astype(BF16)
                kk2t_ref[j] = jnp.concatenate([kij, kj], axis=0).astype(BF16)
                vvt_ref[j] = jnp.concatenate([vj, ones_half], axis=0).astype(BF16)

    x = x_ref[0]
    xn = _rms(x, g_ref[...]).astype(BF16)
    z_tail = _dot(xn, win_ref[:, col_xd:])
    z = _dot(xn, win_ref[:, :col_xd])

    xd = z_tail[:, :d_d]
    pool_ref[POOL_PAD:POOL_PAD + rows, :] = xd
    np_ref[0] = pool_ref[rows + POOL_PAD - POOL_HIST:rows + POOL_PAD, :]
    run = xd
    wins = {}
    for dshift in range(1, POOL_WINDOWS[-1]):
        run = run + pool_ref[POOL_PAD - dshift:POOL_PAD - dshift + rows, :]
        if dshift + 1 in POOL_WINDOWS:
            wins[dshift + 1] = run
    pool_ref[0:POOL_PAD, :] = pool_ref[rows:rows + POOL_PAD, :]
    dg_d = d_d // len(POOL_WINDOWS)
    lane_d = lax.broadcasted_iota(jnp.int32, (rows, d_d), 1)
    tpos = n_valid_hist + t0 + 1 + lax.broadcasted_iota(jnp.int32, (rows, d_d), 0)
    win_sum = wins[POOL_WINDOWS[-1]]
    width = jnp.full((rows, d_d), POOL_WINDOWS[-1], jnp.int32)
    for gi in range(len(POOL_WINDOWS) - 2, -1, -1):
        in_g = lane_d < (gi + 1) * dg_d
        win_sum = jnp.where(in_g, wins[POOL_WINDOWS[gi]], win_sum)
        width = jnp.where(in_g, POOL_WINDOWS[gi], width)
    count = jnp.minimum(tpos, width).astype(F32)
    m_pool = win_sum / count - xd
    yd_ref[...] = _dot(m_pool.astype(BF16), wpool_ref[...]) * spool_ref[...]

    g1 = z_tail[:, d_d:d_d + lanes]
    g2 = z_tail[:, d_d + lanes:]
    k_ref[0] = g1[:, :HD_C]
    v_ref[0] = g1[:, HD_C:]
    ki_ref[0] = g2[:, :D_I]
    lane = lax.broadcasted_iota(jnp.int32, (rows, lanes), 1)
    low = lane < HD_C
    g1r = pltpu.roll(g1, HD_C, 1)
    g2r = pltpu.roll(g2, D_I, 1)
    kk_new = jnp.where(low, g1, g2r).astype(BF16)
    kk2_new = jnp.where(low, g2, g1r).astype(BF16)
    vv_new = jnp.where(low, g1r, 1.0).astype(BF16)
    if rows % kt == 0:
        base_tile = (hist_len + t0) // kt
        for i in range(rows // kt):
            kk_ref[base_tile + i] = kk_new[i * kt:(i + 1) * kt]
            kk2_ref[base_tile + i] = kk2_new[i * kt:(i + 1) * kt]
            vv_ref[base_tile + i] = vv_new[i * kt:(i + 1) * kt]
    else:
        zpad = jnp.zeros((kt - rows, lanes), BF16)
        kk_ref[hist_tiles] = jnp.concatenate([kk_new, zpad], axis=0)
        kk2_ref[hist_tiles] = jnp.concatenate([kk2_new, zpad], axis=0)
        vv_ref[hist_tiles] = jnp.concatenate([vv_new, zpad], axis=0)
    kv_len = hist_len + t0 + rows

    w_idx = g2[:, D_I:D_I + H_I] * ((H_I ** -0.5) * (D_I ** -0.5))
    tri = tri_ref[...]
    ones_rhs = jnp.ones((kt, lanes), BF16)
    hb = min(rows, ATTN_HALF_ROWS)
    n_half = rows // hb
    n_new = max(rows // kt, 1)
    low_h = lax.broadcasted_iota(jnp.int32, (hb, lanes), 1) < HD_C

    jb = (hist_len + t0) // kt
    n_tiles = jb + n_new

    for hf in range(n_half):
        zr = z[hf * hb:(hf + 1) * hb]
        for g in range(n_pairs):
            grp = zr[:, g * lanes:(g + 1) * lanes] * (HD_C ** -0.5 * LOG2E)
            qe_ref[(hf * n_pairs + g) * hb:(hf * n_pairs + g + 1) * hb] = (
                jnp.where(low_h, grp, 0.0).astype(BF16))
            qo_ref[(hf * n_pairs + g) * hb:(hf * n_pairs + g + 1) * hb] = (
                jnp.where(low_h, 0.0, grp).astype(BF16))
    for g in range(n_ipairs):
        grp = z[:, d_c + g * lanes:d_c + (g + 1) * lanes]
        ie_ref[g * rows:(g + 1) * rows] = jnp.where(low, grp, 0.0).astype(BF16)
        io_ref[g * rows:(g + 1) * rows] = jnp.where(low, 0.0, grp).astype(BF16)
    for h in range(H_I):
        wb_ref[h] = jnp.broadcast_to(w_idx[:, h:h + 1], (rows, lanes))
    qchunk = (hist_len + t0 + lax.broadcasted_iota(jnp.int32, (rows, 1), 0)) >> CHUNK_SHIFT
    kpos_rel = lax.broadcasted_iota(jnp.int32, (rows, kt), 1)

    def score_body(j):
        se = jnp.maximum(dot_keys_t(ie_ref[...], kk2_ref, kk2t_ref if has_hist else None, j), 0.0)
        so = jnp.maximum(dot_keys_t(io_ref[...], kk_ref, kkt_ref if has_hist else None, j), 0.0)
        score = jnp.zeros((rows, kt), F32)
        for g in range(n_ipairs):
            we = wb_ref[2 * g]
            wo = wb_ref[2 * g + 1]
            score = score + se[g * rows:(g + 1) * rows] * jnp.concatenate([we] * (kt // lanes), axis=1)
            score = score + so[g * rows:(g + 1) * rows] * jnp.concatenate([wo] * (kt // lanes), axis=1)
        kpos = kpos_rel + j * kt
        adm = jnp.logical_and((kpos >> CHUNK_SHIFT) <= qchunk, kpos < kv_len)
        keys_ref[j] = _sortable(jnp.where(adm, score, NEG_INF))
    _for_range(0, n_tiles, score_body)

    def fill_body(j):
        keys_ref[j] = jnp.full((rows, kt), INT_MIN, jnp.int32)
    _for_range(n_tiles, keys_ref.shape[0], fill_body)

    n_groups = keys_ref.shape[0] * (kt // lanes) // SEARCH_GROUP

    def plane_body(rc, carry):
        r = pl.multiple_of(rc * SEARCH_ROWS, SEARCH_ROWS)
        for gi in range(n_groups):
            a = []
            for i in range(SEARCH_GROUP):
                lt = gi * SEARCH_GROUP + i
                a.append(keys_ref[lt // (kt // lanes), pl.ds(r, SEARCH_ROWS),
                                  (lt % (kt // lanes)) * lanes:(lt % (kt // lanes) + 1) * lanes])
            for sh, msk in ((8, 0x00FF00FF), (4, 0x0F0F0F0F), (2, 0x33333333), (1, 0x55555555)):
                for k in range(SEARCH_GROUP):
                    if k & sh == 0:
                        tmp = (a[k] ^ lax.shift_right_logical(a[k + sh], sh)) & msk
                        a[k] = a[k] ^ tmp
                        a[k + sh] = a[k + sh] ^ (tmp << sh)
            for w in range(SEARCH_GROUP):
                planes_ref[gi, w, pl.ds(r, SEARCH_ROWS), :] = a[w]
        return carry
    lax.fori_loop(0, rows // SEARCH_ROWS, plane_body, 0)

    alive = [jnp.full((rows, lanes), -65536, jnp.int32) for _ in range(n_groups)]
    above = jnp.zeros((rows, 1), F32)
    thr_u = jnp.zeros((rows, 1), jnp.int32)
    for b in range(31, -1, -1):
        w = (31 - b) if b >= 16 else (15 - b)
        if b == 15:
            alive = [lax.shift_right_logical(a, 16) for a in alive]
        ones = []
        for gi in range(n_groups):
            plane = planes_ref[gi, w]
            if b == 31:
                plane = ~plane
            ones.append(alive[gi] & plane)
        pc = lax.population_count(ones[0])
        for gi in range(1, n_groups):
            pc = pc + lax.population_count(ones[gi])
        cnt = jnp.sum(pc.astype(F32), axis=1, keepdims=True)
        take = (above + cnt) >= float(top)
        alive = [jnp.where(take, o, a ^ o) for a, o in zip(alive, ones)]
        above = jnp.where(take, above, above + cnt)
        thr_u = jnp.where(take, thr_u | jnp.int32(INT_MIN if b == 31 else (1 << b)), thr_u)
    thr_ref[...] = jnp.broadcast_to(thr_u ^ jnp.int32(INT_MIN), (rows, lanes))
    need_ref[...] = jnp.broadcast_to(float(top) - above, (rows, lanes))
    off_ref[...] = jnp.zeros((rows, lanes), F32)
    m_ref[...] = jnp.full(m_ref.shape, M_INIT, F32)
    acc_ref[...] = jnp.zeros(acc_ref.shape, F32)

    def attn_half(j, hf, dj):
        rs = slice(hf * hb, (hf + 1) * hb)
        kj = keys_ref[j, rs]
        thr_t = jnp.concatenate([thr_ref[rs]] * (kt // lanes), axis=1)
        need_t = jnp.concatenate([need_ref[rs]] * (kt // lanes), axis=1)
        eq = jnp.where(kj == thr_t, 1.0, 0.0)
        eq_b = eq.astype(BF16)
        rank = _dot(eq_b, tri) + jnp.concatenate([off_ref[rs]] * (kt // lanes), axis=1)
        self_ = jnp.where(kj > thr_t, 1.0, jnp.where(rank <= need_t, eq, 0.0))
        sel = jnp.where(kj > NEG_INF_KEY, self_, 0.0) > 0.5
        mask_add = jnp.where(sel, 0.0, NEG_INF)
        off_ref[rs] = off_ref[rs] + _dot(eq_b, ones_rhs)
        q_rows = slice(hf * n_pairs * hb, (hf + 1) * n_pairs * hb)
        lg = (dot_keys_t(qe_ref[q_rows], kk_ref, kkt_ref if has_hist else None, j),
              dot_keys_t(qo_ref[q_rows], kk2_ref, kk2t_ref if has_hist else None, j))
        ps = []
        alphas = []
        for eo in range(2):
            for g in range(n_pairs):
                h = 2 * g + eo
                l = lg[eo][g * hb:(g + 1) * hb]
                if dj is not None:
                    row_parts = []
                    for sub in range(hb // qb):
                        sblk = ((hf * (hb // qb) + sub) * qb) // lanes
                        parts = []
                        for c in range(kt // lanes):
                            part = l[sub * qb:(sub + 1) * qb, c * lanes:(c + 1) * lanes]
                            d = (kt // lanes) * dj + c - sblk
                            if isinstance(d, int):
                                if d in (-1, 0):
                                    part = part + bias_ref[d + 1, h]
                            else:
                                entry = jnp.where(d == -1, 0, jnp.where(d == 0, 1, 2))
                                part = part + bias_ref[entry, h]
                            parts.append(part)
                        row_parts.append(jnp.concatenate(parts, axis=1))
                    l = row_parts[0] if len(row_parts) == 1 else jnp.concatenate(row_parts, axis=0)
                l = l + mask_add
                st = slice(((hf * 2 + eo) * n_pairs + g) * hb, ((hf * 2 + eo) * n_pairs + g + 1) * hb)
                m_old = m_ref[st]
                m_new = jnp.maximum(m_old, jnp.max(l, axis=1, keepdims=True))
                alphas.append(jnp.exp2(m_old - m_new))
                ps.append(jnp.exp2(l - jnp.concatenate([m_new] * (kt // lanes), axis=1)).astype(BF16))
                m_ref[st] = m_new
        a_rows = slice(hf * H_C * hb, (hf + 1) * H_C * hb)
        acc_ref[a_rows] = (acc_ref[a_rows] * jnp.concatenate(alphas, axis=0)
                           + dot_values(jnp.concatenate(ps, axis=0), j))

    def far_body(j):
        for hf in range(n_half):
            attn_half(j, hf, None)
    _for_range(0, jb - 1, far_body)

    def near_body(j):
        for hf in range(n_half):
            attn_half(j, hf, j - jb)
    _for_range(max(jb - 1, 0) if single_step else jnp.maximum(jb - 1, 0), jb + 1, near_body)

    for hf in range(n_half):
        for dj in range(1, n_new):
            if ((hf + 1) * hb - qb) // lanes >= (kt // lanes) * dj:
                attn_half(jb + dj, hf, dj)
        for g in range(n_pairs):
            oe = acc_ref[((hf * 2) * n_pairs + g) * hb:((hf * 2) * n_pairs + g + 1) * hb]
            oo = acc_ref[((hf * 2 + 1) * n_pairs + g) * hb:((hf * 2 + 1) * n_pairs + g + 1) * hb]
            num = jnp.where(low_h, oe, pltpu.roll(oo, HD_C, 1))
            den = jnp.where(low_h, pltpu.roll(oe, HD_C, 1), oo)
            yc_ref[hf * hb:(hf + 1) * hb, g * lanes:(g + 1) * lanes] = num / den
        rs = slice(hf * hb, (hf + 1) * hb)
        y_cat = jnp.concatenate([yc_ref[rs], yd_ref[rs]], axis=1).astype(BF16)
        y_ref[0, rs, :] = x[rs] + _dot(y_cat, wout_ref[...])


def _odd_layer(x, k_hist, v_hist, ki_hist, pool_hist, n_valid_hist, g_norm, w_in, w_pool, s_pool,
               w_out, rel_bias, *, rows):
    b, t, d = x.shape
    hist_len = k_hist.shape[1]
    has_hist = hist_len > 0
    d_c = H_C * HD_C
    d_qi = H_I * D_I
    d_d = d - d_c
    qb = min(2 * CHUNK, rows)
    kt = KEY_TILE
    lanes = V7X_LANES
    s_total = hist_len + t
    top = min(TOPK_MAX, s_total // 4)
    assert t % rows == 0 and rows % qb == 0 and hist_len % kt == 0
    assert rows % kt == 0 or (rows == t and rows == qb and rows <= CHUNK)
    n_tiles = (s_total + kt - 1) // kt
    tiles_per_group = SEARCH_GROUP * lanes // kt
    n_groups = (n_tiles + tiles_per_group - 1) // tiles_per_group
    n_tiles_pad = n_groups * tiles_per_group
    assert rows % SEARCH_ROWS == 0 and rows % min(rows, ATTN_HALF_ROWS) == 0

    offs = np.cumsum([0, d_c, HD_C, HD_C, d_qi, D_I, H_I]).tolist()
    q_w, k_w, v_w, qi_w, ki_w, wi_w = (w_in[:, offs[i]:offs[i + 1]] for i in range(6))
    xd_w = w_in[:, offs[6]:]
    pad_w = jnp.zeros((d, lanes - D_I - H_I), w_in.dtype)
    w_all = jnp.concatenate([q_w, qi_w, xd_w, k_w, v_w, ki_w, wi_w, pad_w], axis=1).astype(BF16)
    wpool_bd = jax.scipy.linalg.block_diag(*[w_pool[g] for g in range(w_pool.shape[0])]).astype(BF16)

    rel = (lanes * jnp.arange(-1, 1, dtype=jnp.int32)[:, None, None]
           + jnp.arange(lanes, dtype=jnp.int32)[None, None, :]
           - jnp.arange(qb, dtype=jnp.int32)[None, :, None])
    bucket = _t5_bucket(rel)
    far_bucket = _t5_bucket(jnp.int32(-2 * lanes))
    rb_shift = rel_bias - rel_bias[far_bucket][None, :]
    tri = (jnp.arange(kt)[:, None] <= jnp.arange(kt)[None, :]).astype(BF16)

    args = [x]
    in_specs = [pl.BlockSpec((1, rows, d), lambda i, j: (i, j, 0))]
    if has_hist:
        assert rows == t, "history tiles are addressed statically"
        args += [jnp.swapaxes(k_hist, 1, 2), jnp.swapaxes(v_hist, 1, 2), jnp.swapaxes(ki_hist, 1, 2)]
        in_specs += [pl.BlockSpec((1, HD_C, hist_len), lambda i, j: (i, 0, 0))] * 2
        in_specs += [pl.BlockSpec((1, D_I, hist_len), lambda i, j: (i, 0, 0))]
    consts = [g_norm.reshape(1, d), w_all, wpool_bd, s_pool.reshape(1, d_d), w_out.astype(BF16), bucket]
    args += [pool_hist] + consts + [rb_shift, tri]
    in_specs += ([pl.BlockSpec((1, POOL_HIST, d_d), lambda i, j: (i, 0, 0))]
                 + [_const_spec(a.shape) for a in consts]
                 + [pl.BlockSpec(memory_space=pltpu.SMEM), _const_spec(tri.shape)])
    kern = functools.partial(_odd_kernel, rows=rows, qb=qb, hist_len=hist_len,
                             n_valid_hist=n_valid_hist, top=top, has_hist=has_hist,
                             single_step=(rows == t))
    return pl.pallas_call(
        kern,
        grid=(b, t // rows),
        in_specs=in_specs,
        out_specs=[
            pl.BlockSpec((1, rows, d), lambda i, j: (i, j, 0)),
            pl.BlockSpec((1, rows, HD_C), lambda i, j: (i, j, 0)),
            pl.BlockSpec((1, rows, HD_C), lambda i, j: (i, j, 0)),
            pl.BlockSpec((1, rows, D_I), lambda i, j: (i, j, 0)),
            pl.BlockSpec((1, POOL_HIST, d_d), lambda i, j: (i, 0, 0)),
        ],
        out_shape=[jax.ShapeDtypeStruct((b, t, d), F32),
                   jax.ShapeDtypeStruct((b, t, HD_C), F32),
                   jax.ShapeDtypeStruct((b, t, HD_C), F32),
                   jax.ShapeDtypeStruct((b, t, D_I), F32),
                   jax.ShapeDtypeStruct((b, POOL_HIST, d_d), F32)],
        scratch_shapes=[
            pltpu.VMEM((n_tiles, kt, lanes), BF16),
            pltpu.VMEM((n_tiles, kt, lanes), BF16),
            pltpu.VMEM((n_tiles, kt, lanes), BF16),
            pltpu.VMEM((n_tiles_pad, rows, kt), jnp.int32),
            pltpu.VMEM((n_groups, SEARCH_GROUP, rows, lanes), jnp.int32),
            pltpu.VMEM((H_C * rows, lanes), F32),
            pltpu.VMEM((H_C * rows, lanes), F32),
            pltpu.VMEM((rows, lanes), F32),
            pltpu.VMEM((rows, lanes), jnp.int32),
            pltpu.VMEM((rows, lanes), F32),
            pltpu.VMEM((3, H_C, qb, lanes), F32),
            pltpu.VMEM((POOL_PAD + rows, d_d), F32),
            pltpu.VMEM((rows, d_c), F32),
            pltpu.VMEM((rows, d_d), F32),
            pltpu.VMEM((H_C // 2 * rows, lanes), BF16),
            pltpu.VMEM((H_C // 2 * rows, lanes), BF16),
            pltpu.VMEM((H_I // 2 * rows, lanes), BF16),
            pltpu.VMEM((H_I // 2 * rows, lanes), BF16),
            pltpu.VMEM((H_I, rows, lanes), F32),
        ] + ([pltpu.VMEM((hist_len // kt, lanes, kt), BF16)] * 3 if has_hist else []),
        compiler_params=pltpu.CompilerParams(
            dimension_semantics=("arbitrary", "arbitrary"),
            vmem_limit_bytes=V7X_VMEM_LIMIT_BYTES),
        name="odd_layer",
    )(*args)


def kernel(x_prompt, x_sample, cache_b_conv, cache_c_k, cache_c_v, cache_c_kidx, cache_d_pool, cache_ffn_conv, ln_mix, ln_ffn, ln_final, e_w_in, e_ln_v_g, e_ln_v_b, e_w_s, e_b_s, e_w_dw, e_b_dw, e_ln_c_g, e_ln_c_b, e_w_out, o_w_in, o_w_pool, o_s_pool, o_w_out, rel_bias, f_w_up, f_w_dw, f_b_dw, f_w_down):
    hp, hs = x_prompt, x_sample
    bp, bs = x_prompt.shape[0], x_sample.shape[0]
    ts = x_sample.shape[1]
    depth = ln_mix.shape[0]
    d_ff = f_w_down.shape[1]
    a_s_l, b_p_l, b_s_l = [], [], []
    ck_p_l, cv_p_l, cki_p_l, ck_s_l, cv_s_l, cki_s_l, d_p_l, d_s_l = [], [], [], [], [], [], [], []
    f_p_l, f_s_l = [], []
    f_w_up_b = f_w_up.astype(BF16)
    f_w_down_b = f_w_down.astype(BF16)
    for layer in range(depth):
        i = layer // 2
        if layer % 2 == 0:
            ew = (ln_mix[layer], e_w_in[i], e_ln_v_g[i], e_ln_v_b[i], e_w_s[i], e_b_s[i], e_w_dw[i],
                  e_b_dw[i], e_ln_c_g[i], e_ln_c_b[i], e_w_out[i])
            hp, b_p = _even_layer(hp, jnp.zeros((bp, K_B - 1, e_w_dw.shape[-1]), F32), *ew,
                                  n_streams=1, rows=PROMPT_ROWS, emit_v=False)
            hs, a_s, b_s = _even_layer(hs, cache_b_conv[i], *ew, n_streams=bs, rows=ts, emit_v=True)
            a_s_l.append(a_s); b_p_l.append(b_p); b_s_l.append(b_s)
        else:
            ow = (ln_mix[layer], o_w_in[i], o_w_pool[i], o_s_pool[i], o_w_out[i], rel_bias)
            d_d = o_w_pool.shape[1] * o_w_pool.shape[2]
            hp, k_p, v_p, ki_p, d_p = _odd_layer(
                hp, jnp.zeros((bp, 0, HD_C), F32), jnp.zeros((bp, 0, HD_C), F32),
                jnp.zeros((bp, 0, D_I), F32), jnp.zeros((bp, POOL_HIST, d_d), F32), 0, *ow,
                rows=PROMPT_ROWS)
            hs, k_s, v_s, ki_s, d_s = _odd_layer(
                hs, cache_c_k[i], cache_c_v[i], cache_c_kidx[i], cache_d_pool[i], POOL_HIST, *ow,
                rows=ts)
            ck_p_l.append(k_p); cv_p_l.append(v_p); cki_p_l.append(ki_p)
            ck_s_l.append(k_s); cv_s_l.append(v_s); cki_s_l.append(ki_s)
            d_p_l.append(d_p); d_s_l.append(d_s)
        g_final = ln_final if layer == depth - 1 else None
        fw = (ln_ffn[layer], f_w_up_b, f_w_dw[layer], f_b_dw[layer], f_w_down_b, g_final, layer)
        hp, f_p = _conv_ffn(hp, jnp.zeros((bp, K_FFN - 1, d_ff), F32), *fw,
                            n_streams=1, rows=PROMPT_ROWS)
        hs, f_s = _conv_ffn(hs, cache_ffn_conv[layer], *fw, n_streams=bs, rows=ts)
        f_p_l.append(f_p); f_s_l.append(f_s)
    return (hp, hs,
            jnp.stack(a_s_l), jnp.stack(b_p_l), jnp.stack(b_s_l),
            jnp.stack(ck_p_l), jnp.stack(cv_p_l), jnp.stack(cki_p_l),
            jnp.stack(ck_s_l), jnp.stack(cv_s_l), jnp.stack(cki_s_l),
            jnp.stack(d_p_l), jnp.stack(d_s_l),
            jnp.stack(f_p_l), jnp.stack(f_s_l))
```

```python
import functools
import math

import jax
import jax.numpy as jnp
import numpy as np
from jax import lax
from jax.experimental import pallas as pl
from jax.experimental.pallas import tpu as pltpu

F32 = jnp.float32
BF16 = jnp.bfloat16

EPS = 1e-6
CHUNK = 64
CHUNK_SHIFT = CHUNK.bit_length() - 1
SGU_CHUNK = 128
G_A = 4
K_B = 31
H_C = 12
HD_C = 64
H_I = 8
D_I = 64
TOPK_MAX = 256
N_BUCKETS = 32
MAX_DIST = 128
POOL_WINDOWS = (2, 4, 8, 16)
POOL_HIST = 15
K_FFN = 3

V7X_LANES = 128
V7X_SUBLANES = 8
V7X_MXU_DIM = 256
V7X_VMEM_LIMIT_BYTES = 56 * 1024 * 1024

KEY_TILE = V7X_MXU_DIM
PROMPT_ROWS = 512
NEG_INF = float("-inf")


def _rms(x, g):
    return x * lax.rsqrt(jnp.mean(x * x, axis=-1, keepdims=True) + EPS) * g


def _layer_norm(x, g, b):
    mu = jnp.mean(x, axis=-1, keepdims=True)
    xc = x - mu
    return xc * lax.rsqrt(jnp.mean(xc * xc, axis=-1, keepdims=True) + EPS) * g + b


def _gelu_tanh(x):
    cdf = 0.5 * (1.0 + jnp.tanh(math.sqrt(2.0 / math.pi) * (x + 0.044715 * (x * x * x))))
    return x * cdf


def _dot(a, b):
    return jnp.dot(a, b, preferred_element_type=F32)


def _dot_nt(a, b):
    return lax.dot_general(a, b, (((1,), (1,)), ((), ())), preferred_element_type=F32)


def _const_spec(shape):
    nd = len(shape)
    return pl.BlockSpec(shape, lambda *_: (0,) * nd, pipeline_mode=pl.Buffered(1))


def _ffn_kernel(*refs, n_streams, rows, final_norm):
    if final_norm:
        (x_ref, hist_ref, g_ref, wu_ref, wd_ref, dw_ref, bdw_ref, gf_ref,
         y_ref, nh_ref, carry_ref) = refs
    else:
        (x_ref, hist_ref, g_ref, wu_ref, wd_ref, dw_ref, bdw_ref,
         y_ref, nh_ref, carry_ref) = refs
        gf_ref = None
    d_model = x_ref.shape[-1]
    d_ff = wd_ref.shape[0]
    m = n_streams * rows
    sub = V7X_SUBLANES
    n_hist = K_FFN - 1

    @pl.when(pl.program_id(1) == 0)
    def _():
        carry_ref[:, sub - n_hist:sub, :] = hist_ref[...]

    x = x_ref[...].reshape(m, d_model)
    xn = _rms(x, g_ref[...]).astype(BF16)
    a = _dot(xn, wu_ref[:, :d_ff])
    val = _dot(xn, wu_ref[:, d_ff:])
    w = dw_ref[...]
    ys = []
    for s in range(n_streams):
        a_s = a[s * rows:(s + 1) * rows]
        ext = jnp.concatenate([carry_ref[s], a_s], axis=0)
        y_s = w[n_hist:K_FFN] * a_s
        for k in range(n_hist):
            y_s = y_s + w[k:k + 1] * ext[sub - n_hist + k:sub - n_hist + k + rows]
        ys.append(y_s)
        carry_ref[s] = a_s[rows - sub:rows]
        nh_ref[s] = a_s[rows - n_hist:rows]
    y = (ys[0] if n_streams == 1 else jnp.concatenate(ys, axis=0)) + bdw_ref[...]
    out = x + _dot((_gelu_tanh(y) * val).astype(BF16), wd_ref[...])
    if final_norm:
        out = _rms(out, gf_ref[...])
    y_ref[...] = out.reshape(y_ref.shape)


def _layer_spec(shape, layer):
    nd = len(shape)
    return pl.BlockSpec((None,) + tuple(shape[1:]), lambda *_: (layer,) + (0,) * (nd - 1),
                        pipeline_mode=pl.Buffered(1))


def _conv_ffn(x, hist, g_norm, w_up_all, w_dw, b_dw, w_down_all, g_final, layer, *, n_streams, rows):
    b, t, d = x.shape
    d_ff = w_down_all.shape[1]
    assert d_ff % V7X_LANES == 0
    assert b % n_streams == 0 and t % rows == 0 and rows % V7X_SUBLANES == 0
    final_norm = g_final is not None
    args = [x, hist, g_norm.reshape(1, d), w_up_all, w_down_all, w_dw, b_dw.reshape(1, d_ff)]
    in_specs = [
        pl.BlockSpec((n_streams, rows, d), lambda i, j: (i, j, 0)),
        pl.BlockSpec((n_streams, K_FFN - 1, d_ff), lambda i, j: (i, 0, 0)),
        _const_spec((1, d)),
        _layer_spec(w_up_all.shape, layer), _layer_spec(w_down_all.shape, layer),
        _const_spec(w_dw.shape), _const_spec((1, d_ff)),
    ]
    if final_norm:
        args.append(g_final.reshape(1, d))
        in_specs.append(_const_spec((1, d)))
    kern = functools.partial(_ffn_kernel, n_streams=n_streams, rows=rows, final_norm=final_norm)
    return pl.pallas_call(
        kern,
        grid=(b // n_streams, t // rows),
        in_specs=in_specs,
        out_specs=[
            pl.BlockSpec((n_streams, rows, d), lambda i, j: (i, j, 0)),
            pl.BlockSpec((n_streams, K_FFN - 1, d_ff), lambda i, j: (i, 0, 0)),
        ],
        out_shape=[jax.ShapeDtypeStruct((b, t, d), F32),
                   jax.ShapeDtypeStruct((b, K_FFN - 1, d_ff), F32)],
        scratch_shapes=[pltpu.VMEM((n_streams, V7X_SUBLANES, d_ff), F32)],
        compiler_params=pltpu.CompilerParams(
            dimension_semantics=("arbitrary", "arbitrary"),
            vmem_limit_bytes=V7X_VMEM_LIMIT_BYTES),
        name="conv_ffn",
    )(*args)


CONV_PAD = 32
CONV_ROW_BLOCK = 32
CONV_PITCH_PAD = 4


def _conv_rows_transposed(glu, s, hist_ref, wdw, bdw, stage_in_ref, stage_out_ref, prev_ref, cout_ref,
                          *, rows, first):
    sub = V7X_SUBLANES
    lanes = V7X_LANES
    nv = rows // sub
    pitch = nv + CONV_PITCH_PAD
    hist_rows = K_B - 1
    d_b = glu.shape[-1]
    sub_id = lax.broadcasted_iota(jnp.int32, (sub, lanes), 0)
    for lt in range(d_b // lanes):
        cols = slice(lt * lanes, (lt + 1) * lanes)

        @pl.when(first)
        def _():
            for e in range(hist_rows):
                prev_ref[s, lt, e, sub - 1:sub, :] = hist_ref[s, e:e + 1, cols]
        for q in range(sub):
            stage_in_ref[lt, q * pitch:q * pitch + nv, :] = (
                glu[s * rows + q * nv:s * rows + (q + 1) * nv, cols])
        cur = [stage_in_ref[lt, pl.ds(v, sub, stride=pitch), :] for v in range(nv)]
        head = []
        for e in range(hist_rows):
            merged = jnp.where(sub_id == sub - 1, prev_ref[s, lt, e], cur[nv - hist_rows + e])
            head.append(pltpu.roll(merged, 1, 0))
        for e in range(hist_rows):
            prev_ref[s, lt, e] = cur[nv - hist_rows + e]
        ext = head + cur
        wk = [jnp.broadcast_to(wdw[k:k + 1, cols], (sub, lanes)) for k in range(K_B)]
        bias = jnp.broadcast_to(bdw[:, cols], (sub, lanes))
        for v in range(nv):
            acc = bias
            for k in range(K_B):
                acc = acc + wk[k] * ext[v + k]
            stage_out_ref[lt, pl.ds(v, sub, stride=pitch), :] = acc
        for q in range(sub):
            cout_ref[s * rows + q * nv:s * rows + (q + 1) * nv, cols] = (
                stage_out_ref[lt, q * pitch:q * pitch + nv, :])


def _even_kernel(*refs, n_streams, rows, sgu_n, emit_v, transposed_conv):
    (x_ref, hist_ref, g_ref, win_ref, lnvg_ref, lnvb_ref, ws_ref, bs_ref, wdw_ref, bdw_ref,
     lncg_ref, lncb_ref, wout_ref) = refs[:13]
    n_out = 3 if emit_v else 2
    y_ref = refs[13]
    av_ref = refs[14] if emit_v else None
    nb_ref = refs[13 + n_out - 1]
    if transposed_conv:
        stage_in_ref, stage_out_ref, prev_ref, cout_ref = refs[13 + n_out:]
    else:
        conv_ref, cout_ref = refs[13 + n_out:]
    d_model = x_ref.shape[-1]
    d_b = wdw_ref.shape[-1]
    d_a = d_b
    dg = d_a // G_A
    hist_rows = K_B - 1
    off = CONV_PAD - hist_rows

    if not transposed_conv:
        @pl.when(pl.program_id(1) == 0)
        def _():
            conv_ref[:, off:CONV_PAD, :] = hist_ref[...]

    lnvg = lnvg_ref[...]
    lnvb = lnvb_ref[...]
    tril = (lax.broadcasted_iota(jnp.int32, (sgu_n, sgu_n), 0)
            >= lax.broadcasted_iota(jnp.int32, (sgu_n, sgu_n), 1))
    ws = [jnp.where(tril, ws_ref[g], 0.0).astype(BF16) for g in range(G_A)]
    bs = bs_ref[...]
    wdw = wdw_ref[...]
    bdw = bdw_ref[...]

    m = n_streams * rows
    x = x_ref[...].reshape(m, d_model)
    xn = _rms(x, g_ref[...]).astype(BF16)
    z = _dot(xn, win_ref[...])

    za = _gelu_tanh(z[:, :2 * d_a])
    u = za[:, :d_a]
    v = jnp.concatenate(
        [_layer_norm(za[:, d_a + g * dg:d_a + (g + 1) * dg], lnvg[:, g * dg:(g + 1) * dg],
                     lnvb[:, g * dg:(g + 1) * dg]) for g in range(G_A)], axis=1)
    if emit_v:
        av_ref[...] = v.reshape(av_ref.shape)
    vb = v.astype(BF16)
    sg_rows = []
    for c in range(m // sgu_n):
        vc = vb[c * sgu_n:(c + 1) * sgu_n]
        sg_rows.append(jnp.concatenate(
            [_dot(ws[g], vc[:, g * dg:(g + 1) * dg]) for g in range(G_A)], axis=1) + bs)
    sg = sg_rows[0] if len(sg_rows) == 1 else jnp.concatenate(sg_rows, axis=0)
    y_a = u * sg

    glu = z[:, 2 * d_a:2 * d_a + d_b] * jax.nn.sigmoid(z[:, 2 * d_a + d_b:])
    for s in range(n_streams):
        if transposed_conv:
            _conv_rows_transposed(glu, s, hist_ref, wdw, bdw, stage_in_ref, stage_out_ref, prev_ref,
                                  cout_ref, rows=rows, first=pl.program_id(1) == 0)
            nb_ref[s] = glu[(s + 1) * rows - hist_rows:(s + 1) * rows]
            continue
        conv_ref[s, CONV_PAD:CONV_PAD + rows, :] = glu[s * rows:(s + 1) * rows]
        for rb in range(rows // CONV_ROW_BLOCK):
            r0 = rb * CONV_ROW_BLOCK
            acc = jnp.zeros((CONV_ROW_BLOCK, d_b), F32) + bdw
            for k in range(K_B):
                acc = acc + wdw[k:k + 1] * conv_ref[s, r0 + off + k:r0 + off + k + CONV_ROW_BLOCK, :]
            cout_ref[s * rows + r0:s * rows + r0 + CONV_ROW_BLOCK, :] = acc
        nb_ref[s] = conv_ref[s, rows + off:rows + CONV_PAD, :]
        conv_ref[s, 0:CONV_PAD, :] = conv_ref[s, rows:rows + CONV_PAD, :]
    y_b = jax.nn.silu(_layer_norm(cout_ref[...], lncg_ref[...], lncb_ref[...]))

    y = _dot(jnp.concatenate([y_a, y_b], axis=1).astype(BF16), wout_ref[...])
    y_ref[...] = (x + y).reshape(y_ref.shape)


def _even_layer(x, conv_hist, g_norm, w_in, ln_v_g, ln_v_b, w_s, b_s, w_dw, b_dw, ln_c_g, ln_c_b,
                w_out, *, n_streams, rows, emit_v):
    b, t, d = x.shape
    d_b = w_dw.shape[-1]
    d_a = ln_v_g.shape[-1]
    dg = d_a // G_A
    sgu_n = min(t, SGU_CHUNK)
    assert b % n_streams == 0 and t % rows == 0 and rows % sgu_n == 0 and (n_streams * rows) % sgu_n == 0
    assert rows % CONV_ROW_BLOCK == 0 and rows >= CONV_PAD and d_a == d_b
    bs_full = jnp.repeat(b_s[:, :sgu_n].T, dg, axis=1)
    args = [x, conv_hist, g_norm.reshape(1, d), w_in.astype(BF16), ln_v_g.reshape(1, d_a),
            ln_v_b.reshape(1, d_a), w_s[:, :sgu_n, :sgu_n], bs_full, w_dw, b_dw.reshape(1, d_b),
            ln_c_g.reshape(1, d_b), ln_c_b.reshape(1, d_b), w_out.astype(BF16)]
    in_specs = [
        pl.BlockSpec((n_streams, rows, d), lambda i, j: (i, j, 0)),
        pl.BlockSpec((n_streams, K_B - 1, d_b), lambda i, j: (i, 0, 0)),
    ] + [_const_spec(a.shape) for a in args[2:]]
    out_specs = [pl.BlockSpec((n_streams, rows, d), lambda i, j: (i, j, 0))]
    out_shape = [jax.ShapeDtypeStruct((b, t, d), F32)]
    if emit_v:
        out_specs.append(pl.BlockSpec((n_streams, rows, d_a), lambda i, j: (i, j, 0)))
        out_shape.append(jax.ShapeDtypeStruct((b, t, d_a), F32))
    out_specs.append(pl.BlockSpec((n_streams, K_B - 1, d_b), lambda i, j: (i, 0, 0)))
    out_shape.append(jax.ShapeDtypeStruct((b, K_B - 1, d_b), F32))
    transposed_conv = rows // V7X_SUBLANES >= K_B - 1
    kern = functools.partial(_even_kernel, n_streams=n_streams, rows=rows, sgu_n=sgu_n, emit_v=emit_v,
                             transposed_conv=transposed_conv)
    if transposed_conv:
        stage_shape = (d_b // V7X_LANES, V7X_SUBLANES * (rows // V7X_SUBLANES + CONV_PITCH_PAD), V7X_LANES)
        conv_scratch = [pltpu.VMEM(stage_shape, F32), pltpu.VMEM(stage_shape, F32),
                        pltpu.VMEM((n_streams, d_b // V7X_LANES, K_B - 1, V7X_SUBLANES, V7X_LANES), F32)]
    else:
        conv_scratch = [pltpu.VMEM((n_streams, CONV_PAD + rows, d_b), F32)]
    return pl.pallas_call(
        kern,
        grid=(b // n_streams, t // rows),
        in_specs=in_specs,
        out_specs=out_specs,
        out_shape=out_shape,
        scratch_shapes=conv_scratch + [pltpu.VMEM((n_streams * rows, d_b), F32)],
        compiler_params=pltpu.CompilerParams(
            dimension_semantics=("arbitrary", "arbitrary"),
            vmem_limit_bytes=V7X_VMEM_LIMIT_BYTES),
        name="even_layer",
    )(*args)


POOL_PAD = 16
ATTN_HALF_ROWS = 256
SEARCH_GROUP = 16
SEARCH_ROWS = 16
INT_MIN = -2 ** 31
NEG_INF_KEY = -2 ** 31 + 0x7FFFFF
LOG2E = math.log2(math.e)
M_INIT = -3.0e38


def _t5_bucket(rel):
    nb = N_BUCKETS // 2
    exact = nb // 2
    side = jnp.where(rel > 0, nb, 0)
    n = jnp.abs(rel)
    large = exact + (jnp.log(jnp.maximum(n, 1).astype(jnp.float32) / exact)
                     / math.log(MAX_DIST / exact) * (nb - exact)).astype(jnp.int32)
    large = jnp.minimum(large, nb - 1)
    return side + jnp.where(n < exact, n, large)


def _sortable(score):
    bits = pltpu.bitcast(score, jnp.int32)
    return bits ^ ((bits >> 31) & 0x7FFFFFFF)


def _for_range(lo, hi, body):
    if isinstance(lo, int) and isinstance(hi, int):
        for j in range(lo, hi):
            body(j)
    else:
        def step(j, carry):
            body(j)
            return carry
        lax.fori_loop(lo, hi, step, 0)


def _odd_kernel(*refs, rows, qb, hist_len, n_valid_hist, top, has_hist, single_step):
    refs = list(refs)
    x_ref = refs.pop(0)
    if has_hist:
        kht_ref, vht_ref, kiht_ref = refs[:3]
        kkt_ref, kk2t_ref, vvt_ref = refs[-3:]
        refs = refs[3:-3]
    (poolh_ref, g_ref, win_ref, wpool_ref, spool_ref, wout_ref, bucket_ref, rb_ref, tri_ref,
     y_ref, k_ref, v_ref, ki_ref, np_ref,
     kk_ref, kk2_ref, vv_ref, keys_ref, planes_ref, acc_ref, m_ref, off_ref, thr_ref, need_ref,
     bias_ref, pool_ref, yc_ref, yd_ref, qe_ref, qo_ref, ie_ref, io_ref, wb_ref) = refs
    d_c = H_C * HD_C
    d_qi = H_I * D_I
    d_d = wpool_ref.shape[0]
    n_pairs = H_C // 2
    n_ipairs = H_I // 2
    kt = KEY_TILE
    lanes = V7X_LANES
    t = pl.program_id(1)
    t0 = 0 if single_step else t * rows
    hist_tiles = hist_len // kt
    col_xd = d_c + d_qi

    def dot_keys_t(lhs, nat_ref, t_ref, j):
        if has_hist and j < hist_tiles:
            return _dot(lhs, t_ref[j])
        return _dot_nt(lhs, nat_ref[j])

    def dot_values(p, j):
        if has_hist and j < hist_tiles:
            return _dot_nt(p, vvt_ref[j])
        return _dot(p, vv_ref[j])

    @pl.when(jnp.logical_and(pl.program_id(0) == 0, t == 0))
    def _():
        bucket = bucket_ref[...]
        for h in range(H_C):
            b_acc = jnp.zeros(bucket.shape, F32)
            for b in range(N_BUCKETS):
                b_acc = jnp.where(bucket == b, rb_ref[b, h] * LOG2E, b_acc)
            bias_ref[0:2, h] = b_acc
            bias_ref[2, h] = jnp.zeros(bucket.shape[1:], F32)

    @pl.when(t == 0)
    def _():
        pool_ref[POOL_PAD - POOL_HIST:POOL_PAD, :] = poolh_ref[0]
        if has_hist:
            ones_half = jnp.ones((lanes - HD_C, kt), F32)
            for j in range(hist_tiles):
                kj = kht_ref[0, :, j * kt:(j + 1) * kt]
                kij = kiht_ref[0, :, j * kt:(j + 1) * kt]
                vj = vht_ref[0, :, j * kt:(j + 1) * kt]
                kkt_ref[j] = jnp.concatenate([kj, kij], axis=0).astype(BF16)
                kk2t_ref[j] = jnp.concatenate([kij, kj], axis=0).astype(BF16)
                vvt_ref[j] = jnp.concatenate([vj, ones_half], axis=0).astype(BF16)

    x = x_ref[0]
    xn = _rms(x, g_ref[...]).astype(BF16)
    z_tail = _dot(xn, win_ref[:, col_xd:])
    z = _dot(xn, win_ref[:, :col_xd])

    xd = z_tail[:, :d_d]
    pool_ref[POOL_PAD:POOL_PAD + rows, :] = xd
    np_ref[0] = pool_ref[rows + POOL_PAD - POOL_HIST:rows + POOL_PAD, :]
    run = xd
    wins = {}
    for dshift in range(1, POOL_WINDOWS[-1]):
        run = run + pool_ref[POOL_PAD - dshift:POOL_PAD - dshift + rows, :]
        if dshift + 1 in POOL_WINDOWS:
            wins[dshift + 1] = run
    pool_ref[0:POOL_PAD, :] = pool_ref[rows:rows + POOL_PAD, :]
    dg_d = d_d // len(POOL_WINDOWS)
    lane_d = lax.broadcasted_iota(jnp.int32, (rows, d_d), 1)
    tpos = n_valid_hist + t0 + 1 + lax.broadcasted_iota(jnp.int32, (rows, d_d), 0)
    win_sum = wins[POOL_WINDOWS[-1]]
    width = jnp.full((rows, d_d), POOL_WINDOWS[-1], jnp.int32)
    for gi in range(len(POOL_WINDOWS) - 2, -1, -1):
        in_g = lane_d < (gi + 1) * dg_d
        win_sum = jnp.where(in_g, wins[POOL_WINDOWS[gi]], win_sum)
        width = jnp.where(in_g, POOL_WINDOWS[gi], width)
    count = jnp.minimum(tpos, width).astype(F32)
    m_pool = win_sum / count - xd
    yd_ref[...] = _dot(m_pool.astype(BF16), wpool_ref[...]) * spool_ref[...]

    g1 = z_tail[:, d_d:d_d + lanes]
    g2 = z_tail[:, d_d + lanes:]
    k_ref[0] = g1[:, :HD_C]
    v_ref[0] = g1[:, HD_C:]
    ki_ref[0] = g2[:, :D_I]
    lane = lax.broadcasted_iota(jnp.int32, (rows, lanes), 1)
    low = lane < HD_C
    g1r = pltpu.roll(g1, HD_C, 1)
    g2r = pltpu.roll(g2, D_I, 1)
    kk_new = jnp.where(low, g1, g2r).astype(BF16)
    kk2_new = jnp.where(low, g2, g1r).astype(BF16)
    vv_new = jnp.where(low, g1r, 1.0).astype(BF16)
    if rows % kt == 0:
        base_tile = (hist_len + t0) // kt
        for i in range(rows // kt):
            kk_ref[base_tile + i] = kk_new[i * kt:(i + 1) * kt]
            kk2_ref[base_tile + i] = kk2_new[i * kt:(i + 1) * kt]
            vv_ref[base_tile + i] = vv_new[i * kt:(i + 1) * kt]
    else:
        zpad = jnp.zeros((kt - rows, lanes), BF16)
        kk_ref[hist_tiles] = jnp.concatenate([kk_new, zpad], axis=0)
        kk2_ref[hist_tiles] = jnp.concatenate([kk2_new, zpad], axis=0)
        vv_ref[hist_tiles] = jnp.concatenate([vv_new, zpad], axis=0)
    kv_len = hist_len + t0 + rows

    w_idx = g2[:, D_I:D_I + H_I] * ((H_I ** -0.5) * (D_I ** -0.5))
    tri = tri_ref[...]
    ones_rhs = jnp.ones((kt, lanes), BF16)
    hb = min(rows, ATTN_HALF_ROWS)
    n_half = rows // hb
    n_new = max(rows // kt, 1)
    low_h = lax.broadcasted_iota(jnp.int32, (hb, lanes), 1) < HD_C

    jb = (hist_len + t0) // kt
    n_tiles = jb + n_new

    for hf in range(n_half):
        zr = z[hf * hb:(hf + 1) * hb]
        for g in range(n_pairs):
            grp = zr[:, g * lanes:(g + 1) * lanes] * (HD_C ** -0.5 * LOG2E)
            qe_ref[(hf * n_pairs + g) * hb:(hf * n_pairs + g + 1) * hb] = (
                jnp.where(low_h, grp, 0.0).astype(BF16))
            qo_ref[(hf * n_pairs + g) * hb:(hf * n_pairs + g + 1) * hb] = (
                jnp.where(low_h, 0.0, grp).astype(BF16))
    for g in range(n_ipairs):
        grp = z[:, d_c + g * lanes:d_c + (g + 1) * lanes]
        ie_ref[g * rows:(g + 1) * rows] = jnp.where(low, grp, 0.0).astype(BF16)
        io_ref[g * rows:(g + 1) * rows] = jnp.where(low, 0.0, grp).astype(BF16)
    for h in range(H_I):
        wb_ref[h] = jnp.broadcast_to(w_idx[:, h:h + 1], (rows, lanes))

    def score_body(j, r0=0):
        nr = rows - r0
        ie = jnp.concatenate([ie_ref[g * rows + r0:(g + 1) * rows] for g in range(n_ipairs)], axis=0)
        io = jnp.concatenate([io_ref[g * rows + r0:(g + 1) * rows] for g in range(n_ipairs)], axis=0)
        se = jnp.maximum(dot_keys_t(ie, kk2_ref, kk2t_ref if has_hist else None, j), 0.0)
        so = jnp.maximum(dot_keys_t(io, kk_ref, kkt_ref if has_hist else None, j), 0.0)
        score = jnp.zeros((nr, kt), F32)
        for g in range(n_ipairs):
            we = wb_ref[2 * g, r0:rows]
            wo = wb_ref[2 * g + 1, r0:rows]
            score = score + se[g * nr:(g + 1) * nr] * jnp.concatenate([we] * (kt // lanes), axis=1)
            score = score + so[g * nr:(g + 1) * nr] * jnp.concatenate([wo] * (kt // lanes), axis=1)
        qchunk = (hist_len + t0 + r0 + lax.broadcasted_iota(jnp.int32, (nr, 1), 0)) >> CHUNK_SHIFT
        kpos = lax.broadcasted_iota(jnp.int32, (nr, kt), 1) + j * kt
        adm = jnp.logical_and((kpos >> CHUNK_SHIFT) <= qchunk, kpos < kv_len)
        keys_ref[j, r0:rows] = _sortable(jnp.where(adm, score, NEG_INF))
        if r0:
            keys_ref[j, 0:r0] = jnp.full((r0, kt), NEG_INF_KEY, jnp.int32)
    _for_range(0, jb + 1, score_body)
    for dj in range(1, n_new):
        first_half = min(hf for hf in range(n_half)
                         if ((hf + 1) * hb - qb) // lanes >= (kt // lanes) * dj)
        score_body(jb + dj, first_half * hb)

    def fill_body(j):
        keys_ref[j] = jnp.full((rows, kt), INT_MIN, jnp.int32)
    _for_range(n_tiles, keys_ref.shape[0], fill_body)

    n_groups = keys_ref.shape[0] * (kt // lanes) // SEARCH_GROUP

    def plane_body(rc, carry):
        r = pl.multiple_of(rc * SEARCH_ROWS, SEARCH_ROWS)
        for gi in range(n_groups):
            a = []
            for i in range(SEARCH_GROUP):
                lt = gi * SEARCH_GROUP + i
                a.append(keys_ref[lt // (kt // lanes), pl.ds(r, SEARCH_ROWS),
                                  (lt % (kt // lanes)) * lanes:(lt % (kt // lanes) + 1) * lanes])
            for sh, msk in ((8, 0x00FF00FF), (4, 0x0F0F0F0F), (2, 0x33333333), (1, 0x55555555)):
                for k in range(SEARCH_GROUP):
                    if k & sh == 0:
                        tmp = (a[k] ^ lax.shift_right_logical(a[k + sh], sh)) & msk
                        a[k] = a[k] ^ tmp
                        a[k + sh] = a[k + sh] ^ (tmp << sh)
            for w in range(SEARCH_GROUP):
                planes_ref[gi, w, pl.ds(r, SEARCH_ROWS), :] = a[w]
        return carry

    def plane_body_half(rc, carry):
        half = SEARCH_GROUP // 2
        r = pl.multiple_of(rc * SEARCH_ROWS, SEARCH_ROWS)
        a = [keys_ref[lt // (kt // lanes), pl.ds(r, SEARCH_ROWS),
                      (lt % (kt // lanes)) * lanes:(lt % (kt // lanes) + 1) * lanes] for lt in range(half)]
        for sh, msk in ((4, 0x0F0F0F0F), (2, 0x33333333), (1, 0x55555555)):
            for k in range(half):
                if k & sh == 0:
                    tmp = (a[k] ^ lax.shift_right_logical(a[k + sh], sh)) & msk
                    a[k] = a[k] ^ tmp
                    a[k + sh] = a[k + sh] ^ (tmp << sh)
        top_bytes = jnp.int32(-16711936)
        for w in range(half):
            word = a[w] & top_bytes
            if w == 0:
                word = word | jnp.int32(0x00FF0000)
            planes_ref[0, w, pl.ds(r, SEARCH_ROWS), :] = word
            planes_ref[0, w + half, pl.ds(r, SEARCH_ROWS), :] = (a[w] << 8) & top_bytes
        return carry

    if n_groups == 1 and not single_step:
        few_keys = n_tiles * (kt // lanes) <= SEARCH_GROUP // 2

        @pl.when(few_keys)
        def _():
            lax.fori_loop(0, rows // SEARCH_ROWS, plane_body_half, 0)

        @pl.when(jnp.logical_not(few_keys))
        def _():
            lax.fori_loop(0, rows // SEARCH_ROWS, plane_body, 0)
    else:
        lax.fori_loop(0, rows // SEARCH_ROWS, plane_body, 0)

    alive = [jnp.full((rows, lanes), -65536, jnp.int32) for _ in range(n_groups)]
    above = jnp.zeros((rows, 1), F32)
    thr_u = jnp.zeros((rows, 1), jnp.int32)
    for b in range(31, -1, -1):
        w = (31 - b) if b >= 16 else (15 - b)
        if b == 15:
            alive = [lax.shift_right_logical(a, 16) for a in alive]
        ones = []
        for gi in range(n_groups):
            plane = planes_ref[gi, w]
            if b == 31:
                plane = ~plane
            ones.append(alive[gi] & plane)
        pc = lax.population_count(ones[0])
        for gi in range(1, n_groups):
            pc = pc + lax.population_count(ones[gi])
        cnt = jnp.sum(pc.astype(F32), axis=1, keepdims=True)
        take = (above + cnt) >= float(top)
        alive = [jnp.where(take, o, a ^ o) for a, o in zip(alive, ones)]
        above = jnp.where(take, above, above + cnt)
        thr_u = jnp.where(take, thr_u | jnp.int32(INT_MIN if b == 31 else (1 << b)), thr_u)
    thr_ref[...] = jnp.broadcast_to(thr_u ^ jnp.int32(INT_MIN), (rows, lanes))
    need_ref[...] = jnp.broadcast_to(float(top) - above, (rows, lanes))
    off_ref[...] = jnp.zeros((rows, lanes), F32)
    m_ref[...] = jnp.full(m_ref.shape, M_INIT, F32)
    acc_ref[...] = jnp.zeros(acc_ref.shape, F32)

    def attn_half(j, hf, dj):
        rs = slice(hf * hb, (hf + 1) * hb)
        kj = keys_ref[j, rs]
        thr_t = jnp.concatenate([thr_ref[rs]] * (kt // lanes), axis=1)
        need_t = jnp.concatenate([need_ref[rs]] * (kt // lanes), axis=1)
        eq = jnp.where(kj == thr_t, 1.0, 0.0)
        eq_b = eq.astype(BF16)
        rank = _dot(eq_b, tri) + jnp.concatenate([off_ref[rs]] * (kt // lanes), axis=1)
        self_ = jnp.where(kj > thr_t, 1.0, jnp.where(rank <= need_t, eq, 0.0))
        sel = jnp.where(kj > NEG_INF_KEY, self_, 0.0) > 0.5
        mask_add = jnp.where(sel, 0.0, NEG_INF)
        off_ref[rs] = off_ref[rs] + _dot(eq_b, ones_rhs)
        q_rows = slice(hf * n_pairs * hb, (hf + 1) * n_pairs * hb)
        lg = (dot_keys_t(qe_ref[q_rows], kk_ref, kkt_ref if has_hist else None, j),
              dot_keys_t(qo_ref[q_rows], kk2_ref, kk2t_ref if has_hist else None, j))
        ps = []
        alphas = []
        for eo in range(2):
            for g in range(n_pairs):
                h = 2 * g + eo
                l = lg[eo][g * hb:(g + 1) * hb]
                if dj is not None:
                    row_parts = []
                    for sub in range(hb // qb):
                        sblk = ((hf * (hb // qb) + sub) * qb) // lanes
                        parts = []
                        for c in range(kt // lanes):
                            part = l[sub * qb:(sub + 1) * qb, c * lanes:(c + 1) * lanes]
                            d = (kt // lanes) * dj + c - sblk
                            if isinstance(d, int):
                                if d in (-1, 0):
                                    part = part + bias_ref[d + 1, h]
                            else:
                                entry = jnp.where(d == -1, 0, jnp.where(d == 0, 1, 2))
                                part = part + bias_ref[entry, h]
                            parts.append(part)
                        row_parts.append(jnp.concatenate(parts, axis=1))
                    l = row_parts[0] if len(row_parts) == 1 else jnp.concatenate(row_parts, axis=0)
                l = l + mask_add
                st = slice(((hf * 2 + eo) * n_pairs + g) * hb, ((hf * 2 + eo) * n_pairs + g + 1) * hb)
                m_old = m_ref[st]
                m_new = jnp.maximum(m_old, jnp.max(l, axis=1, keepdims=True))
                alphas.append(jnp.exp2(m_old - m_new))
                ps.append(jnp.exp2(l - jnp.concatenate([m_new] * (kt // lanes), axis=1)).astype(BF16))
                m_ref[st] = m_new
        a_rows = slice(hf * H_C * hb, (hf + 1) * H_C * hb)
        acc_ref[a_rows] = (acc_ref[a_rows] * jnp.concatenate(alphas, axis=0)
                           + dot_values(jnp.concatenate(ps, axis=0), j))

    def far_body(j):
        for hf in range(n_half):
            attn_half(j, hf, None)
    _for_range(0, jb - 1, far_body)

    def near_body(j):
        for hf in range(n_half):
            attn_half(j, hf, j - jb)
    _for_range(max(jb - 1, 0) if single_step else jnp.maximum(jb - 1, 0), jb + 1, near_body)

    for hf in range(n_half):
        for dj in range(1, n_new):
            if ((hf + 1) * hb - qb) // lanes >= (kt // lanes) * dj:
                attn_half(jb + dj, hf, dj)
        for g in range(n_pairs):
            oe = acc_ref[((hf * 2) * n_pairs + g) * hb:((hf * 2) * n_pairs + g + 1) * hb]
            oo = acc_ref[((hf * 2 + 1) * n_pairs + g) * hb:((hf * 2 + 1) * n_pairs + g + 1) * hb]
            num = jnp.where(low_h, oe, pltpu.roll(oo, HD_C, 1))
            den = jnp.where(low_h, pltpu.roll(oe, HD_C, 1), oo)
            yc_ref[hf * hb:(hf + 1) * hb, g * lanes:(g + 1) * lanes] = num / den
        rs = slice(hf * hb, (hf + 1) * hb)
        y_cat = jnp.concatenate([yc_ref[rs], yd_ref[rs]], axis=1).astype(BF16)
        y_ref[0, rs, :] = x[rs] + _dot(y_cat, wout_ref[...])


def _odd_layer(x, k_hist, v_hist, ki_hist, pool_hist, n_valid_hist, g_norm, w_in, w_pool, s_pool,
               w_out, rel_bias, *, rows):
    b, t, d = x.shape
    hist_len = k_hist.shape[1]
    has_hist = hist_len > 0
    d_c = H_C * HD_C
    d_qi = H_I * D_I
    d_d = d - d_c
    qb = min(2 * CHUNK, rows)
    kt = KEY_TILE
    lanes = V7X_LANES
    s_total = hist_len + t
    top = min(TOPK_MAX, s_total // 4)
    assert t % rows == 0 and rows % qb == 0 and hist_len % kt == 0
    assert rows % kt == 0 or (rows == t and rows == qb and rows <= CHUNK)
    n_tiles = (s_total + kt - 1) // kt
    tiles_per_group = SEARCH_GROUP * lanes // kt
    n_groups = (n_tiles + tiles_per_group - 1) // tiles_per_group
    n_tiles_pad = n_groups * tiles_per_group
    assert rows % SEARCH_ROWS == 0 and rows % min(rows, ATTN_HALF_ROWS) == 0

    offs = np.cumsum([0, d_c, HD_C, HD_C, d_qi, D_I, H_I]).tolist()
    q_w, k_w, v_w, qi_w, ki_w, wi_w = (w_in[:, offs[i]:offs[i + 1]] for i in range(6))
    xd_w = w_in[:, offs[6]:]
    pad_w = jnp.zeros((d, lanes - D_I - H_I), w_in.dtype)
    w_all = jnp.concatenate([q_w, qi_w, xd_w, k_w, v_w, ki_w, wi_w, pad_w], axis=1).astype(BF16)
    wpool_bd = jax.scipy.linalg.block_diag(*[w_pool[g] for g in range(w_pool.shape[0])]).astype(BF16)

    rel = (lanes * jnp.arange(-1, 1, dtype=jnp.int32)[:, None, None]
           + jnp.arange(lanes, dtype=jnp.int32)[None, None, :]
           - jnp.arange(qb, dtype=jnp.int32)[None, :, None])
    bucket = _t5_bucket(rel)
    far_bucket = _t5_bucket(jnp.int32(-2 * lanes))
    rb_shift = rel_bias - rel_bias[far_bucket][None, :]
    tri = (jnp.arange(kt)[:, None] <= jnp.arange(kt)[None, :]).astype(BF16)

    args = [x]
    in_specs = [pl.BlockSpec((1, rows, d), lambda i, j: (i, j, 0))]
    if has_hist:
        assert rows == t, "history tiles are addressed statically"
        args += [jnp.swapaxes(k_hist, 1, 2), jnp.swapaxes(v_hist, 1, 2), jnp.swapaxes(ki_hist, 1, 2)]
        in_specs += [pl.BlockSpec((1, HD_C, hist_len), lambda i, j: (i, 0, 0))] * 2
        in_specs += [pl.BlockSpec((1, D_I, hist_len), lambda i, j: (i, 0, 0))]
    consts = [g_norm.reshape(1, d), w_all, wpool_bd, s_pool.reshape(1, d_d), w_out.astype(BF16), bucket]
    args += [pool_hist] + consts + [rb_shift, tri]
    in_specs += ([pl.BlockSpec((1, POOL_HIST, d_d), lambda i, j: (i, 0, 0))]
                 + [_const_spec(a.shape) for a in consts]
                 + [pl.BlockSpec(memory_space=pltpu.SMEM), _const_spec(tri.shape)])
    kern = functools.partial(_odd_kernel, rows=rows, qb=qb, hist_len=hist_len,
                             n_valid_hist=n_valid_hist, top=top, has_hist=has_hist,
                             single_step=(rows == t))
    return pl.pallas_call(
        kern,
        grid=(b, t // rows),
        in_specs=in_specs,
        out_specs=[
            pl.BlockSpec((1, rows, d), lambda i, j: (i, j, 0)),
            pl.BlockSpec((1, rows, HD_C), lambda i, j: (i, j, 0)),
            pl.BlockSpec((1, rows, HD_C), lambda i, j: (i, j, 0)),
            pl.BlockSpec((1, rows, D_I), lambda i, j: (i, j, 0)),
            pl.BlockSpec((1, POOL_HIST, d_d), lambda i, j: (i, 0, 0)),
        ],
        out_shape=[jax.ShapeDtypeStruct((b, t, d), F32),
                   jax.ShapeDtypeStruct((b, t, HD_C), F32),
                   jax.ShapeDtypeStruct((b, t, HD_C), F32),
                   jax.ShapeDtypeStruct((b, t, D_I), F32),
                   jax.ShapeDtypeStruct((b, POOL_HIST, d_d), F32)],
        scratch_shapes=[
            pltpu.VMEM((n_tiles, kt, lanes), BF16),
            pltpu.VMEM((n_tiles, kt, lanes), BF16),
            pltpu.VMEM((n_tiles, kt, lanes), BF16),
            pltpu.VMEM((n_tiles_pad, rows, kt), jnp.int32),
            pltpu.VMEM((n_groups, SEARCH_GROUP, rows, lanes), jnp.int32),
            pltpu.VMEM((H_C * rows, lanes), F32),
            pltpu.VMEM((H_C * rows, lanes), F32),
            pltpu.VMEM((rows, lanes), F32),
            pltpu.VMEM((rows, lanes), jnp.int32),
            pltpu.VMEM((rows, lanes), F32),
            pltpu.VMEM((3, H_C, qb, lanes), F32),
            pltpu.VMEM((POOL_PAD + rows, d_d), F32),
            pltpu.VMEM((rows, d_c), F32),
            pltpu.VMEM((rows, d_d), F32),
            pltpu.VMEM((H_C // 2 * rows, lanes), BF16),
            pltpu.VMEM((H_C // 2 * rows, lanes), BF16),
            pltpu.VMEM((H_I // 2 * rows, lanes), BF16),
            pltpu.VMEM((H_I // 2 * rows, lanes), BF16),
            pltpu.VMEM((H_I, rows, lanes), F32),
        ] + ([pltpu.VMEM((hist_len // kt, lanes, kt), BF16)] * 3 if has_hist else []),
        compiler_params=pltpu.CompilerParams(
            dimension_semantics=("arbitrary", "arbitrary"),
            vmem_limit_bytes=V7X_VMEM_LIMIT_BYTES),
        name="odd_layer",
    )(*args)


def kernel(x_prompt, x_sample, cache_b_conv, cache_c_k, cache_c_v, cache_c_kidx, cache_d_pool, cache_ffn_conv, ln_mix, ln_ffn, ln_final, e_w_in, e_ln_v_g, e_ln_v_b, e_w_s, e_b_s, e_w_dw, e_b_dw, e_ln_c_g, e_ln_c_b, e_w_out, o_w_in, o_w_pool, o_s_pool, o_w_out, rel_bias, f_w_up, f_w_dw, f_b_dw, f_w_down):
    hp, hs = x_prompt, x_sample
    bp, bs = x_prompt.shape[0], x_sample.shape[0]
    ts = x_sample.shape[1]
    depth = ln_mix.shape[0]
    d_ff = f_w_down.shape[1]
    a_s_l, b_p_l, b_s_l = [], [], []
    ck_p_l, cv_p_l, cki_p_l, ck_s_l, cv_s_l, cki_s_l, d_p_l, d_s_l = [], [], [], [], [], [], [], []
    f_p_l, f_s_l = [], []
    f_w_up_b = f_w_up.astype(BF16)
    f_w_down_b = f_w_down.astype(BF16)
    for layer in range(depth):
        i = layer // 2
        if layer % 2 == 0:
            ew = (ln_mix[layer], e_w_in[i], e_ln_v_g[i], e_ln_v_b[i], e_w_s[i], e_b_s[i], e_w_dw[i],
                  e_b_dw[i], e_ln_c_g[i], e_ln_c_b[i], e_w_out[i])
            hp, b_p = _even_layer(hp, jnp.zeros((bp, K_B - 1, e_w_dw.shape[-1]), F32), *ew,
                                  n_streams=1, rows=PROMPT_ROWS, emit_v=False)
            hs, a_s, b_s = _even_layer(hs, cache_b_conv[i], *ew, n_streams=bs, rows=ts, emit_v=True)
            a_s_l.append(a_s); b_p_l.append(b_p); b_s_l.append(b_s)
        else:
            ow = (ln_mix[layer], o_w_in[i], o_w_pool[i], o_s_pool[i], o_w_out[i], rel_bias)
            d_d = o_w_pool.shape[1] * o_w_pool.shape[2]
            hp, k_p, v_p, ki_p, d_p = _odd_layer(
                hp, jnp.zeros((bp, 0, HD_C), F32), jnp.zeros((bp, 0, HD_C), F32),
                jnp.zeros((bp, 0, D_I), F32), jnp.zeros((bp, POOL_HIST, d_d), F32), 0, *ow,
                rows=PROMPT_ROWS)
            hs, k_s, v_s, ki_s, d_s = _odd_layer(
                hs, cache_c_k[i], cache_c_v[i], cache_c_kidx[i], cache_d_pool[i], POOL_HIST, *ow,
                rows=ts)
            ck_p_l.append(k_p); cv_p_l.append(v_p); cki_p_l.append(ki_p)
            ck_s_l.append(k_s); cv_s_l.append(v_s); cki_s_l.append(ki_s)
            d_p_l.append(d_p); d_s_l.append(d_s)
        g_final = ln_final if layer == depth - 1 else None
        fw = (ln_ffn[layer], f_w_up_b, f_w_dw[layer], f_b_dw[layer], f_w_down_b, g_final, layer)
        hp, f_p = _conv_ffn(hp, jnp.zeros((bp, K_FFN - 1, d_ff), F32), *fw,
                            n_streams=1, rows=PROMPT_ROWS)
        hs, f_s = _conv_ffn(hs, cache_ffn_conv[layer], *fw, n_streams=bs, rows=ts)
        f_p_l.append(f_p); f_s_l.append(f_s)
    return (hp, hs,
            jnp.stack(a_s_l), jnp.stack(b_p_l), jnp.stack(b_s_l),
            jnp.stack(ck_p_l), jnp.stack(cv_p_l), jnp.stack(cki_p_l),
            jnp.stack(ck_s_l), jnp.stack(cv_s_l), jnp.stack(cki_s_l),
            jnp.stack(d_p_l), jnp.stack(d_s_l),
            jnp.stack(f_p_l), jnp.stack(f_s_l))
```

```python
import functools
import math

import jax
import jax.numpy as jnp
import numpy as np
from jax import lax
from jax.experimental import pallas as pl
from jax.experimental.pallas import tpu as pltpu

F32 = jnp.float32
BF16 = jnp.bfloat16

EPS = 1e-6
CHUNK = 64
CHUNK_SHIFT = CHUNK.bit_length() - 1
SGU_CHUNK = 128
G_A = 4
K_B = 31
H_C = 12
HD_C = 64
H_I = 8
D_I = 64
TOPK_MAX = 256
N_BUCKETS = 32
MAX_DIST = 128
POOL_WINDOWS = (2, 4, 8, 16)
POOL_HIST = 15
K_FFN = 3

V7X_LANES = 128
V7X_SUBLANES = 8
V7X_MXU_DIM = 256
V7X_VMEM_LIMIT_BYTES = 56 * 1024 * 1024

KEY_TILE = V7X_MXU_DIM
PROMPT_ROWS = 512
NEG_INF = float("-inf")


def _rms(x, g):
    return x * lax.rsqrt(jnp.mean(x * x, axis=-1, keepdims=True) + EPS) * g


def _layer_norm(x, g, b):
    mu = jnp.mean(x, axis=-1, keepdims=True)
    xc = x - mu
    return xc * lax.rsqrt(jnp.mean(xc * xc, axis=-1, keepdims=True) + EPS) * g + b


def _gelu_tanh(x):
    cdf = 0.5 * (1.0 + jnp.tanh(math.sqrt(2.0 / math.pi) * (x + 0.044715 * (x * x * x))))
    return x * cdf


def _dot(a, b):
    return jnp.dot(a, b, preferred_element_type=F32)


def _dot_nt(a, b):
    return lax.dot_general(a, b, (((1,), (1,)), ((), ())), preferred_element_type=F32)


def _const_spec(shape):
    nd = len(shape)
    return pl.BlockSpec(shape, lambda *_: (0,) * nd, pipeline_mode=pl.Buffered(1))


def _ffn_kernel(*refs, n_streams, rows, final_norm):
    if final_norm:
        (x_ref, hist_ref, g_ref, wu_ref, wd_ref, dw_ref, bdw_ref, gf_ref,
         y_ref, nh_ref, carry_ref) = refs
    else:
        (x_ref, hist_ref, g_ref, wu_ref, wd_ref, dw_ref, bdw_ref,
         y_ref, nh_ref, carry_ref) = refs
        gf_ref = None
    d_model = x_ref.shape[-1]
    d_ff = wd_ref.shape[0]
    m = n_streams * rows
    sub = V7X_SUBLANES
    n_hist = K_FFN - 1

    @pl.when(pl.program_id(1) == 0)
    def _():
        carry_ref[:, sub - n_hist:sub, :] = hist_ref[...]

    x = x_ref[...].reshape(m, d_model)
    xn = _rms(x, g_ref[...]).astype(BF16)
    a = _dot(xn, wu_ref[:, :d_ff])
    val = _dot(xn, wu_ref[:, d_ff:])
    w = dw_ref[...]
    ys = []
    for s in range(n_streams):
        a_s = a[s * rows:(s + 1) * rows]
        ext = jnp.concatenate([carry_ref[s], a_s], axis=0)
        y_s = w[n_hist:K_FFN] * a_s
        for k in range(n_hist):
            y_s = y_s + w[k:k + 1] * ext[sub - n_hist + k:sub - n_hist + k + rows]
        ys.append(y_s)
        carry_ref[s] = a_s[rows - sub:rows]
        nh_ref[s] = a_s[rows - n_hist:rows]
    y = (ys[0] if n_streams == 1 else jnp.concatenate(ys, axis=0)) + bdw_ref[...]
    out = x + _dot((_gelu_tanh(y) * val).astype(BF16), wd_ref[...])
    if final_norm:
        out = _rms(out, gf_ref[...])
    y_ref[...] = out.reshape(y_ref.shape)


def _layer_spec(shape, layer):
    nd = len(shape)
    return pl.BlockSpec((None,) + tuple(shape[1:]), lambda *_: (layer,) + (0,) * (nd - 1),
                        pipeline_mode=pl.Buffered(1))


def _conv_ffn(x, hist, g_norm, w_up_all, w_dw, b_dw, w_down_all, g_final, layer, *, n_streams, rows):
    b, t, d = x.shape
    d_ff = w_down_all.shape[1]
    assert d_ff % V7X_LANES == 0
    assert b % n_streams == 0 and t % rows == 0 and rows % V7X_SUBLANES == 0
    final_norm = g_final is not None
    args = [x, hist, g_norm.reshape(1, d), w_up_all, w_down_all, w_dw, b_dw.reshape(1, d_ff)]
    in_specs = [
        pl.BlockSpec((n_streams, rows, d), lambda i, j: (i, j, 0)),
        pl.BlockSpec((n_streams, K_FFN - 1, d_ff), lambda i, j: (i, 0, 0)),
        _const_spec((1, d)),
        _layer_spec(w_up_all.shape, layer), _layer_spec(w_down_all.shape, layer),
        _const_spec(w_dw.shape), _const_spec((1, d_ff)),
    ]
    if final_norm:
        args.append(g_final.reshape(1, d))
        in_specs.append(_const_spec((1, d)))
    kern = functools.partial(_ffn_kernel, n_streams=n_streams, rows=rows, final_norm=final_norm)
    return pl.pallas_call(
        kern,
        grid=(b // n_streams, t // rows),
        in_specs=in_specs,
        out_specs=[
            pl.BlockSpec((n_streams, rows, d), lambda i, j: (i, j, 0)),
            pl.BlockSpec((n_streams, K_FFN - 1, d_ff), lambda i, j: (i, 0, 0)),
        ],
        out_shape=[jax.ShapeDtypeStruct((b, t, d), F32),
                   jax.ShapeDtypeStruct((b, K_FFN - 1, d_ff), F32)],
        scratch_shapes=[pltpu.VMEM((n_streams, V7X_SUBLANES, d_ff), F32)],
        compiler_params=pltpu.CompilerParams(
            dimension_semantics=("arbitrary", "arbitrary"),
            vmem_limit_bytes=V7X_VMEM_LIMIT_BYTES),
        name="conv_ffn",
    )(*args)


CONV_PAD = 32
CONV_ROW_BLOCK = 32
CONV_PITCH_PAD = 4


def _conv_rows_transposed(glu, s, hist_ref, wdw, bdw, stage_in_ref, stage_out_ref, prev_ref, cout_ref,
                          *, rows, first):
    sub = V7X_SUBLANES
    lanes = V7X_LANES
    nv = rows // sub
    pitch = nv + CONV_PITCH_PAD
    hist_rows = K_B - 1
    d_b = glu.shape[-1]
    sub_id = lax.broadcasted_iota(jnp.int32, (sub, lanes), 0)
    for lt in range(d_b // lanes):
        cols = slice(lt * lanes, (lt + 1) * lanes)

        @pl.when(first)
        def _():
            for e in range(hist_rows):
                prev_ref[s, lt, e, sub - 1:sub, :] = hist_ref[s, e:e + 1, cols]
        for q in range(sub):
            stage_in_ref[lt, q * pitch:q * pitch + nv, :] = (
                glu[s * rows + q * nv:s * rows + (q + 1) * nv, cols])
        cur = [stage_in_ref[lt, pl.ds(v, sub, stride=pitch), :] for v in range(nv)]
        head = []
        for e in range(hist_rows):
            merged = jnp.where(sub_id == sub - 1, prev_ref[s, lt, e], cur[nv - hist_rows + e])
            head.append(pltpu.roll(merged, 1, 0))
        for e in range(hist_rows):
            prev_ref[s, lt, e] = cur[nv - hist_rows + e]
        ext = head + cur
        wk = [jnp.broadcast_to(wdw[k:k + 1, cols], (sub, lanes)) for k in range(K_B)]
        bias = jnp.broadcast_to(bdw[:, cols], (sub, lanes))
        for v in range(nv):
            acc = bias
            for k in range(K_B):
                acc = acc + wk[k] * ext[v + k]
            stage_out_ref[lt, pl.ds(v, sub, stride=pitch), :] = acc
        for q in range(sub):
            cout_ref[s * rows + q * nv:s * rows + (q + 1) * nv, cols] = (
                stage_out_ref[lt, q * pitch:q * pitch + nv, :])


def _even_kernel(*refs, n_streams, rows, sgu_n, emit_v, transposed_conv):
    (x_ref, hist_ref, g_ref, win_ref, lnvg_ref, lnvb_ref, ws_ref, bs_ref, wdw_ref, bdw_ref,
     lncg_ref, lncb_ref, wout_ref) = refs[:13]
    n_out = 3 if emit_v else 2
    y_ref = refs[13]
    av_ref = refs[14] if emit_v else None
    nb_ref = refs[13 + n_out - 1]
    if transposed_conv:
        stage_in_ref, stage_out_ref, prev_ref, cout_ref = refs[13 + n_out:]
    else:
        conv_ref, cout_ref = refs[13 + n_out:]
    d_model = x_ref.shape[-1]
    d_b = wdw_ref.shape[-1]
    d_a = d_b
    dg = d_a // G_A
    hist_rows = K_B - 1
    off = CONV_PAD - hist_rows

    if not transposed_conv:
        @pl.when(pl.program_id(1) == 0)
        def _():
            conv_ref[:, off:CONV_PAD, :] = hist_ref[...]

    lnvg = lnvg_ref[...]
    lnvb = lnvb_ref[...]
    tril = (lax.broadcasted_iota(jnp.int32, (sgu_n, sgu_n), 0)
            >= lax.broadcasted_iota(jnp.int32, (sgu_n, sgu_n), 1))
    ws = [jnp.where(tril, ws_ref[g], 0.0).astype(BF16) for g in range(G_A)]
    bs = bs_ref[...]
    wdw = wdw_ref[...]
    bdw = bdw_ref[...]

    m = n_streams * rows
    x = x_ref[...].reshape(m, d_model)
    xn = _rms(x, g_ref[...]).astype(BF16)
    z = _dot(xn, win_ref[...])

    za = _gelu_tanh(z[:, :2 * d_a])
    u = za[:, :d_a]
    v = jnp.concatenate(
        [_layer_norm(za[:, d_a + g * dg:d_a + (g + 1) * dg], lnvg[:, g * dg:(g + 1) * dg],
                     lnvb[:, g * dg:(g + 1) * dg]) for g in range(G_A)], axis=1)
    if emit_v:
        av_ref[...] = v.reshape(av_ref.shape)
    vb = v.astype(BF16)
    sg_rows = []
    for c in range(m // sgu_n):
        vc = vb[c * sgu_n:(c + 1) * sgu_n]
        sg_rows.append(jnp.concatenate(
            [_dot(ws[g], vc[:, g * dg:(g + 1) * dg]) for g in range(G_A)], axis=1) + bs)
    sg = sg_rows[0] if len(sg_rows) == 1 else jnp.concatenate(sg_rows, axis=0)
    y_a = u * sg

    glu = z[:, 2 * d_a:2 * d_a + d_b] * jax.nn.sigmoid(z[:, 2 * d_a + d_b:])
    for s in range(n_streams):
        if transposed_conv:
            _conv_rows_transposed(glu, s, hist_ref, wdw, bdw, stage_in_ref, stage_out_ref, prev_ref,
                                  cout_ref, rows=rows, first=pl.program_id(1) == 0)
            nb_ref[s] = glu[(s + 1) * rows - hist_rows:(s + 1) * rows]
            continue
        conv_ref[s, CONV_PAD:CONV_PAD + rows, :] = glu[s * rows:(s + 1) * rows]
        for rb in range(rows // CONV_ROW_BLOCK):
            r0 = rb * CONV_ROW_BLOCK
            acc = jnp.zeros((CONV_ROW_BLOCK, d_b), F32) + bdw
            for k in range(K_B):
                acc = acc + wdw[k:k + 1] * conv_ref[s, r0 + off + k:r0 + off + k + CONV_ROW_BLOCK, :]
            cout_ref[s * rows + r0:s * rows + r0 + CONV_ROW_BLOCK, :] = acc
        nb_ref[s] = conv_ref[s, rows + off:rows + CONV_PAD, :]
        conv_ref[s, 0:CONV_PAD, :] = conv_ref[s, rows:rows + CONV_PAD, :]
    y_b = jax.nn.silu(_layer_norm(cout_ref[...], lncg_ref[...], lncb_ref[...]))

    y = _dot(jnp.concatenate([y_a, y_b], axis=1).astype(BF16), wout_ref[...])
    y_ref[...] = (x + y).reshape(y_ref.shape)


def _even_layer(x, conv_hist, g_norm, w_in, ln_v_g, ln_v_b, w_s, b_s, w_dw, b_dw, ln_c_g, ln_c_b,
                w_out, *, n_streams, rows, emit_v):
    b, t, d = x.shape
    d_b = w_dw.shape[-1]
    d_a = ln_v_g.shape[-1]
    dg = d_a // G_A
    sgu_n = min(t, SGU_CHUNK)
    assert b % n_streams == 0 and t % rows == 0 and rows % sgu_n == 0 and (n_streams * rows) % sgu_n == 0
    assert rows % CONV_ROW_BLOCK == 0 and rows >= CONV_PAD and d_a == d_b
    bs_full = jnp.repeat(b_s[:, :sgu_n].T, dg, axis=1)
    args = [x, conv_hist, g_norm.reshape(1, d), w_in.astype(BF16), ln_v_g.reshape(1, d_a),
            ln_v_b.reshape(1, d_a), w_s[:, :sgu_n, :sgu_n], bs_full, w_dw, b_dw.reshape(1, d_b),
            ln_c_g.reshape(1, d_b), ln_c_b.reshape(1, d_b), w_out.astype(BF16)]
    in_specs = [
        pl.BlockSpec((n_streams, rows, d), lambda i, j: (i, j, 0)),
        pl.BlockSpec((n_streams, K_B - 1, d_b), lambda i, j: (i, 0, 0)),
    ] + [_const_spec(a.shape) for a in args[2:]]
    out_specs = [pl.BlockSpec((n_streams, rows, d), lambda i, j: (i, j, 0))]
    out_shape = [jax.ShapeDtypeStruct((b, t, d), F32)]
    if emit_v:
        out_specs.append(pl.BlockSpec((n_streams, rows, d_a), lambda i, j: (i, j, 0)))
        out_shape.append(jax.ShapeDtypeStruct((b, t, d_a), F32))
    out_specs.append(pl.BlockSpec((n_streams, K_B - 1, d_b), lambda i, j: (i, 0, 0)))
    out_shape.append(jax.ShapeDtypeStruct((b, K_B - 1, d_b), F32))
    transposed_conv = rows // V7X_SUBLANES >= K_B - 1
    kern = functools.partial(_even_kernel, n_streams=n_streams, rows=rows, sgu_n=sgu_n, emit_v=emit_v,
                             transposed_conv=transposed_conv)
    if transposed_conv:
        stage_shape = (d_b // V7X_LANES, V7X_SUBLANES * (rows // V7X_SUBLANES + CONV_PITCH_PAD), V7X_LANES)
        conv_scratch = [pltpu.VMEM(stage_shape, F32), pltpu.VMEM(stage_shape, F32),
                        pltpu.VMEM((n_streams, d_b // V7X_LANES, K_B - 1, V7X_SUBLANES, V7X_LANES), F32)]
    else:
        conv_scratch = [pltpu.VMEM((n_streams, CONV_PAD + rows, d_b), F32)]
    return pl.pallas_call(
        kern,
        grid=(b // n_streams, t // rows),
        in_specs=in_specs,
        out_specs=out_specs,
        out_shape=out_shape,
        scratch_shapes=conv_scratch + [pltpu.VMEM((n_streams * rows, d_b), F32)],
        compiler_params=pltpu.CompilerParams(
            dimension_semantics=("arbitrary", "arbitrary"),
            vmem_limit_bytes=V7X_VMEM_LIMIT_BYTES),
        name="even_layer",
    )(*args)


POOL_PAD = 16
ATTN_HALF_ROWS = 256
SEARCH_GROUP = 16
SEARCH_ROWS = 16
INT_MIN = -2 ** 31
NEG_INF_KEY = -2 ** 31 + 0x7FFFFF
LOG2E = math.log2(math.e)
M_INIT = -3.0e38


def _t5_bucket(rel):
    nb = N_BUCKETS // 2
    exact = nb // 2
    side = jnp.where(rel > 0, nb, 0)
    n = jnp.abs(rel)
    large = exact + (jnp.log(jnp.maximum(n, 1).astype(jnp.float32) / exact)
                     / math.log(MAX_DIST / exact) * (nb - exact)).astype(jnp.int32)
    large = jnp.minimum(large, nb - 1)
    return side + jnp.where(n < exact, n, large)


def _sortable(score):
    bits = pltpu.bitcast(score, jnp.int32)
    return bits ^ ((bits >> 31) & 0x7FFFFFFF)


def _for_range(lo, hi, body):
    if isinstance(lo, int) and isinstance(hi, int):
        for j in range(lo, hi):
            body(j)
    else:
        def step(j, carry):
            body(j)
            return carry
        lax.fori_loop(lo, hi, step, 0)


def _odd_kernel(*refs, rows, qb, hist_len, n_valid_hist, top, has_hist, single_step):
    refs = list(refs)
    x_ref = refs.pop(0)
    if has_hist:
        kht_ref, vht_ref, kiht_ref = refs[:3]
        kkt_ref, kk2t_ref, vvt_ref = refs[-3:]
        refs = refs[3:-3]
    (poolh_ref, g_ref, win_ref, wpool_ref, spool_ref, wout_ref, bucket_ref, rb_ref, tri_ref,
     y_ref, k_ref, v_ref, ki_ref, np_ref,
     kk_ref, kk2_ref, vv_ref, keys_ref, planes_ref, acc_ref, m_ref, off_ref, thr_ref, need_ref,
     bias_ref, pool_ref, yc_ref, yd_ref, qe_ref, qo_ref, ie_ref, io_ref, wb_ref) = refs
    d_c = H_C * HD_C
    d_qi = H_I * D_I
    d_d = wpool_ref.shape[0]
    n_pairs = H_C // 2
    n_ipairs = H_I // 2
    kt = KEY_TILE
    lanes = V7X_LANES
    t = pl.program_id(1)
    t0 = 0 if single_step else t * rows
    hist_tiles = hist_len // kt
    col_xd = d_c + d_qi

    def cat(parts, axis):
        return parts[0] if len(parts) == 1 else jnp.concatenate(parts, axis=axis)

    def is_hist(js):
        hist = [has_hist and j < hist_tiles for j in js]
        assert all(hist) or not any(hist)
        return hist[0]

    def dot_keys_t(lhs, nat_ref, t_ref, js):
        if is_hist(js):
            return _dot(lhs, cat([t_ref[j] for j in js], 1))
        return _dot_nt(lhs, cat([nat_ref[j] for j in js], 0))

    def dot_values(p, js):
        if is_hist(js):
            return _dot_nt(p, cat([vvt_ref[j] for j in js], 1))
        return _dot(p, cat([vv_ref[j] for j in js], 0))

    @pl.when(jnp.logical_and(pl.program_id(0) == 0, t == 0))
    def _():
        bucket = bucket_ref[...]
        for h in range(H_C):
            b_acc = jnp.zeros(bucket.shape, F32)
            for b in range(N_BUCKETS):
                b_acc = jnp.where(bucket == b, rb_ref[b, h] * LOG2E, b_acc)
            bias_ref[0:2, h] = b_acc
            bias_ref[2, h] = jnp.zeros(bucket.shape[1:], F32)

    @pl.when(t == 0)
    def _():
        pool_ref[POOL_PAD - POOL_HIST:POOL_PAD, :] = poolh_ref[0]
        if has_hist:
            ones_half = jnp.ones((lanes - HD_C, kt), F32)
            for j in range(hist_tiles):
                kj = kht_ref[0, :, j * kt:(j + 1) * kt]
                kij = kiht_ref[0, :, j * kt:(j + 1) * kt]
                vj = vht_ref[0, :, j * kt:(j + 1) * kt]
                kkt_ref[j] = jnp.concatenate([kj, kij], axis=0).astype(BF16)
                kk2t_ref[j] = jnp.concatenate([kij, kj], axis=0).astype(BF16)
                vvt_ref[j] = jnp.concatenate([vj, ones_half], axis=0).astype(BF16)

    x = x_ref[0]
    xn = _rms(x, g_ref[...]).astype(BF16)
    z_tail = _dot(xn, win_ref[:, col_xd:])
    z = _dot(xn, win_ref[:, :col_xd])

    xd = z_tail[:, :d_d]
    pool_ref[POOL_PAD:POOL_PAD + rows, :] = xd
    np_ref[0] = pool_ref[rows + POOL_PAD - POOL_HIST:rows + POOL_PAD, :]
    run = xd
    wins = {}
    for dshift in range(1, POOL_WINDOWS[-1]):
        run = run + pool_ref[POOL_PAD - dshift:POOL_PAD - dshift + rows, :]
        if dshift + 1 in POOL_WINDOWS:
            wins[dshift + 1] = run
    pool_ref[0:POOL_PAD, :] = pool_ref[rows:rows + POOL_PAD, :]
    dg_d = d_d // len(POOL_WINDOWS)
    lane_d = lax.broadcasted_iota(jnp.int32, (rows, d_d), 1)
    tpos = n_valid_hist + t0 + 1 + lax.broadcasted_iota(jnp.int32, (rows, d_d), 0)
    win_sum = wins[POOL_WINDOWS[-1]]
    width = jnp.full((rows, d_d), POOL_WINDOWS[-1], jnp.int32)
    for gi in range(len(POOL_WINDOWS) - 2, -1, -1):
        in_g = lane_d < (gi + 1) * dg_d
        win_sum = jnp.where(in_g, wins[POOL_WINDOWS[gi]], win_sum)
        width = jnp.where(in_g, POOL_WINDOWS[gi], width)
    count = jnp.minimum(tpos, width).astype(F32)
    m_pool = win_sum / count - xd
    yd_ref[...] = _dot(m_pool.astype(BF16), wpool_ref[...]) * spool_ref[...]

    g1 = z_tail[:, d_d:d_d + lanes]
    g2 = z_tail[:, d_d + lanes:]
    k_ref[0] = g1[:, :HD_C]
    v_ref[0] = g1[:, HD_C:]
    ki_ref[0] = g2[:, :D_I]
    lane = lax.broadcasted_iota(jnp.int32, (rows, lanes), 1)
    low = lane < HD_C
    g1r = pltpu.roll(g1, HD_C, 1)
    g2r = pltpu.roll(g2, D_I, 1)
    kk_new = jnp.where(low, g1, g2r).astype(BF16)
    kk2_new = jnp.where(low, g2, g1r).astype(BF16)
    vv_new = jnp.where(low, g1r, 1.0).astype(BF16)
    if rows % kt == 0:
        base_tile = (hist_len + t0) // kt
        for i in range(rows // kt):
            kk_ref[base_tile + i] = kk_new[i * kt:(i + 1) * kt]
            kk2_ref[base_tile + i] = kk2_new[i * kt:(i + 1) * kt]
            vv_ref[base_tile + i] = vv_new[i * kt:(i + 1) * kt]
    else:
        zpad = jnp.zeros((kt - rows, lanes), BF16)
        kk_ref[hist_tiles] = jnp.concatenate([kk_new, zpad], axis=0)
        kk2_ref[hist_tiles] = jnp.concatenate([kk2_new, zpad], axis=0)
        vv_ref[hist_tiles] = jnp.concatenate([vv_new, zpad], axis=0)
    kv_len = hist_len + t0 + rows

    w_idx = g2[:, D_I:D_I + H_I] * ((H_I ** -0.5) * (D_I ** -0.5))
    tri = tri_ref[...]
    ones_rhs = jnp.ones((kt, lanes), BF16)
    hb = min(rows, ATTN_HALF_ROWS)
    n_half = rows // hb
    n_new = max(rows // kt, 1)
    low_h = lax.broadcasted_iota(jnp.int32, (hb, lanes), 1) < HD_C

    jb = (hist_len + t0) // kt
    n_tiles = jb + n_new

    for hf in range(n_half):
        zr = z[hf * hb:(hf + 1) * hb]
        for g in range(n_pairs):
            grp = zr[:, g * lanes:(g + 1) * lanes] * (HD_C ** -0.5 * LOG2E)
            qe_ref[(hf * n_pairs + g) * hb:(hf * n_pairs + g + 1) * hb] = (
                jnp.where(low_h, grp, 0.0).astype(BF16))
            qo_ref[(hf * n_pairs + g) * hb:(hf * n_pairs + g + 1) * hb] = (
                jnp.where(low_h, 0.0, grp).astype(BF16))
    for g in range(n_ipairs):
        grp = z[:, d_c + g * lanes:d_c + (g + 1) * lanes]
        ie_ref[g * rows:(g + 1) * rows] = jnp.where(low, grp, 0.0).astype(BF16)
        io_ref[g * rows:(g + 1) * rows] = jnp.where(low, 0.0, grp).astype(BF16)
    for h in range(H_I):
        wb_ref[h] = jnp.broadcast_to(w_idx[:, h:h + 1], (rows, lanes))

    def score_body(j, r0=0):
        nr = rows - r0
        ie = jnp.concatenate([ie_ref[g * rows + r0:(g + 1) * rows] for g in range(n_ipairs)], axis=0)
        io = jnp.concatenate([io_ref[g * rows + r0:(g + 1) * rows] for g in range(n_ipairs)], axis=0)
        se = jnp.maximum(dot_keys_t(ie, kk2_ref, kk2t_ref if has_hist else None, [j]), 0.0)
        so = jnp.maximum(dot_keys_t(io, kk_ref, kkt_ref if has_hist else None, [j]), 0.0)
        score = jnp.zeros((nr, kt), F32)
        for g in range(n_ipairs):
            we = wb_ref[2 * g, r0:rows]
            wo = wb_ref[2 * g + 1, r0:rows]
            score = score + se[g * nr:(g + 1) * nr] * jnp.concatenate([we] * (kt // lanes), axis=1)
            score = score + so[g * nr:(g + 1) * nr] * jnp.concatenate([wo] * (kt // lanes), axis=1)
        qchunk = (hist_len + t0 + r0 + lax.broadcasted_iota(jnp.int32, (nr, 1), 0)) >> CHUNK_SHIFT
        kpos = lax.broadcasted_iota(jnp.int32, (nr, kt), 1) + j * kt
        adm = jnp.logical_and((kpos >> CHUNK_SHIFT) <= qchunk, kpos < kv_len)
        keys_ref[j, r0:rows] = _sortable(jnp.where(adm, score, NEG_INF))
        if r0:
            keys_ref[j, 0:r0] = jnp.full((r0, kt), NEG_INF_KEY, jnp.int32)
    _for_range(0, jb + 1, score_body)
    for dj in range(1, n_new):
        first_half = min(hf for hf in range(n_half)
                         if ((hf + 1) * hb - qb) // lanes >= (kt // lanes) * dj)
        score_body(jb + dj, first_half * hb)

    def fill_body(j):
        keys_ref[j] = jnp.full((rows, kt), INT_MIN, jnp.int32)
    _for_range(n_tiles, keys_ref.shape[0], fill_body)

    n_groups = keys_ref.shape[0] * (kt // lanes) // SEARCH_GROUP

    def plane_body(rc, carry):
        r = pl.multiple_of(rc * SEARCH_ROWS, SEARCH_ROWS)
        for gi in range(n_groups):
            a = []
            for i in range(SEARCH_GROUP):
                lt = gi * SEARCH_GROUP + i
                a.append(keys_ref[lt // (kt // lanes), pl.ds(r, SEARCH_ROWS),
                                  (lt % (kt // lanes)) * lanes:(lt % (kt // lanes) + 1) * lanes])
            for sh, msk in ((8, 0x00FF00FF), (4, 0x0F0F0F0F), (2, 0x33333333), (1, 0x55555555)):
                for k in range(SEARCH_GROUP):
                    if k & sh == 0:
                        tmp = (a[k] ^ lax.shift_right_logical(a[k + sh], sh)) & msk
                        a[k] = a[k] ^ tmp
                        a[k + sh] = a[k + sh] ^ (tmp << sh)
            for w in range(SEARCH_GROUP):
                planes_ref[gi, w, pl.ds(r, SEARCH_ROWS), :] = a[w]
        return carry

    def plane_body_half(rc, carry):
        half = SEARCH_GROUP // 2
        r = pl.multiple_of(rc * SEARCH_ROWS, SEARCH_ROWS)
        a = [keys_ref[lt // (kt // lanes), pl.ds(r, SEARCH_ROWS),
                      (lt % (kt // lanes)) * lanes:(lt % (kt // lanes) + 1) * lanes] for lt in range(half)]
        for sh, msk in ((4, 0x0F0F0F0F), (2, 0x33333333), (1, 0x55555555)):
            for k in range(half):
                if k & sh == 0:
                    tmp = (a[k] ^ lax.shift_right_logical(a[k + sh], sh)) & msk
                    a[k] = a[k] ^ tmp
                    a[k + sh] = a[k + sh] ^ (tmp << sh)
        top_bytes = jnp.int32(-16711936)
        for w in range(half):
            word = a[w] & top_bytes
            if w == 0:
                word = word | jnp.int32(0x00FF0000)
            planes_ref[0, w, pl.ds(r, SEARCH_ROWS), :] = word
            planes_ref[0, w + half, pl.ds(r, SEARCH_ROWS), :] = (a[w] << 8) & top_bytes
        return carry

    if n_groups == 1 and not single_step:
        few_keys = n_tiles * (kt // lanes) <= SEARCH_GROUP // 2

        @pl.when(few_keys)
        def _():
            lax.fori_loop(0, rows // SEARCH_ROWS, plane_body_half, 0)

        @pl.when(jnp.logical_not(few_keys))
        def _():
            lax.fori_loop(0, rows // SEARCH_ROWS, plane_body, 0)
    else:
        lax.fori_loop(0, rows // SEARCH_ROWS, plane_body, 0)

    alive = [jnp.full((rows, lanes), -65536, jnp.int32) for _ in range(n_groups)]
    above = jnp.zeros((rows, 1), F32)
    thr_u = jnp.zeros((rows, 1), jnp.int32)
    for b in range(31, -1, -1):
        w = (31 - b) if b >= 16 else (15 - b)
        if b == 15:
            alive = [lax.shift_right_logical(a, 16) for a in alive]
        ones = []
        for gi in range(n_groups):
            plane = planes_ref[gi, w]
            if b == 31:
                plane = ~plane
            ones.append(alive[gi] & plane)
        pc = lax.population_count(ones[0])
        for gi in range(1, n_groups):
            pc = pc + lax.population_count(ones[gi])
        cnt = jnp.sum(pc.astype(F32), axis=1, keepdims=True)
        take = (above + cnt) >= float(top)
        alive = [jnp.where(take, o, a ^ o) for a, o in zip(alive, ones)]
        above = jnp.where(take, above, above + cnt)
        thr_u = jnp.where(take, thr_u | jnp.int32(INT_MIN if b == 31 else (1 << b)), thr_u)
    thr_ref[...] = jnp.broadcast_to(thr_u ^ jnp.int32(INT_MIN), (rows, lanes))
    need_ref[...] = jnp.broadcast_to(float(top) - above, (rows, lanes))
    off_ref[...] = jnp.zeros((rows, lanes), F32)
    m_ref[...] = jnp.full(m_ref.shape, M_INIT, F32)
    acc_ref[...] = jnp.zeros(acc_ref.shape, F32)

    def attn_half(tiles, hf):
        rs = slice(hf * hb, (hf + 1) * hb)
        js = [j for j, _ in tiles]
        n_lane_tiles = len(tiles) * (kt // lanes)
        thr_t = jnp.concatenate([thr_ref[rs]] * (kt // lanes), axis=1)
        need_t = jnp.concatenate([need_ref[rs]] * (kt // lanes), axis=1)
        masks = []
        for j in js:
            kj = keys_ref[j, rs]
            eq = jnp.where(kj == thr_t, 1.0, 0.0)
            eq_b = eq.astype(BF16)
            rank = _dot(eq_b, tri) + jnp.concatenate([off_ref[rs]] * (kt // lanes), axis=1)
            self_ = jnp.where(kj > thr_t, 1.0, jnp.where(rank <= need_t, eq, 0.0))
            sel = jnp.where(kj > NEG_INF_KEY, self_, 0.0) > 0.5
            masks.append(jnp.where(sel, 0.0, NEG_INF))
            off_ref[rs] = off_ref[rs] + _dot(eq_b, ones_rhs)
        mask_add = masks[0] if len(masks) == 1 else jnp.concatenate(masks, axis=1)
        q_rows = slice(hf * n_pairs * hb, (hf + 1) * n_pairs * hb)
        lg = (dot_keys_t(qe_ref[q_rows], kk_ref, kkt_ref if has_hist else None, js),
              dot_keys_t(qo_ref[q_rows], kk2_ref, kk2t_ref if has_hist else None, js))
        ps = []
        alphas = []
        for eo in range(2):
            for g in range(n_pairs):
                h = 2 * g + eo
                l = lg[eo][g * hb:(g + 1) * hb]
                if any(dj is not None for _, dj in tiles):
                    row_parts = []
                    for sub in range(hb // qb):
                        sblk = ((hf * (hb // qb) + sub) * qb) // lanes
                        parts = []
                        for ti, (_, dj) in enumerate(tiles):
                            for c in range(kt // lanes):
                                lane0 = ti * kt + c * lanes
                                part = l[sub * qb:(sub + 1) * qb, lane0:lane0 + lanes]
                                if dj is not None and (kt // lanes) * dj + c - sblk in (-1, 0):
                                    part = part + bias_ref[(kt // lanes) * dj + c - sblk + 1, h]
                                parts.append(part)
                        row_parts.append(jnp.concatenate(parts, axis=1))
                    l = row_parts[0] if len(row_parts) == 1 else jnp.concatenate(row_parts, axis=0)
                l = l + mask_add
                st = slice(((hf * 2 + eo) * n_pairs + g) * hb, ((hf * 2 + eo) * n_pairs + g + 1) * hb)
                m_old = m_ref[st]
                m_new = jnp.maximum(m_old, jnp.max(l, axis=1, keepdims=True))
                alphas.append(jnp.exp2(m_old - m_new))
                ps.append(jnp.exp2(l - jnp.concatenate([m_new] * n_lane_tiles, axis=1)).astype(BF16))
                m_ref[st] = m_new
        a_rows = slice(hf * H_C * hb, (hf + 1) * H_C * hb)
        acc_ref[a_rows] = (acc_ref[a_rows] * jnp.concatenate(alphas, axis=0)
                           + dot_values(jnp.concatenate(ps, axis=0), js))

    def far_pair(k):
        for hf in range(n_half):
            attn_half([(2 * k, None), (2 * k + 1, None)], hf)
    _for_range(0, jb // 2 - 1, far_pair)

    def last_old_pair():
        for hf in range(n_half):
            attn_half([(jb - 2, -2), (jb - 1, -1)], hf)
    if single_step:
        if jb >= 2:
            last_old_pair()
    else:
        pl.when(jb >= 2)(last_old_pair)

    for hf in range(n_half):
        new = [(jb + dj, dj) for dj in range(n_new)
               if ((hf + 1) * hb - qb) // lanes >= (kt // lanes) * dj]
        for i in range(0, len(new), 2):
            attn_half(new[i:i + 2], hf)
        for g in range(n_pairs):
            oe = acc_ref[((hf * 2) * n_pairs + g) * hb:((hf * 2) * n_pairs + g + 1) * hb]
            oo = acc_ref[((hf * 2 + 1) * n_pairs + g) * hb:((hf * 2 + 1) * n_pairs + g + 1) * hb]
            num = jnp.where(low_h, oe, pltpu.roll(oo, HD_C, 1))
            den = jnp.where(low_h, pltpu.roll(oe, HD_C, 1), oo)
            yc_ref[hf * hb:(hf + 1) * hb, g * lanes:(g + 1) * lanes] = num / den
        rs = slice(hf * hb, (hf + 1) * hb)
        y_cat = jnp.concatenate([yc_ref[rs], yd_ref[rs]], axis=1).astype(BF16)
        y_ref[0, rs, :] = x[rs] + _dot(y_cat, wout_ref[...])


def _odd_layer(x, k_hist, v_hist, ki_hist, pool_hist, n_valid_hist, g_norm, w_in, w_pool, s_pool,
               w_out, rel_bias, *, rows):
    b, t, d = x.shape
    hist_len = k_hist.shape[1]
    has_hist = hist_len > 0
    d_c = H_C * HD_C
    d_qi = H_I * D_I
    d_d = d - d_c
    qb = min(2 * CHUNK, rows)
    kt = KEY_TILE
    lanes = V7X_LANES
    s_total = hist_len + t
    top = min(TOPK_MAX, s_total // 4)
    assert t % rows == 0 and rows % qb == 0 and hist_len % kt == 0
    assert rows % kt == 0 or (rows == t and rows == qb and rows <= CHUNK)
    assert hist_len % (2 * kt) == 0 and (rows % (2 * kt) == 0 or rows == t)
    n_tiles = (s_total + kt - 1) // kt
    tiles_per_group = SEARCH_GROUP * lanes // kt
    n_groups = (n_tiles + tiles_per_group - 1) // tiles_per_group
    n_tiles_pad = n_groups * tiles_per_group
    assert rows % SEARCH_ROWS == 0 and rows % min(rows, ATTN_HALF_ROWS) == 0

    offs = np.cumsum([0, d_c, HD_C, HD_C, d_qi, D_I, H_I]).tolist()
    q_w, k_w, v_w, qi_w, ki_w, wi_w = (w_in[:, offs[i]:offs[i + 1]] for i in range(6))
    xd_w = w_in[:, offs[6]:]
    pad_w = jnp.zeros((d, lanes - D_I - H_I), w_in.dtype)
    w_all = jnp.concatenate([q_w, qi_w, xd_w, k_w, v_w, ki_w, wi_w, pad_w], axis=1).astype(BF16)
    wpool_bd = jax.scipy.linalg.block_diag(*[w_pool[g] for g in range(w_pool.shape[0])]).astype(BF16)

    rel = (lanes * jnp.arange(-1, 1, dtype=jnp.int32)[:, None, None]
           + jnp.arange(lanes, dtype=jnp.int32)[None, None, :]
           - jnp.arange(qb, dtype=jnp.int32)[None, :, None])
    bucket = _t5_bucket(rel)
    far_bucket = _t5_bucket(jnp.int32(-2 * lanes))
    rb_shift = rel_bias - rel_bias[far_bucket][None, :]
    tri = (jnp.arange(kt)[:, None] <= jnp.arange(kt)[None, :]).astype(BF16)

    args = [x]
    in_specs = [pl.BlockSpec((1, rows, d), lambda i, j: (i, j, 0))]
    if has_hist:
        assert rows == t, "history tiles are addressed statically"
        args += [jnp.swapaxes(k_hist, 1, 2), jnp.swapaxes(v_hist, 1, 2), jnp.swapaxes(ki_hist, 1, 2)]
        in_specs += [pl.BlockSpec((1, HD_C, hist_len), lambda i, j: (i, 0, 0))] * 2
        in_specs += [pl.BlockSpec((1, D_I, hist_len), lambda i, j: (i, 0, 0))]
    consts = [g_norm.reshape(1, d), w_all, wpool_bd, s_pool.reshape(1, d_d), w_out.astype(BF16), bucket]
    args += [pool_hist] + consts + [rb_shift, tri]
    in_specs += ([pl.BlockSpec((1, POOL_HIST, d_d), lambda i, j: (i, 0, 0))]
                 + [_const_spec(a.shape) for a in consts]
                 + [pl.BlockSpec(memory_space=pltpu.SMEM), _const_spec(tri.shape)])
    kern = functools.partial(_odd_kernel, rows=rows, qb=qb, hist_len=hist_len,
                             n_valid_hist=n_valid_hist, top=top, has_hist=has_hist,
                             single_step=(rows == t))
    return pl.pallas_call(
        kern,
        grid=(b, t // rows),
        in_specs=in_specs,
        out_specs=[
            pl.BlockSpec((1, rows, d), lambda i, j: (i, j, 0)),
            pl.BlockSpec((1, rows, HD_C), lambda i, j: (i, j, 0)),
            pl.BlockSpec((1, rows, HD_C), lambda i, j: (i, j, 0)),
            pl.BlockSpec((1, rows, D_I), lambda i, j: (i, j, 0)),
            pl.BlockSpec((1, POOL_HIST, d_d), lambda i, j: (i, 0, 0)),
        ],
        out_shape=[jax.ShapeDtypeStruct((b, t, d), F32),
                   jax.ShapeDtypeStruct((b, t, HD_C), F32),
                   jax.ShapeDtypeStruct((b, t, HD_C), F32),
                   jax.ShapeDtypeStruct((b, t, D_I), F32),
                   jax.ShapeDtypeStruct((b, POOL_HIST, d_d), F32)],
        scratch_shapes=[
            pltpu.VMEM((n_tiles, kt, lanes), BF16),
            pltpu.VMEM((n_tiles, kt, lanes), BF16),
            pltpu.VMEM((n_tiles, kt, lanes), BF16),
            pltpu.VMEM((n_tiles_pad, rows, kt), jnp.int32),
            pltpu.VMEM((n_groups, SEARCH_GROUP, rows, lanes), jnp.int32),
            pltpu.VMEM((H_C * rows, lanes), F32),
            pltpu.VMEM((H_C * rows, lanes), F32),
            pltpu.VMEM((rows, lanes), F32),
            pltpu.VMEM((rows, lanes), jnp.int32),
            pltpu.VMEM((rows, lanes), F32),
            pltpu.VMEM((3, H_C, qb, lanes), F32),
            pltpu.VMEM((POOL_PAD + rows, d_d), F32),
            pltpu.VMEM((rows, d_c), F32),
            pltpu.VMEM((rows, d_d), F32),
            pltpu.VMEM((H_C // 2 * rows, lanes), BF16),
            pltpu.VMEM((H_C // 2 * rows, lanes), BF16),
            pltpu.VMEM((H_I // 2 * rows, lanes), BF16),
            pltpu.VMEM((H_I // 2 * rows, lanes), BF16),
            pltpu.VMEM((H_I, rows, lanes), F32),
        ] + ([pltpu.VMEM((hist_len // kt, lanes, kt), BF16)] * 3 if has_hist else []),
        compiler_params=pltpu.CompilerParams(
            dimension_semantics=("arbitrary", "arbitrary"),
            vmem_limit_bytes=V7X_VMEM_LIMIT_BYTES),
        name="odd_layer",
    )(*args)


def kernel(x_prompt, x_sample, cache_b_conv, cache_c_k, cache_c_v, cache_c_kidx, cache_d_pool, cache_ffn_conv, ln_mix, ln_ffn, ln_final, e_w_in, e_ln_v_g, e_ln_v_b, e_w_s, e_b_s, e_w_dw, e_b_dw, e_ln_c_g, e_ln_c_b, e_w_out, o_w_in, o_w_pool, o_s_pool, o_w_out, rel_bias, f_w_up, f_w_dw, f_b_dw, f_w_down):
    hp, hs = x_prompt, x_sample
    bp, bs = x_prompt.shape[0], x_sample.shape[0]
    ts = x_sample.shape[1]
    depth = ln_mix.shape[0]
    d_ff = f_w_down.shape[1]
    a_s_l, b_p_l, b_s_l = [], [], []
    ck_p_l, cv_p_l, cki_p_l, ck_s_l, cv_s_l, cki_s_l, d_p_l, d_s_l = [], [], [], [], [], [], [], []
    f_p_l, f_s_l = [], []
    f_w_up_b = f_w_up.astype(BF16)
    f_w_down_b = f_w_down.astype(BF16)
    for layer in range(depth):
        i = layer // 2
        if layer % 2 == 0:
            ew = (ln_mix[layer], e_w_in[i], e_ln_v_g[i], e_ln_v_b[i], e_w_s[i], e_b_s[i], e_w_dw[i],
                  e_b_dw[i], e_ln_c_g[i], e_ln_c_b[i], e_w_out[i])
            hp, b_p = _even_layer(hp, jnp.zeros((bp, K_B - 1, e_w_dw.shape[-1]), F32), *ew,
                                  n_streams=1, rows=PROMPT_ROWS, emit_v=False)
            hs, a_s, b_s = _even_layer(hs, cache_b_conv[i], *ew, n_streams=bs, rows=ts, emit_v=True)
            a_s_l.append(a_s); b_p_l.append(b_p); b_s_l.append(b_s)
        else:
            ow = (ln_mix[layer], o_w_in[i], o_w_pool[i], o_s_pool[i], o_w_out[i], rel_bias)
            d_d = o_w_pool.shape[1] * o_w_pool.shape[2]
            hp, k_p, v_p, ki_p, d_p = _odd_layer(
                hp, jnp.zeros((bp, 0, HD_C), F32), jnp.zeros((bp, 0, HD_C), F32),
                jnp.zeros((bp, 0, D_I), F32), jnp.zeros((bp, POOL_HIST, d_d), F32), 0, *ow,
                rows=PROMPT_ROWS)
            hs, k_s, v_s, ki_s, d_s = _odd_layer(
                hs, cache_c_k[i], cache_c_v[i], cache_c_kidx[i], cache_d_pool[i], POOL_HIST, *ow,
                rows=ts)
            ck_p_l.append(k_p); cv_p_l.append(v_p); cki_p_l.append(ki_p)
            ck_s_l.append(k_s); cv_s_l.append(v_s); cki_s_l.append(ki_s)
            d_p_l.append(d_p); d_s_l.append(d_s)
        g_final = ln_final if layer == depth - 1 else None
        fw = (ln_ffn[layer], f_w_up_b, f_w_dw[layer], f_b_dw[layer], f_w_down_b, g_final, layer)
        hp, f_p = _conv_ffn(hp, jnp.zeros((bp, K_FFN - 1, d_ff), F32), *fw,
                            n_streams=1, rows=PROMPT_ROWS)
        hs, f_s = _conv_ffn(hs, cache_ffn_conv[layer], *fw, n_streams=bs, rows=ts)
        f_p_l.append(f_p); f_s_l.append(f_s)
    return (hp, hs,
            jnp.stack(a_s_l), jnp.stack(b_p_l), jnp.stack(b_s_l),
            jnp.stack(ck_p_l), jnp.stack(cv_p_l), jnp.stack(cki_p_l),
            jnp.stack(ck_s_l), jnp.stack(cv_s_l), jnp.stack(cki_s_l),
            jnp.stack(d_p_l), jnp.stack(d_s_l),
            jnp.stack(f_p_l), jnp.stack(f_s_l))
```

```python
import functools
import math

import jax
import jax.numpy as jnp
import numpy as np
from jax import lax
from jax.experimental import pallas as pl
from jax.experimental.pallas import tpu as pltpu

F32 = jnp.float32
BF16 = jnp.bfloat16

EPS = 1e-6
CHUNK = 64
CHUNK_SHIFT = CHUNK.bit_length() - 1
SGU_CHUNK = 128
G_A = 4
K_B = 31
H_C = 12
HD_C = 64
H_I = 8
D_I = 64
TOPK_MAX = 256
N_BUCKETS = 32
MAX_DIST = 128
POOL_WINDOWS = (2, 4, 8, 16)
POOL_HIST = 15
K_FFN = 3

V7X_LANES = 128
V7X_SUBLANES = 8
V7X_MXU_DIM = 256
V7X_VMEM_LIMIT_BYTES = 56 * 1024 * 1024

KEY_TILE = V7X_MXU_DIM
PROMPT_ROWS = 512
NEG_INF = float("-inf")


def _rms(x, g):
    return x * lax.rsqrt(jnp.mean(x * x, axis=-1, keepdims=True) + EPS) * g


def _layer_norm(x, g, b):
    mu = jnp.mean(x, axis=-1, keepdims=True)
    xc = x - mu
    return xc * lax.rsqrt(jnp.mean(xc * xc, axis=-1, keepdims=True) + EPS) * g + b


def _gelu_tanh(x):
    cdf = 0.5 * (1.0 + jnp.tanh(math.sqrt(2.0 / math.pi) * (x + 0.044715 * (x * x * x))))
    return x * cdf


def _dot(a, b):
    return jnp.dot(a, b, preferred_element_type=F32)


def _dot_nt(a, b):
    return lax.dot_general(a, b, (((1,), (1,)), ((), ())), preferred_element_type=F32)


def _const_spec(shape):
    nd = len(shape)
    return pl.BlockSpec(shape, lambda *_: (0,) * nd, pipeline_mode=pl.Buffered(1))


def _ffn_kernel(*refs, n_streams, rows, final_norm):
    if final_norm:
        (x_ref, hist_ref, g_ref, wu_ref, wd_ref, dw_ref, bdw_ref, gf_ref,
         y_ref, nh_ref, carry_ref) = refs
    else:
        (x_ref, hist_ref, g_ref, wu_ref, wd_ref, dw_ref, bdw_ref,
         y_ref, nh_ref, carry_ref) = refs
        gf_ref = None
    d_model = x_ref.shape[-1]
    d_ff = wd_ref.shape[0]
    m = n_streams * rows
    sub = V7X_SUBLANES
    n_hist = K_FFN - 1

    @pl.when(pl.program_id(1) == 0)
    def _():
        carry_ref[:, sub - n_hist:sub, :] = hist_ref[...]

    x = x_ref[...].reshape(m, d_model)
    xn = _rms(x, g_ref[...]).astype(BF16)
    a = _dot(xn, wu_ref[:, :d_ff])
    val = _dot(xn, wu_ref[:, d_ff:])
    w = dw_ref[...]
    ys = []
    for s in range(n_streams):
        a_s = a[s * rows:(s + 1) * rows]
        ext = jnp.concatenate([carry_ref[s], a_s], axis=0)
        y_s = w[n_hist:K_FFN] * a_s
        for k in range(n_hist):
            y_s = y_s + w[k:k + 1] * ext[sub - n_hist + k:sub - n_hist + k + rows]
        ys.append(y_s)
        carry_ref[s] = a_s[rows - sub:rows]
        nh_ref[s] = a_s[rows - n_hist:rows]
    y = (ys[0] if n_streams == 1 else jnp.concatenate(ys, axis=0)) + bdw_ref[...]
    out = x + _dot((_gelu_tanh(y) * val).astype(BF16), wd_ref[...])
    if final_norm:
        out = _rms(out, gf_ref[...])
    y_ref[...] = out.reshape(y_ref.shape)


def _layer_spec(shape, layer):
    nd = len(shape)
    return pl.BlockSpec((None,) + tuple(shape[1:]), lambda *_: (layer,) + (0,) * (nd - 1),
                        pipeline_mode=pl.Buffered(1))


def _conv_ffn(x, hist, g_norm, w_up_all, w_dw, b_dw, w_down_all, g_final, layer, *, n_streams, rows):
    b, t, d = x.shape
    d_ff = w_down_all.shape[1]
    assert d_ff % V7X_LANES == 0
    assert b % n_streams == 0 and t % rows == 0 and rows % V7X_SUBLANES == 0
    final_norm = g_final is not None
    args = [x, hist, g_norm.reshape(1, d), w_up_all, w_down_all, w_dw, b_dw.reshape(1, d_ff)]
    in_specs = [
        pl.BlockSpec((n_streams, rows, d), lambda i, j: (i, j, 0)),
        pl.BlockSpec((n_streams, K_FFN - 1, d_ff), lambda i, j: (i, 0, 0)),
        _const_spec((1, d)),
        _layer_spec(w_up_all.shape, layer), _layer_spec(w_down_all.shape, layer),
        _const_spec(w_dw.shape), _const_spec((1, d_ff)),
    ]
    if final_norm:
        args.append(g_final.reshape(1, d))
        in_specs.append(_const_spec((1, d)))
    kern = functools.partial(_ffn_kernel, n_streams=n_streams, rows=rows, final_norm=final_norm)
    return pl.pallas_call(
        kern,
        grid=(b // n_streams, t // rows),
        in_specs=in_specs,
        out_specs=[
            pl.BlockSpec((n_streams, rows, d), lambda i, j: (i, j, 0)),
            pl.BlockSpec((n_streams, K_FFN - 1, d_ff), lambda i, j: (i, 0, 0)),
        ],
        out_shape=[jax.ShapeDtypeStruct((b, t, d), F32),
                   jax.ShapeDtypeStruct((b, K_FFN - 1, d_ff), F32)],
        scratch_shapes=[pltpu.VMEM((n_streams, V7X_SUBLANES, d_ff), F32)],
        compiler_params=pltpu.CompilerParams(
            dimension_semantics=("arbitrary", "arbitrary"),
            vmem_limit_bytes=V7X_VMEM_LIMIT_BYTES),
        name="conv_ffn",
    )(*args)


CONV_PAD = 32
CONV_ROW_BLOCK = 32
CONV_PITCH_PAD = 4


def _conv_rows_transposed(glu, s, hist_ref, wdw, bdw, stage_in_ref, stage_out_ref, prev_ref, cout_ref,
                          *, rows, first):
    sub = V7X_SUBLANES
    lanes = V7X_LANES
    nv = rows // sub
    pitch = nv + CONV_PITCH_PAD
    hist_rows = K_B - 1
    d_b = glu.shape[-1]
    sub_id = lax.broadcasted_iota(jnp.int32, (sub, lanes), 0)
    for lt in range(d_b // lanes):
        cols = slice(lt * lanes, (lt + 1) * lanes)

        @pl.when(first)
        def _():
            for e in range(hist_rows):
                prev_ref[s, lt, e, sub - 1:sub, :] = hist_ref[s, e:e + 1, cols]
        for q in range(sub):
            stage_in_ref[lt, q * pitch:q * pitch + nv, :] = (
                glu[s * rows + q * nv:s * rows + (q + 1) * nv, cols])
        cur = [stage_in_ref[lt, pl.ds(v, sub, stride=pitch), :] for v in range(nv)]
        head = []
        for e in range(hist_rows):
            merged = jnp.where(sub_id == sub - 1, prev_ref[s, lt, e], cur[nv - hist_rows + e])
            head.append(pltpu.roll(merged, 1, 0))
        for e in range(hist_rows):
            prev_ref[s, lt, e] = cur[nv - hist_rows + e]
        ext = head + cur
        wk = [jnp.broadcast_to(wdw[k:k + 1, cols], (sub, lanes)) for k in range(K_B)]
        bias = jnp.broadcast_to(bdw[:, cols], (sub, lanes))
        for v in range(nv):
            acc = bias
            for k in range(K_B):
                acc = acc + wk[k] * ext[v + k]
            stage_out_ref[lt, pl.ds(v, sub, stride=pitch), :] = acc
        for q in range(sub):
            cout_ref[s * rows + q * nv:s * rows + (q + 1) * nv, cols] = (
                stage_out_ref[lt, q * pitch:q * pitch + nv, :])


def _even_kernel(*refs, n_streams, rows, sgu_n, emit_v, transposed_conv):
    (x_ref, hist_ref, g_ref, win_ref, lnvg_ref, lnvb_ref, ws_ref, bs_ref, wdw_ref, bdw_ref,
     lncg_ref, lncb_ref, wout_ref) = refs[:13]
    n_out = 3 if emit_v else 2
    y_ref = refs[13]
    av_ref = refs[14] if emit_v else None
    nb_ref = refs[13 + n_out - 1]
    if transposed_conv:
        stage_in_ref, stage_out_ref, prev_ref, cout_ref = refs[13 + n_out:]
    else:
        conv_ref, cout_ref = refs[13 + n_out:]
    d_model = x_ref.shape[-1]
    d_b = wdw_ref.shape[-1]
    d_a = d_b
    dg = d_a // G_A
    hist_rows = K_B - 1
    off = CONV_PAD - hist_rows

    if not transposed_conv:
        @pl.when(pl.program_id(1) == 0)
        def _():
            conv_ref[:, off:CONV_PAD, :] = hist_ref[...]

    lnvg = lnvg_ref[...]
    lnvb = lnvb_ref[...]
    tril = (lax.broadcasted_iota(jnp.int32, (sgu_n, sgu_n), 0)
            >= lax.broadcasted_iota(jnp.int32, (sgu_n, sgu_n), 1))
    ws = [jnp.where(tril, ws_ref[g], 0.0).astype(BF16) for g in range(G_A)]
    bs = bs_ref[...]
    wdw = wdw_ref[...]
    bdw = bdw_ref[...]

    m = n_streams * rows
    x = x_ref[...].reshape(m, d_model)
    xn = _rms(x, g_ref[...]).astype(BF16)
    z = _dot(xn, win_ref[...])

    za = _gelu_tanh(z[:, :2 * d_a])
    u = za[:, :d_a]
    v = jnp.concatenate(
        [_layer_norm(za[:, d_a + g * dg:d_a + (g + 1) * dg], lnvg[:, g * dg:(g + 1) * dg],
                     lnvb[:, g * dg:(g + 1) * dg]) for g in range(G_A)], axis=1)
    if emit_v:
        av_ref[...] = v.reshape(av_ref.shape)
    vb = v.astype(BF16)
    sg_rows = []
    for c in range(m // sgu_n):
        vc = vb[c * sgu_n:(c + 1) * sgu_n]
        sg_rows.append(jnp.concatenate(
            [_dot(ws[g], vc[:, g * dg:(g + 1) * dg]) for g in range(G_A)], axis=1) + bs)
    sg = sg_rows[0] if len(sg_rows) == 1 else jnp.concatenate(sg_rows, axis=0)
    y_a = u * sg

    glu = z[:, 2 * d_a:2 * d_a + d_b] * jax.nn.sigmoid(z[:, 2 * d_a + d_b:])
    for s in range(n_streams):
        if transposed_conv:
            _conv_rows_transposed(glu, s, hist_ref, wdw, bdw, stage_in_ref, stage_out_ref, prev_ref,
                                  cout_ref, rows=rows, first=pl.program_id(1) == 0)
            nb_ref[s] = glu[(s + 1) * rows - hist_rows:(s + 1) * rows]
            continue
        conv_ref[s, CONV_PAD:CONV_PAD + rows, :] = glu[s * rows:(s + 1) * rows]
        for rb in range(rows // CONV_ROW_BLOCK):
            r0 = rb * CONV_ROW_BLOCK
            acc = jnp.zeros((CONV_ROW_BLOCK, d_b), F32) + bdw
            for k in range(K_B):
                acc = acc + wdw[k:k + 1] * conv_ref[s, r0 + off + k:r0 + off + k + CONV_ROW_BLOCK, :]
            cout_ref[s * rows + r0:s * rows + r0 + CONV_ROW_BLOCK, :] = acc
        nb_ref[s] = conv_ref[s, rows + off:rows + CONV_PAD, :]
        conv_ref[s, 0:CONV_PAD, :] = conv_ref[s, rows:rows + CONV_PAD, :]
    y_b = jax.nn.silu(_layer_norm(cout_ref[...], lncg_ref[...], lncb_ref[...]))

    y = _dot(jnp.concatenate([y_a, y_b], axis=1).astype(BF16), wout_ref[...])
    y_ref[...] = (x + y).reshape(y_ref.shape)


def _even_layer(x, conv_hist, g_norm, w_in, ln_v_g, ln_v_b, w_s, b_s, w_dw, b_dw, ln_c_g, ln_c_b,
                w_out, *, n_streams, rows, emit_v):
    b, t, d = x.shape
    d_b = w_dw.shape[-1]
    d_a = ln_v_g.shape[-1]
    dg = d_a // G_A
    sgu_n = min(t, SGU_CHUNK)
    assert b % n_streams == 0 and t % rows == 0 and rows % sgu_n == 0 and (n_streams * rows) % sgu_n == 0
    assert rows % CONV_ROW_BLOCK == 0 and rows >= CONV_PAD and d_a == d_b
    bs_full = jnp.repeat(b_s[:, :sgu_n].T, dg, axis=1)
    args = [x, conv_hist, g_norm.reshape(1, d), w_in.astype(BF16), ln_v_g.reshape(1, d_a),
            ln_v_b.reshape(1, d_a), w_s[:, :sgu_n, :sgu_n], bs_full, w_dw, b_dw.reshape(1, d_b),
            ln_c_g.reshape(1, d_b), ln_c_b.reshape(1, d_b), w_out.astype(BF16)]
    in_specs = [
        pl.BlockSpec((n_streams, rows, d), lambda i, j: (i, j, 0)),
        pl.BlockSpec((n_streams, K_B - 1, d_b), lambda i, j: (i, 0, 0)),
    ] + [_const_spec(a.shape) for a in args[2:]]
    out_specs = [pl.BlockSpec((n_streams, rows, d), lambda i, j: (i, j, 0))]
    out_shape = [jax.ShapeDtypeStruct((b, t, d), F32)]
    if emit_v:
        out_specs.append(pl.BlockSpec((n_streams, rows, d_a), lambda i, j: (i, j, 0)))
        out_shape.append(jax.ShapeDtypeStruct((b, t, d_a), F32))
    out_specs.append(pl.BlockSpec((n_streams, K_B - 1, d_b), lambda i, j: (i, 0, 0)))
    out_shape.append(jax.ShapeDtypeStruct((b, K_B - 1, d_b), F32))
    transposed_conv = rows // V7X_SUBLANES >= K_B - 1
    kern = functools.partial(_even_kernel, n_streams=n_streams, rows=rows, sgu_n=sgu_n, emit_v=emit_v,
                             transposed_conv=transposed_conv)
    if transposed_conv:
        stage_shape = (d_b // V7X_LANES, V7X_SUBLANES * (rows // V7X_SUBLANES + CONV_PITCH_PAD), V7X_LANES)
        conv_scratch = [pltpu.VMEM(stage_shape, F32), pltpu.VMEM(stage_shape, F32),
                        pltpu.VMEM((n_streams, d_b // V7X_LANES, K_B - 1, V7X_SUBLANES, V7X_LANES), F32)]
    else:
        conv_scratch = [pltpu.VMEM((n_streams, CONV_PAD + rows, d_b), F32)]
    return pl.pallas_call(
        kern,
        grid=(b // n_streams, t // rows),
        in_specs=in_specs,
        out_specs=out_specs,
        out_shape=out_shape,
        scratch_shapes=conv_scratch + [pltpu.VMEM((n_streams * rows, d_b), F32)],
        compiler_params=pltpu.CompilerParams(
            dimension_semantics=("arbitrary", "arbitrary"),
            vmem_limit_bytes=V7X_VMEM_LIMIT_BYTES),
        name="even_layer",
    )(*args)


POOL_PAD = 16
ATTN_HALF_ROWS = 256
SEARCH_GROUP = 16
SEARCH_ROWS = 16
INT_MIN = -2 ** 31
NEG_INF_KEY = -2 ** 31 + 0x7FFFFF
LOG2E = math.log2(math.e)
M_INIT = -3.0e38


def _t5_bucket(rel):
    nb = N_BUCKETS // 2
    exact = nb // 2
    side = jnp.where(rel > 0, nb, 0)
    n = jnp.abs(rel)
    large = exact + (jnp.log(jnp.maximum(n, 1).astype(jnp.float32) / exact)
                     / math.log(MAX_DIST / exact) * (nb - exact)).astype(jnp.int32)
    large = jnp.minimum(large, nb - 1)
    return side + jnp.where(n < exact, n, large)


def _sortable(score):
    bits = pltpu.bitcast(score, jnp.int32)
    return bits ^ ((bits >> 31) & 0x7FFFFFFF)


def _for_range(lo, hi, body):
    if isinstance(lo, int) and isinstance(hi, int):
        for j in range(lo, hi):
            body(j)
    else:
        def step(j, carry):
            body(j)
            return carry
        lax.fori_loop(lo, hi, step, 0)


def _odd_kernel(*refs, rows, qb, hist_len, n_valid_hist, top, has_hist, single_step):
    refs = list(refs)
    x_ref = refs.pop(0)
    if has_hist:
        kht_ref, vht_ref, kiht_ref = refs[:3]
        kkt_ref, kk2t_ref, vvt_ref = refs[-3:]
        refs = refs[3:-3]
    (poolh_ref, g_ref, win_ref, wpool_ref, spool_ref, wout_ref, bucket_ref, rb_ref, tri_ref,
     y_ref, k_ref, v_ref, ki_ref, np_ref,
     kk_ref, kk2_ref, vv_ref, keys_ref, planes_ref, acc_ref, m_ref, off_ref, thr_ref, need_ref,
     bias_ref, pool_ref, yc_ref, yd_ref, qe_ref, qo_ref, ie_ref, io_ref, wb_ref) = refs
    d_c = H_C * HD_C
    d_qi = H_I * D_I
    d_d = wpool_ref.shape[0]
    n_pairs = H_C // 2
    n_ipairs = H_I // 2
    kt = KEY_TILE
    lanes = V7X_LANES
    t = pl.program_id(1)
    t0 = 0 if single_step else t * rows
    hist_tiles = hist_len // kt
    col_xd = d_c + d_qi

    def cat(parts, axis):
        return parts[0] if len(parts) == 1 else jnp.concatenate(parts, axis=axis)

    def is_hist(js):
        hist = [has_hist and j < hist_tiles for j in js]
        assert all(hist) or not any(hist)
        return hist[0]

    def dot_keys_t(lhs, nat_ref, t_ref, js):
        if is_hist(js):
            return _dot(lhs, cat([t_ref[j] for j in js], 1))
        return _dot_nt(lhs, cat([nat_ref[j] for j in js], 0))

    def dot_values(p, js):
        if is_hist(js):
            return _dot_nt(p, cat([vvt_ref[j] for j in js], 1))
        return _dot(p, cat([vv_ref[j] for j in js], 0))

    @pl.when(jnp.logical_and(pl.program_id(0) == 0, t == 0))
    def _():
        bucket = bucket_ref[...]
        for h in range(H_C):
            b_acc = jnp.zeros(bucket.shape, F32)
            for b in range(N_BUCKETS):
                b_acc = jnp.where(bucket == b, rb_ref[b, h] * LOG2E, b_acc)
            bias_ref[0:2, h] = b_acc
            bias_ref[2, h] = jnp.zeros(bucket.shape[1:], F32)

    @pl.when(t == 0)
    def _():
        pool_ref[POOL_PAD - POOL_HIST:POOL_PAD, :] = poolh_ref[0]
        if has_hist:
            ones_half = jnp.ones((lanes - HD_C, kt), F32)
            for j in range(hist_tiles):
                kj = kht_ref[0, :, j * kt:(j + 1) * kt]
                kij = kiht_ref[0, :, j * kt:(j + 1) * kt]
                vj = vht_ref[0, :, j * kt:(j + 1) * kt]
                kkt_ref[j] = jnp.concatenate([kj, kij], axis=0).astype(BF16)
                kk2t_ref[j] = jnp.concatenate([kij, kj], axis=0).astype(BF16)
                vvt_ref[j] = jnp.concatenate([vj, ones_half], axis=0).astype(BF16)

    x = x_ref[0]
    xn = _rms(x, g_ref[...]).astype(BF16)
    z_tail = _dot(xn, win_ref[:, col_xd:])
    z = _dot(xn, win_ref[:, :col_xd])

    xd = z_tail[:, :d_d]
    pool_ref[POOL_PAD:POOL_PAD + rows, :] = xd
    np_ref[0] = pool_ref[rows + POOL_PAD - POOL_HIST:rows + POOL_PAD, :]
    run = xd
    wins = {}
    for dshift in range(1, POOL_WINDOWS[-1]):
        run = run + pool_ref[POOL_PAD - dshift:POOL_PAD - dshift + rows, :]
        if dshift + 1 in POOL_WINDOWS:
            wins[dshift + 1] = run
    pool_ref[0:POOL_PAD, :] = pool_ref[rows:rows + POOL_PAD, :]
    dg_d = d_d // len(POOL_WINDOWS)
    lane_d = lax.broadcasted_iota(jnp.int32, (rows, d_d), 1)
    tpos = n_valid_hist + t0 + 1 + lax.broadcasted_iota(jnp.int32, (rows, d_d), 0)
    win_sum = wins[POOL_WINDOWS[-1]]
    width = jnp.full((rows, d_d), POOL_WINDOWS[-1], jnp.int32)
    for gi in range(len(POOL_WINDOWS) - 2, -1, -1):
        in_g = lane_d < (gi + 1) * dg_d
        win_sum = jnp.where(in_g, wins[POOL_WINDOWS[gi]], win_sum)
        width = jnp.where(in_g, POOL_WINDOWS[gi], width)
    count = jnp.minimum(tpos, width).astype(F32)
    m_pool = win_sum / count - xd
    yd_ref[...] = _dot(m_pool.astype(BF16), wpool_ref[...]) * spool_ref[...]

    g1 = z_tail[:, d_d:d_d + lanes]
    g2 = z_tail[:, d_d + lanes:]
    k_ref[0] = g1[:, :HD_C]
    v_ref[0] = g1[:, HD_C:]
    ki_ref[0] = g2[:, :D_I]
    lane = lax.broadcasted_iota(jnp.int32, (rows, lanes), 1)
    low = lane < HD_C
    g1r = pltpu.roll(g1, HD_C, 1)
    g2r = pltpu.roll(g2, D_I, 1)
    kk_new = jnp.where(low, g1, g2r).astype(BF16)
    kk2_new = jnp.where(low, g2, g1r).astype(BF16)
    vv_new = jnp.where(low, g1r, 1.0).astype(BF16)
    if rows % kt == 0:
        base_tile = (hist_len + t0) // kt
        for i in range(rows // kt):
            kk_ref[base_tile + i] = kk_new[i * kt:(i + 1) * kt]
            kk2_ref[base_tile + i] = kk2_new[i * kt:(i + 1) * kt]
            vv_ref[base_tile + i] = vv_new[i * kt:(i + 1) * kt]
    else:
        zpad = jnp.zeros((kt - rows, lanes), BF16)
        kk_ref[hist_tiles] = jnp.concatenate([kk_new, zpad], axis=0)
        kk2_ref[hist_tiles] = jnp.concatenate([kk2_new, zpad], axis=0)
        vv_ref[hist_tiles] = jnp.concatenate([vv_new, zpad], axis=0)
    kv_len = hist_len + t0 + rows

    w_idx = g2[:, D_I:D_I + H_I] * ((H_I ** -0.5) * (D_I ** -0.5))
    tri = tri_ref[...]
    ones_rhs = jnp.ones((kt, lanes), BF16)
    hb = min(rows, ATTN_HALF_ROWS)
    n_half = rows // hb
    n_new = max(rows // kt, 1)
    low_h = lax.broadcasted_iota(jnp.int32, (hb, lanes), 1) < HD_C

    jb = (hist_len + t0) // kt
    n_tiles = jb + n_new

    for hf in range(n_half):
        zr = z[hf * hb:(hf + 1) * hb]
        for g in range(n_pairs):
            grp = zr[:, g * lanes:(g + 1) * lanes] * (HD_C ** -0.5 * LOG2E)
            qe_ref[(hf * n_pairs + g) * hb:(hf * n_pairs + g + 1) * hb] = (
                jnp.where(low_h, grp, 0.0).astype(BF16))
            qo_ref[(hf * n_pairs + g) * hb:(hf * n_pairs + g + 1) * hb] = (
                jnp.where(low_h, 0.0, grp).astype(BF16))
    for g in range(n_ipairs):
        grp = z[:, d_c + g * lanes:d_c + (g + 1) * lanes]
        ie_ref[g * rows:(g + 1) * rows] = jnp.where(low, grp, 0.0).astype(BF16)
        io_ref[g * rows:(g + 1) * rows] = jnp.where(low, 0.0, grp).astype(BF16)
    for h in range(H_I):
        wb_ref[h] = jnp.broadcast_to(w_idx[:, h:h + 1], (rows, lanes))

    def score_body(j, r0=0):
        nr = rows - r0
        ie = jnp.concatenate([ie_ref[g * rows + r0:(g + 1) * rows] for g in range(n_ipairs)], axis=0)
        io = jnp.concatenate([io_ref[g * rows + r0:(g + 1) * rows] for g in range(n_ipairs)], axis=0)
        se = jnp.maximum(dot_keys_t(ie, kk2_ref, kk2t_ref if has_hist else None, [j]), 0.0)
        so = jnp.maximum(dot_keys_t(io, kk_ref, kkt_ref if has_hist else None, [j]), 0.0)
        score = jnp.zeros((nr, kt), F32)
        for g in range(n_ipairs):
            we = wb_ref[2 * g, r0:rows]
            wo = wb_ref[2 * g + 1, r0:rows]
            score = score + se[g * nr:(g + 1) * nr] * jnp.concatenate([we] * (kt // lanes), axis=1)
            score = score + so[g * nr:(g + 1) * nr] * jnp.concatenate([wo] * (kt // lanes), axis=1)
        qchunk = (hist_len + t0 + r0 + lax.broadcasted_iota(jnp.int32, (nr, 1), 0)) >> CHUNK_SHIFT
        kpos = lax.broadcasted_iota(jnp.int32, (nr, kt), 1) + j * kt
        adm = jnp.logical_and((kpos >> CHUNK_SHIFT) <= qchunk, kpos < kv_len)
        keys_ref[j, r0:rows] = _sortable(jnp.where(adm, score, NEG_INF))
        if r0:
            keys_ref[j, 0:r0] = jnp.full((r0, kt), NEG_INF_KEY, jnp.int32)

    def score_two(k):
        score_body(2 * k)
        score_body(2 * k + 1)
    _for_range(0, jb // 2, score_two)
    score_body(jb)
    for dj in range(1, n_new):
        first_half = min(hf for hf in range(n_half)
                         if ((hf + 1) * hb - qb) // lanes >= (kt // lanes) * dj)
        score_body(jb + dj, first_half * hb)

    def fill_body(j):
        keys_ref[j] = jnp.full((rows, kt), INT_MIN, jnp.int32)
    _for_range(n_tiles, keys_ref.shape[0], fill_body)

    n_groups = keys_ref.shape[0] * (kt // lanes) // SEARCH_GROUP

    def plane_body(rc, carry):
        r = pl.multiple_of(rc * SEARCH_ROWS, SEARCH_ROWS)
        for gi in range(n_groups):
            a = []
            for i in range(SEARCH_GROUP):
                lt = gi * SEARCH_GROUP + i
                a.append(keys_ref[lt // (kt // lanes), pl.ds(r, SEARCH_ROWS),
                                  (lt % (kt // lanes)) * lanes:(lt % (kt // lanes) + 1) * lanes])
            for sh, msk in ((8, 0x00FF00FF), (4, 0x0F0F0F0F), (2, 0x33333333), (1, 0x55555555)):
                for k in range(SEARCH_GROUP):
                    if k & sh == 0:
                        tmp = (a[k] ^ lax.shift_right_logical(a[k + sh], sh)) & msk
                        a[k] = a[k] ^ tmp
                        a[k + sh] = a[k + sh] ^ (tmp << sh)
            for w in range(SEARCH_GROUP):
                planes_ref[gi, w, pl.ds(r, SEARCH_ROWS), :] = a[w]
        return carry

    def plane_body_half(rc, carry):
        half = SEARCH_GROUP // 2
        r = pl.multiple_of(rc * SEARCH_ROWS, SEARCH_ROWS)
        a = [keys_ref[lt // (kt // lanes), pl.ds(r, SEARCH_ROWS),
                      (lt % (kt // lanes)) * lanes:(lt % (kt // lanes) + 1) * lanes] for lt in range(half)]
        for sh, msk in ((4, 0x0F0F0F0F), (2, 0x33333333), (1, 0x55555555)):
            for k in range(half):
                if k & sh == 0:
                    tmp = (a[k] ^ lax.shift_right_logical(a[k + sh], sh)) & msk
                    a[k] = a[k] ^ tmp
                    a[k + sh] = a[k + sh] ^ (tmp << sh)
        top_bytes = jnp.int32(-16711936)
        for w in range(half):
            word = a[w] & top_bytes
            if w == 0:
                word = word | jnp.int32(0x00FF0000)
            planes_ref[0, w, pl.ds(r, SEARCH_ROWS), :] = word
            planes_ref[0, w + half, pl.ds(r, SEARCH_ROWS), :] = (a[w] << 8) & top_bytes
        return carry

    if n_groups == 1 and not single_step:
        few_keys = n_tiles * (kt // lanes) <= SEARCH_GROUP // 2

        @pl.when(few_keys)
        def _():
            lax.fori_loop(0, rows // SEARCH_ROWS, plane_body_half, 0)

        @pl.when(jnp.logical_not(few_keys))
        def _():
            lax.fori_loop(0, rows // SEARCH_ROWS, plane_body, 0)
    else:
        lax.fori_loop(0, rows // SEARCH_ROWS, plane_body, 0)

    alive = [jnp.full((rows, lanes), -65536, jnp.int32) for _ in range(n_groups)]
    above = jnp.zeros((rows, 1), F32)
    thr_u = jnp.zeros((rows, 1), jnp.int32)
    for b in range(31, -1, -1):
        w = (31 - b) if b >= 16 else (15 - b)
        if b == 15:
            alive = [lax.shift_right_logical(a, 16) for a in alive]
        ones = []
        for gi in range(n_groups):
            plane = planes_ref[gi, w]
            if b == 31:
                plane = ~plane
            ones.append(alive[gi] & plane)
        pc = lax.population_count(ones[0])
        for gi in range(1, n_groups):
            pc = pc + lax.population_count(ones[gi])
        cnt = jnp.sum(pc.astype(F32), axis=1, keepdims=True)
        take = (above + cnt) >= float(top)
        alive = [jnp.where(take, o, a ^ o) for a, o in zip(alive, ones)]
        above = jnp.where(take, above, above + cnt)
        thr_u = jnp.where(take, thr_u | jnp.int32(INT_MIN if b == 31 else (1 << b)), thr_u)
    thr_ref[...] = jnp.broadcast_to(thr_u ^ jnp.int32(INT_MIN), (rows, lanes))
    need_ref[...] = jnp.broadcast_to(float(top) - above, (rows, lanes))
    off_ref[...] = jnp.zeros((rows, lanes), F32)
    m_ref[...] = jnp.full(m_ref.shape, M_INIT, F32)
    acc_ref[...] = jnp.zeros(acc_ref.shape, F32)

    def attn_half(tiles, hf):
        rs = slice(hf * hb, (hf + 1) * hb)
        js = [j for j, _ in tiles]
        n_lane_tiles = len(tiles) * (kt // lanes)
        thr_t = jnp.concatenate([thr_ref[rs]] * (kt // lanes), axis=1)
        need_t = jnp.concatenate([need_ref[rs]] * (kt // lanes), axis=1)
        masks = []
        for j in js:
            kj = keys_ref[j, rs]
            eq = jnp.where(kj == thr_t, 1.0, 0.0)
            eq_b = eq.astype(BF16)
            rank = _dot(eq_b, tri) + jnp.concatenate([off_ref[rs]] * (kt // lanes), axis=1)
            self_ = jnp.where(kj > thr_t, 1.0, jnp.where(rank <= need_t, eq, 0.0))
            sel = jnp.where(kj > NEG_INF_KEY, self_, 0.0) > 0.5
            masks.append(jnp.where(sel, 0.0, NEG_INF))
            off_ref[rs] = off_ref[rs] + _dot(eq_b, ones_rhs)
        mask_add = masks[0] if len(masks) == 1 else jnp.concatenate(masks, axis=1)
        q_rows = slice(hf * n_pairs * hb, (hf + 1) * n_pairs * hb)
        lg = (dot_keys_t(qe_ref[q_rows], kk_ref, kkt_ref if has_hist else None, js),
              dot_keys_t(qo_ref[q_rows], kk2_ref, kk2t_ref if has_hist else None, js))
        ps = []
        alphas = []
        for eo in range(2):
            for g in range(n_pairs):
                h = 2 * g + eo
                l = lg[eo][g * hb:(g + 1) * hb]
                if any(dj is not None for _, dj in tiles):
                    row_parts = []
                    for sub in range(hb // qb):
                        sblk = ((hf * (hb // qb) + sub) * qb) // lanes
                        parts = []
                        for ti, (_, dj) in enumerate(tiles):
                            for c in range(kt // lanes):
                                lane0 = ti * kt + c * lanes
                                part = l[sub * qb:(sub + 1) * qb, lane0:lane0 + lanes]
                                if dj is not None and (kt // lanes) * dj + c - sblk in (-1, 0):
                                    part = part + bias_ref[(kt // lanes) * dj + c - sblk + 1, h]
                                parts.append(part)
                        row_parts.append(jnp.concatenate(parts, axis=1))
                    l = row_parts[0] if len(row_parts) == 1 else jnp.concatenate(row_parts, axis=0)
                l = l + mask_add
                st = slice(((hf * 2 + eo) * n_pairs + g) * hb, ((hf * 2 + eo) * n_pairs + g + 1) * hb)
                m_old = m_ref[st]
                m_new = jnp.maximum(m_old, jnp.max(l, axis=1, keepdims=True))
                alphas.append(jnp.exp2(m_old - m_new))
                ps.append(jnp.exp2(l - jnp.concatenate([m_new] * n_lane_tiles, axis=1)).astype(BF16))
                m_ref[st] = m_new
        a_rows = slice(hf * H_C * hb, (hf + 1) * H_C * hb)
        acc_ref[a_rows] = (acc_ref[a_rows] * jnp.concatenate(alphas, axis=0)
                           + dot_values(jnp.concatenate(ps, axis=0), js))

    def far_pair(k):
        for hf in range(n_half):
            attn_half([(2 * k, None), (2 * k + 1, None)], hf)
    _for_range(0, jb // 2 - 1, far_pair)

    def last_old_pair():
        for hf in range(n_half):
            attn_half([(jb - 2, -2), (jb - 1, -1)], hf)
    if single_step:
        if jb >= 2:
            last_old_pair()
    else:
        pl.when(jb >= 2)(last_old_pair)

    for hf in range(n_half):
        new = [(jb + dj, dj) for dj in range(n_new)
               if ((hf + 1) * hb - qb) // lanes >= (kt // lanes) * dj]
        for i in range(0, len(new), 2):
            attn_half(new[i:i + 2], hf)
        for g in range(n_pairs):
            oe = acc_ref[((hf * 2) * n_pairs + g) * hb:((hf * 2) * n_pairs + g + 1) * hb]
            oo = acc_ref[((hf * 2 + 1) * n_pairs + g) * hb:((hf * 2 + 1) * n_pairs + g + 1) * hb]
            num = jnp.where(low_h, oe, pltpu.roll(oo, HD_C, 1))
            den = jnp.where(low_h, pltpu.roll(oe, HD_C, 1), oo)
            yc_ref[hf * hb:(hf + 1) * hb, g * lanes:(g + 1) * lanes] = num / den
        rs = slice(hf * hb, (hf + 1) * hb)
        y_cat = jnp.concatenate([yc_ref[rs], yd_ref[rs]], axis=1).astype(BF16)
        y_ref[0, rs, :] = x[rs] + _dot(y_cat, wout_ref[...])


def _odd_layer(x, k_hist, v_hist, ki_hist, pool_hist, n_valid_hist, g_norm, w_in, w_pool, s_pool,
               w_out, rel_bias, *, rows):
    b, t, d = x.shape
    hist_len = k_hist.shape[1]
    has_hist = hist_len > 0
    d_c = H_C * HD_C
    d_qi = H_I * D_I
    d_d = d - d_c
    qb = min(2 * CHUNK, rows)
    kt = KEY_TILE
    lanes = V7X_LANES
    s_total = hist_len + t
    top = min(TOPK_MAX, s_total // 4)
    assert t % rows == 0 and rows % qb == 0 and hist_len % kt == 0
    assert rows % kt == 0 or (rows == t and rows == qb and rows <= CHUNK)
    assert hist_len % (2 * kt) == 0 and (rows % (2 * kt) == 0 or rows == t)
    n_tiles = (s_total + kt - 1) // kt
    tiles_per_group = SEARCH_GROUP * lanes // kt
    n_groups = (n_tiles + tiles_per_group - 1) // tiles_per_group
    n_tiles_pad = n_groups * tiles_per_group
    assert rows % SEARCH_ROWS == 0 and rows % min(rows, ATTN_HALF_ROWS) == 0

    offs = np.cumsum([0, d_c, HD_C, HD_C, d_qi, D_I, H_I]).tolist()
    q_w, k_w, v_w, qi_w, ki_w, wi_w = (w_in[:, offs[i]:offs[i + 1]] for i in range(6))
    xd_w = w_in[:, offs[6]:]
    pad_w = jnp.zeros((d, lanes - D_I - H_I), w_in.dtype)
    w_all = jnp.concatenate([q_w, qi_w, xd_w, k_w, v_w, ki_w, wi_w, pad_w], axis=1).astype(BF16)
    wpool_bd = jax.scipy.linalg.block_diag(*[w_pool[g] for g in range(w_pool.shape[0])]).astype(BF16)

    rel = (lanes * jnp.arange(-1, 1, dtype=jnp.int32)[:, None, None]
           + jnp.arange(lanes, dtype=jnp.int32)[None, None, :]
           - jnp.arange(qb, dtype=jnp.int32)[None, :, None])
    bucket = _t5_bucket(rel)
    far_bucket = _t5_bucket(jnp.int32(-2 * lanes))
    rb_shift = rel_bias - rel_bias[far_bucket][None, :]
    tri = (jnp.arange(kt)[:, None] <= jnp.arange(kt)[None, :]).astype(BF16)

    args = [x]
    in_specs = [pl.BlockSpec((1, rows, d), lambda i, j: (i, j, 0))]
    if has_hist:
        assert rows == t, "history tiles are addressed statically"
        args += [jnp.swapaxes(k_hist, 1, 2), jnp.swapaxes(v_hist, 1, 2), jnp.swapaxes(ki_hist, 1, 2)]
        in_specs += [pl.BlockSpec((1, HD_C, hist_len), lambda i, j: (i, 0, 0))] * 2
        in_specs += [pl.BlockSpec((1, D_I, hist_len), lambda i, j: (i, 0, 0))]
    consts = [g_norm.reshape(1, d), w_all, wpool_bd, s_pool.reshape(1, d_d), w_out.astype(BF16), bucket]
    args += [pool_hist] + consts + [rb_shift, tri]
    in_specs += ([pl.BlockSpec((1, POOL_HIST, d_d), lambda i, j: (i, 0, 0))]
                 + [_const_spec(a.shape) for a in consts]
                 + [pl.BlockSpec(memory_space=pltpu.SMEM), _const_spec(tri.shape)])
    kern = functools.partial(_odd_kernel, rows=rows, qb=qb, hist_len=hist_len,
                             n_valid_hist=n_valid_hist, top=top, has_hist=has_hist,
                             single_step=(rows == t))
    return pl.pallas_call(
        kern,
        grid=(b, t // rows),
        in_specs=in_specs,
        out_specs=[
            pl.BlockSpec((1, rows, d), lambda i, j: (i, j, 0)),
            pl.BlockSpec((1, rows, HD_C), lambda i, j: (i, j, 0)),
            pl.BlockSpec((1, rows, HD_C), lambda i, j: (i, j, 0)),
            pl.BlockSpec((1, rows, D_I), lambda i, j: (i, j, 0)),
            pl.BlockSpec((1, POOL_HIST, d_d), lambda i, j: (i, 0, 0)),
        ],
        out_shape=[jax.ShapeDtypeStruct((b, t, d), F32),
                   jax.ShapeDtypeStruct((b, t, HD_C), F32),
                   jax.ShapeDtypeStruct((b, t, HD_C), F32),
                   jax.ShapeDtypeStruct((b, t, D_I), F32),
                   jax.ShapeDtypeStruct((b, POOL_HIST, d_d), F32)],
        scratch_shapes=[
            pltpu.VMEM((n_tiles, kt, lanes), BF16),
            pltpu.VMEM((n_tiles, kt, lanes), BF16),
            pltpu.VMEM((n_tiles, kt, lanes), BF16),
            pltpu.VMEM((n_tiles_pad, rows, kt), jnp.int32),
            pltpu.VMEM((n_groups, SEARCH_GROUP, rows, lanes), jnp.int32),
            pltpu.VMEM((H_C * rows, lanes), F32),
            pltpu.VMEM((H_C * rows, lanes), F32),
            pltpu.VMEM((rows, lanes), F32),
            pltpu.VMEM((rows, lanes), jnp.int32),
            pltpu.VMEM((rows, lanes), F32),
            pltpu.VMEM((3, H_C, qb, lanes), F32),
            pltpu.VMEM((POOL_PAD + rows, d_d), F32),
            pltpu.VMEM((rows, d_c), F32),
            pltpu.VMEM((rows, d_d), F32),
            pltpu.VMEM((H_C // 2 * rows, lanes), BF16),
            pltpu.VMEM((H_C // 2 * rows, lanes), BF16),
            pltpu.VMEM((H_I // 2 * rows, lanes), BF16),
            pltpu.VMEM((H_I // 2 * rows, lanes), BF16),
            pltpu.VMEM((H_I, rows, lanes), F32),
        ] + ([pltpu.VMEM((hist_len // kt, lanes, kt), BF16)] * 3 if has_hist else []),
        compiler_params=pltpu.CompilerParams(
            dimension_semantics=("arbitrary", "arbitrary"),
            vmem_limit_bytes=V7X_VMEM_LIMIT_BYTES),
        name="odd_layer",
    )(*args)


def kernel(x_prompt, x_sample, cache_b_conv, cache_c_k, cache_c_v, cache_c_kidx, cache_d_pool, cache_ffn_conv, ln_mix, ln_ffn, ln_final, e_w_in, e_ln_v_g, e_ln_v_b, e_w_s, e_b_s, e_w_dw, e_b_dw, e_ln_c_g, e_ln_c_b, e_w_out, o_w_in, o_w_pool, o_s_pool, o_w_out, rel_bias, f_w_up, f_w_dw, f_b_dw, f_w_down):
    hp, hs = x_prompt, x_sample
    bp, bs = x_prompt.shape[0], x_sample.shape[0]
    ts = x_sample.shape[1]
    depth = ln_mix.shape[0]
    d_ff = f_w_down.shape[1]
    a_s_l, b_p_l, b_s_l = [], [], []
    ck_p_l, cv_p_l, cki_p_l, ck_s_l, cv_s_l, cki_s_l, d_p_l, d_s_l = [], [], [], [], [], [], [], []
    f_p_l, f_s_l = [], []
    f_w_up_b = f_w_up.astype(BF16)
    f_w_down_b = f_w_down.astype(BF16)
    for layer in range(depth):
        i = layer // 2
        if layer % 2 == 0:
            ew = (ln_mix[layer], e_w_in[i], e_ln_v_g[i], e_ln_v_b[i], e_w_s[i], e_b_s[i], e_w_dw[i],
                  e_b_dw[i], e_ln_c_g[i], e_ln_c_b[i], e_w_out[i])
            hp, b_p = _even_layer(hp, jnp.zeros((bp, K_B - 1, e_w_dw.shape[-1]), F32), *ew,
                                  n_streams=1, rows=PROMPT_ROWS, emit_v=False)
            hs, a_s, b_s = _even_layer(hs, cache_b_conv[i], *ew, n_streams=bs, rows=ts, emit_v=True)
            a_s_l.append(a_s); b_p_l.append(b_p); b_s_l.append(b_s)
        else:
            ow = (ln_mix[layer], o_w_in[i], o_w_pool[i], o_s_pool[i], o_w_out[i], rel_bias)
            d_d = o_w_pool.shape[1] * o_w_pool.shape[2]
            hp, k_p, v_p, ki_p, d_p = _odd_layer(
                hp, jnp.zeros((bp, 0, HD_C), F32), jnp.zeros((bp, 0, HD_C), F32),
                jnp.zeros((bp, 0, D_I), F32), jnp.zeros((bp, POOL_HIST, d_d), F32), 0, *ow,
                rows=PROMPT_ROWS)
            hs, k_s, v_s, ki_s, d_s = _odd_layer(
                hs, cache_c_k[i], cache_c_v[i], cache_c_kidx[i], cache_d_pool[i], POOL_HIST, *ow,
                rows=ts)
            ck_p_l.append(k_p); cv_p_l.append(v_p); cki_p_l.append(ki_p)
            ck_s_l.append(k_s); cv_s_l.append(v_s); cki_s_l.append(ki_s)
            d_p_l.append(d_p); d_s_l.append(d_s)
        g_final = ln_final if layer == depth - 1 else None
        fw = (ln_ffn[layer], f_w_up_b, f_w_dw[layer], f_b_dw[layer], f_w_down_b, g_final, layer)
        hp, f_p = _conv_ffn(hp, jnp.zeros((bp, K_FFN - 1, d_ff), F32), *fw,
                            n_streams=1, rows=PROMPT_ROWS)
        hs, f_s = _conv_ffn(hs, cache_ffn_conv[layer], *fw, n_streams=bs, rows=ts)
        f_p_l.append(f_p); f_s_l.append(f_s)
    return (hp, hs,
            jnp.stack(a_s_l), jnp.stack(b_p_l), jnp.stack(b_s_l),
            jnp.stack(ck_p_l), jnp.stack(cv_p_l), jnp.stack(cki_p_l),
            jnp.stack(ck_s_l), jnp.stack(cv_s_l), jnp.stack(cki_s_l),
            jnp.stack(d_p_l), jnp.stack(d_s_l),
            jnp.stack(f_p_l), jnp.stack(f_s_l))
```

```python
import functools
import math

import jax
import jax.numpy as jnp
import numpy as np
from jax import lax
from jax.experimental import pallas as pl
from jax.experimental.pallas import tpu as pltpu

F32 = jnp.float32
BF16 = jnp.bfloat16

EPS = 1e-6
CHUNK = 64
CHUNK_SHIFT = CHUNK.bit_length() - 1
SGU_CHUNK = 128
G_A = 4
K_B = 31
H_C = 12
HD_C = 64
H_I = 8
D_I = 64
TOPK_MAX = 256
N_BUCKETS = 32
MAX_DIST = 128
POOL_WINDOWS = (2, 4, 8, 16)
POOL_HIST = 15
K_FFN = 3

V7X_LANES = 128
V7X_SUBLANES = 8
V7X_MXU_DIM = 256
V7X_VMEM_LIMIT_BYTES = 56 * 1024 * 1024

KEY_TILE = V7X_MXU_DIM
PROMPT_ROWS = 512
NEG_INF = float("-inf")


def _rms(x, g):
    return x * lax.rsqrt(jnp.mean(x * x, axis=-1, keepdims=True) + EPS) * g


def _layer_norm(x, g, b):
    mu = jnp.mean(x, axis=-1, keepdims=True)
    xc = x - mu
    return xc * lax.rsqrt(jnp.mean(xc * xc, axis=-1, keepdims=True) + EPS) * g + b


def _gelu_tanh(x):
    cdf = 0.5 * (1.0 + jnp.tanh(math.sqrt(2.0 / math.pi) * (x + 0.044715 * (x * x * x))))
    return x * cdf


def _dot(a, b):
    return jnp.dot(a, b, preferred_element_type=F32)


def _dot_nt(a, b):
    return lax.dot_general(a, b, (((1,), (1,)), ((), ())), preferred_element_type=F32)


def _const_spec(shape):
    nd = len(shape)
    return pl.BlockSpec(shape, lambda *_: (0,) * nd, pipeline_mode=pl.Buffered(1))


def _ffn_kernel(*refs, n_streams, rows, final_norm):
    if final_norm:
        (x_ref, xnext_ref, hist_ref, g_ref, wu_ref, wd_ref, dw_ref, bdw_ref, gf_ref,
         y_ref, nh_ref, carry_ref, xn_ref) = refs
    else:
        (x_ref, xnext_ref, hist_ref, g_ref, wu_ref, wd_ref, dw_ref, bdw_ref,
         y_ref, nh_ref, carry_ref, xn_ref) = refs
        gf_ref = None
    d_model = x_ref.shape[-1]
    d_ff = wd_ref.shape[0]
    m = n_streams * rows
    sub = V7X_SUBLANES
    n_hist = K_FFN - 1

    @pl.when(pl.program_id(1) == 0)
    def _():
        carry_ref[:, sub - n_hist:sub, :] = hist_ref[...]

    @pl.when(jnp.logical_and(pl.program_id(0) == 0, pl.program_id(1) == 0))
    def _():
        xn_ref[...] = _rms(x_ref[...].reshape(m, d_model), g_ref[...]).astype(BF16)

    x = x_ref[...].reshape(m, d_model)
    xn = xn_ref[...]
    xf_next = _rms(xnext_ref[...].reshape(m, d_model), g_ref[...])
    xn_ref[...] = xf_next.astype(BF16)
    bits = pltpu.bitcast(xf_next, jnp.int32)
    parts = [bits[i * sub:(i + 1) * sub] for i in range(m // sub)]
    while len(parts) > 1:
        parts = [parts[i] | parts[i + 1] for i in range(0, len(parts), 2)]
    lane_parts = [parts[0][:, c * V7X_LANES:(c + 1) * V7X_LANES] for c in range(d_model // V7X_LANES)]
    while len(lane_parts) > 1:
        lane_parts = [lane_parts[i] | lane_parts[i + 1] for i in range(0, len(lane_parts), 2)]
    pin = lax.shift_right_logical(lax.shift_right_logical(lane_parts[0], 16), 16).astype(F32)
    a = _dot(xn, wu_ref[:, :d_ff])
    val = _dot(xn, wu_ref[:, d_ff:])
    w = dw_ref[...]
    ys = []
    for s in range(n_streams):
        a_s = a[s * rows:(s + 1) * rows]
        ext = jnp.concatenate([carry_ref[s], a_s], axis=0)
        y_s = w[n_hist:K_FFN] * a_s
        for k in range(n_hist):
            y_s = y_s + w[k:k + 1] * ext[sub - n_hist + k:sub - n_hist + k + rows]
        ys.append(y_s)
        carry_ref[s] = a_s[rows - sub:rows]
        nh_ref[s] = a_s[rows - n_hist:rows]
    y = (ys[0] if n_streams == 1 else jnp.concatenate(ys, axis=0)) + bdw_ref[...]
    out = (x + pin[0:1, 0:1]) + _dot((_gelu_tanh(y) * val).astype(BF16), wd_ref[...])
    if final_norm:
        out = _rms(out, gf_ref[...])
    y_ref[...] = out.reshape(y_ref.shape)


def _layer_spec(shape, layer):
    nd = len(shape)
    return pl.BlockSpec((None,) + tuple(shape[1:]), lambda *_: (layer,) + (0,) * (nd - 1),
                        pipeline_mode=pl.Buffered(1))


def _conv_ffn(x, hist, g_norm, w_up_all, w_dw, b_dw, w_down_all, g_final, layer, *, n_streams, rows):
    b, t, d = x.shape
    d_ff = w_down_all.shape[1]
    assert d_ff % V7X_LANES == 0
    assert b % n_streams == 0 and t % rows == 0 and rows % V7X_SUBLANES == 0
    final_norm = g_final is not None
    n_i, n_j = b // n_streams, t // rows

    def next_block(i, j):
        f = jnp.minimum(i * n_j + j + 1, n_i * n_j - 1)
        return (f // n_j, f % n_j, 0)

    args = [x, x, hist, g_norm.reshape(1, d), w_up_all, w_down_all, w_dw, b_dw.reshape(1, d_ff)]
    in_specs = [
        pl.BlockSpec((n_streams, rows, d), lambda i, j: (i, j, 0)),
        pl.BlockSpec((n_streams, rows, d), next_block),
        pl.BlockSpec((n_streams, K_FFN - 1, d_ff), lambda i, j: (i, 0, 0)),
        _const_spec((1, d)),
        _layer_spec(w_up_all.shape, layer), _layer_spec(w_down_all.shape, layer),
        _const_spec(w_dw.shape), _const_spec((1, d_ff)),
    ]
    if final_norm:
        args.append(g_final.reshape(1, d))
        in_specs.append(_const_spec((1, d)))
    kern = functools.partial(_ffn_kernel, n_streams=n_streams, rows=rows, final_norm=final_norm)
    return pl.pallas_call(
        kern,
        grid=(b // n_streams, t // rows),
        in_specs=in_specs,
        out_specs=[
            pl.BlockSpec((n_streams, rows, d), lambda i, j: (i, j, 0)),
            pl.BlockSpec((n_streams, K_FFN - 1, d_ff), lambda i, j: (i, 0, 0)),
        ],
        out_shape=[jax.ShapeDtypeStruct((b, t, d), F32),
                   jax.ShapeDtypeStruct((b, K_FFN - 1, d_ff), F32)],
        scratch_shapes=[pltpu.VMEM((n_streams, V7X_SUBLANES, d_ff), F32),
                        pltpu.VMEM((n_streams * rows, d), BF16)],
        compiler_params=pltpu.CompilerParams(
            dimension_semantics=("arbitrary", "arbitrary"),
            vmem_limit_bytes=V7X_VMEM_LIMIT_BYTES),
        name="conv_ffn",
    )(*args)


CONV_PAD = 32
CONV_ROW_BLOCK = 32
CONV_PITCH_PAD = 4


def _conv_rows_transposed(glu, s, hist_ref, wdw, bdw, stage_in_ref, stage_out_ref, prev_ref, cout_ref,
                          *, rows, first):
    sub = V7X_SUBLANES
    lanes = V7X_LANES
    nv = rows // sub
    pitch = nv + CONV_PITCH_PAD
    hist_rows = K_B - 1
    d_b = glu.shape[-1]
    sub_id = lax.broadcasted_iota(jnp.int32, (sub, lanes), 0)
    for lt in range(d_b // lanes):
        cols = slice(lt * lanes, (lt + 1) * lanes)

        @pl.when(first)
        def _():
            for e in range(hist_rows):
                prev_ref[s, lt, e, sub - 1:sub, :] = hist_ref[s, e:e + 1, cols]
        for q in range(sub):
            stage_in_ref[lt, q * pitch:q * pitch + nv, :] = (
                glu[s * rows + q * nv:s * rows + (q + 1) * nv, cols])
        cur = [stage_in_ref[lt, pl.ds(v, sub, stride=pitch), :] for v in range(nv)]
        head = []
        for e in range(hist_rows):
            merged = jnp.where(sub_id == sub - 1, prev_ref[s, lt, e], cur[nv - hist_rows + e])
            head.append(pltpu.roll(merged, 1, 0))
        for e in range(hist_rows):
            prev_ref[s, lt, e] = cur[nv - hist_rows + e]
        ext = head + cur
        wk = [jnp.broadcast_to(wdw[k:k + 1, cols], (sub, lanes)) for k in range(K_B)]
        bias = jnp.broadcast_to(bdw[:, cols], (sub, lanes))
        for v in range(nv):
            acc = bias
            for k in range(K_B):
                acc = acc + wk[k] * ext[v + k]
            stage_out_ref[lt, pl.ds(v, sub, stride=pitch), :] = acc
        for q in range(sub):
            cout_ref[s * rows + q * nv:s * rows + (q + 1) * nv, cols] = (
                stage_out_ref[lt, q * pitch:q * pitch + nv, :])


def _even_kernel(*refs, n_streams, rows, sgu_n, emit_v, transposed_conv):
    (x_ref, hist_ref, g_ref, win_ref, lnvg_ref, lnvb_ref, ws_ref, bs_ref, wdw_ref, bdw_ref,
     lncg_ref, lncb_ref, wout_ref) = refs[:13]
    n_out = 3 if emit_v else 2
    y_ref = refs[13]
    av_ref = refs[14] if emit_v else None
    nb_ref = refs[13 + n_out - 1]
    if transposed_conv:
        stage_in_ref, stage_out_ref, prev_ref, cout_ref = refs[13 + n_out:]
    else:
        conv_ref, cout_ref = refs[13 + n_out:]
    d_model = x_ref.shape[-1]
    d_b = wdw_ref.shape[-1]
    d_a = d_b
    dg = d_a // G_A
    hist_rows = K_B - 1
    off = CONV_PAD - hist_rows

    if not transposed_conv:
        @pl.when(pl.program_id(1) == 0)
        def _():
            conv_ref[:, off:CONV_PAD, :] = hist_ref[...]

    lnvg = lnvg_ref[...]
    lnvb = lnvb_ref[...]
    tril = (lax.broadcasted_iota(jnp.int32, (sgu_n, sgu_n), 0)
            >= lax.broadcasted_iota(jnp.int32, (sgu_n, sgu_n), 1))
    ws = [jnp.where(tril, ws_ref[g], 0.0).astype(BF16) for g in range(G_A)]
    bs = bs_ref[...]
    wdw = wdw_ref[...]
    bdw = bdw_ref[...]

    m = n_streams * rows
    x = x_ref[...].reshape(m, d_model)
    xn = _rms(x, g_ref[...]).astype(BF16)
    z = _dot(xn, win_ref[...])

    za = _gelu_tanh(z[:, :2 * d_a])
    u = za[:, :d_a]
    v = jnp.concatenate(
        [_layer_norm(za[:, d_a + g * dg:d_a + (g + 1) * dg], lnvg[:, g * dg:(g + 1) * dg],
                     lnvb[:, g * dg:(g + 1) * dg]) for g in range(G_A)], axis=1)
    if emit_v:
        av_ref[...] = v.reshape(av_ref.shape)
    vb = v.astype(BF16)
    sg_rows = []
    for c in range(m // sgu_n):
        vc = vb[c * sgu_n:(c + 1) * sgu_n]
        sg_rows.append(jnp.concatenate(
            [_dot(ws[g], vc[:, g * dg:(g + 1) * dg]) for g in range(G_A)], axis=1) + bs)
    sg = sg_rows[0] if len(sg_rows) == 1 else jnp.concatenate(sg_rows, axis=0)
    y_a = u * sg

    glu = z[:, 2 * d_a:2 * d_a + d_b] * jax.nn.sigmoid(z[:, 2 * d_a + d_b:])
    for s in range(n_streams):
        if transposed_conv:
            _conv_rows_transposed(glu, s, hist_ref, wdw, bdw, stage_in_ref, stage_out_ref, prev_ref,
                                  cout_ref, rows=rows, first=pl.program_id(1) == 0)
            nb_ref[s] = glu[(s + 1) * rows - hist_rows:(s + 1) * rows]
            continue
        conv_ref[s, CONV_PAD:CONV_PAD + rows, :] = glu[s * rows:(s + 1) * rows]
        for rb in range(rows // CONV_ROW_BLOCK):
            r0 = rb * CONV_ROW_BLOCK
            acc = jnp.zeros((CONV_ROW_BLOCK, d_b), F32) + bdw
            for k in range(K_B):
                acc = acc + wdw[k:k + 1] * conv_ref[s, r0 + off + k:r0 + off + k + CONV_ROW_BLOCK, :]
            cout_ref[s * rows + r0:s * rows + r0 + CONV_ROW_BLOCK, :] = acc
        nb_ref[s] = conv_ref[s, rows + off:rows + CONV_PAD, :]
        conv_ref[s, 0:CONV_PAD, :] = conv_ref[s, rows:rows + CONV_PAD, :]
    y_b = jax.nn.silu(_layer_norm(cout_ref[...], lncg_ref[...], lncb_ref[...]))

    y = _dot(jnp.concatenate([y_a, y_b], axis=1).astype(BF16), wout_ref[...])
    y_ref[...] = (x + y).reshape(y_ref.shape)


def _even_layer(x, conv_hist, g_norm, w_in, ln_v_g, ln_v_b, w_s, b_s, w_dw, b_dw, ln_c_g, ln_c_b,
                w_out, *, n_streams, rows, emit_v):
    b, t, d = x.shape
    d_b = w_dw.shape[-1]
    d_a = ln_v_g.shape[-1]
    dg = d_a // G_A
    sgu_n = min(t, SGU_CHUNK)
    assert b % n_streams == 0 and t % rows == 0 and rows % sgu_n == 0 and (n_streams * rows) % sgu_n == 0
    assert rows % CONV_ROW_BLOCK == 0 and rows >= CONV_PAD and d_a == d_b
    bs_full = jnp.repeat(b_s[:, :sgu_n].T, dg, axis=1)
    args = [x, conv_hist, g_norm.reshape(1, d), w_in.astype(BF16), ln_v_g.reshape(1, d_a),
            ln_v_b.reshape(1, d_a), w_s[:, :sgu_n, :sgu_n], bs_full, w_dw, b_dw.reshape(1, d_b),
            ln_c_g.reshape(1, d_b), ln_c_b.reshape(1, d_b), w_out.astype(BF16)]
    in_specs = [
        pl.BlockSpec((n_streams, rows, d), lambda i, j: (i, j, 0)),
        pl.BlockSpec((n_streams, K_B - 1, d_b), lambda i, j: (i, 0, 0)),
    ] + [_const_spec(a.shape) for a in args[2:]]
    out_specs = [pl.BlockSpec((n_streams, rows, d), lambda i, j: (i, j, 0))]
    out_shape = [jax.ShapeDtypeStruct((b, t, d), F32)]
    if emit_v:
        out_specs.append(pl.BlockSpec((n_streams, rows, d_a), lambda i, j: (i, j, 0)))
        out_shape.append(jax.ShapeDtypeStruct((b, t, d_a), F32))
    out_specs.append(pl.BlockSpec((n_streams, K_B - 1, d_b), lambda i, j: (i, 0, 0)))
    out_shape.append(jax.ShapeDtypeStruct((b, K_B - 1, d_b), F32))
    transposed_conv = rows // V7X_SUBLANES >= K_B - 1
    kern = functools.partial(_even_kernel, n_streams=n_streams, rows=rows, sgu_n=sgu_n, emit_v=emit_v,
                             transposed_conv=transposed_conv)
    if transposed_conv:
        stage_shape = (d_b // V7X_LANES, V7X_SUBLANES * (rows // V7X_SUBLANES + CONV_PITCH_PAD), V7X_LANES)
        conv_scratch = [pltpu.VMEM(stage_shape, F32), pltpu.VMEM(stage_shape, F32),
                        pltpu.VMEM((n_streams, d_b // V7X_LANES, K_B - 1, V7X_SUBLANES, V7X_LANES), F32)]
    else:
        conv_scratch = [pltpu.VMEM((n_streams, CONV_PAD + rows, d_b), F32)]
    return pl.pallas_call(
        kern,
        grid=(b // n_streams, t // rows),
        in_specs=in_specs,
        out_specs=out_specs,
        out_shape=out_shape,
        scratch_shapes=conv_scratch + [pltpu.VMEM((n_streams * rows, d_b), F32)],
        compiler_params=pltpu.CompilerParams(
            dimension_semantics=("arbitrary", "arbitrary"),
            vmem_limit_bytes=V7X_VMEM_LIMIT_BYTES),
        name="even_layer",
    )(*args)


POOL_PAD = 16
ATTN_HALF_ROWS = 256
SEARCH_GROUP = 16
SEARCH_ROWS = 16
INT_MIN = -2 ** 31
NEG_INF_KEY = -2 ** 31 + 0x7FFFFF
LOG2E = math.log2(math.e)
M_INIT = -3.0e38


def _t5_bucket(rel):
    nb = N_BUCKETS // 2
    exact = nb // 2
    side = jnp.where(rel > 0, nb, 0)
    n = jnp.abs(rel)
    large = exact + (jnp.log(jnp.maximum(n, 1).astype(jnp.float32) / exact)
                     / math.log(MAX_DIST / exact) * (nb - exact)).astype(jnp.int32)
    large = jnp.minimum(large, nb - 1)
    return side + jnp.where(n < exact, n, large)


def _sortable(score):
    bits = pltpu.bitcast(score, jnp.int32)
    return bits ^ ((bits >> 31) & 0x7FFFFFFF)


def _for_range(lo, hi, body):
    if isinstance(lo, int) and isinstance(hi, int):
        for j in range(lo, hi):
            body(j)
    else:
        def step(j, carry):
            body(j)
            return carry
        lax.fori_loop(lo, hi, step, 0)


def _odd_kernel(*refs, rows, qb, hist_len, n_valid_hist, top, has_hist, single_step):
    refs = list(refs)
    x_ref = refs.pop(0)
    if has_hist:
        kht_ref, vht_ref, kiht_ref = refs[:3]
        kkt_ref, kk2t_ref, vvt_ref = refs[-3:]
        refs = refs[3:-3]
    (poolh_ref, g_ref, win_ref, wpool_ref, spool_ref, wout_ref, bucket_ref, rb_ref, tri_ref,
     y_ref, k_ref, v_ref, ki_ref, np_ref,
     kk_ref, kk2_ref, vv_ref, keys_ref, planes_ref, acc_ref, m_ref, off_ref, thr_ref, need_ref,
     bias_ref, pool_ref, yc_ref, yd_ref, qe_ref, qo_ref, ie_ref, io_ref, wb_ref) = refs
    d_c = H_C * HD_C
    d_qi = H_I * D_I
    d_d = wpool_ref.shape[0]
    n_pairs = H_C // 2
    n_ipairs = H_I // 2
    kt = KEY_TILE
    lanes = V7X_LANES
    t = pl.program_id(1)
    t0 = 0 if single_step else t * rows
    hist_tiles = hist_len // kt
    col_xd = d_c + d_qi

    def cat(parts, axis):
        return parts[0] if len(parts) == 1 else jnp.concatenate(parts, axis=axis)

    def is_hist(js):
        hist = [has_hist and j < hist_tiles for j in js]
        assert all(hist) or not any(hist)
        return hist[0]

    def dot_keys_t(lhs, nat_ref, t_ref, js):
        if is_hist(js):
            return _dot(lhs, cat([t_ref[j] for j in js], 1))
        return _dot_nt(lhs, cat([nat_ref[j] for j in js], 0))

    def dot_values(p, js):
        if is_hist(js):
            return _dot_nt(p, cat([vvt_ref[j] for j in js], 1))
        return _dot(p, cat([vv_ref[j] for j in js], 0))

    @pl.when(jnp.logical_and(pl.program_id(0) == 0, t == 0))
    def _():
        bucket = bucket_ref[...]
        for h in range(H_C):
            b_acc = jnp.zeros(bucket.shape, F32)
            for b in range(N_BUCKETS):
                b_acc = jnp.where(bucket == b, rb_ref[b, h] * LOG2E, b_acc)
            bias_ref[0:2, h] = b_acc
            bias_ref[2, h] = jnp.zeros(bucket.shape[1:], F32)

    @pl.when(t == 0)
    def _():
        pool_ref[POOL_PAD - POOL_HIST:POOL_PAD, :] = poolh_ref[0]
        if has_hist:
            ones_half = jnp.ones((lanes - HD_C, kt), F32)
            for j in range(hist_tiles):
                kj = kht_ref[0, :, j * kt:(j + 1) * kt]
                kij = kiht_ref[0, :, j * kt:(j + 1) * kt]
                vj = vht_ref[0, :, j * kt:(j + 1) * kt]
                kkt_ref[j] = jnp.concatenate([kj, kij], axis=0).astype(BF16)
                kk2t_ref[j] = jnp.concatenate([kij, kj], axis=0).astype(BF16)
                vvt_ref[j] = jnp.concatenate([vj, ones_half], axis=0).astype(BF16)

    x = x_ref[0]
    xn = _rms(x, g_ref[...]).astype(BF16)
    z_tail = _dot(xn, win_ref[:, col_xd:])
    z = _dot(xn, win_ref[:, :col_xd])

    xd = z_tail[:, :d_d]
    pool_ref[POOL_PAD:POOL_PAD + rows, :] = xd
    np_ref[0] = pool_ref[rows + POOL_PAD - POOL_HIST:rows + POOL_PAD, :]
    run = xd
    wins = {}
    for dshift in range(1, POOL_WINDOWS[-1]):
        run = run + pool_ref[POOL_PAD - dshift:POOL_PAD - dshift + rows, :]
        if dshift + 1 in POOL_WINDOWS:
            wins[dshift + 1] = run
    pool_ref[0:POOL_PAD, :] = pool_ref[rows:rows + POOL_PAD, :]
    dg_d = d_d // len(POOL_WINDOWS)
    lane_d = lax.broadcasted_iota(jnp.int32, (rows, d_d), 1)
    tpos = n_valid_hist + t0 + 1 + lax.broadcasted_iota(jnp.int32, (rows, d_d), 0)
    win_sum = wins[POOL_WINDOWS[-1]]
    width = jnp.full((rows, d_d), POOL_WINDOWS[-1], jnp.int32)
    for gi in range(len(POOL_WINDOWS) - 2, -1, -1):
        in_g = lane_d < (gi + 1) * dg_d
        win_sum = jnp.where(in_g, wins[POOL_WINDOWS[gi]], win_sum)
        width = jnp.where(in_g, POOL_WINDOWS[gi], width)
    count = jnp.minimum(tpos, width).astype(F32)
    m_pool = win_sum / count - xd
    yd_ref[...] = _dot(m_pool.astype(BF16), wpool_ref[...]) * spool_ref[...]

    g1 = z_tail[:, d_d:d_d + lanes]
    g2 = z_tail[:, d_d + lanes:]
    k_ref[0] = g1[:, :HD_C]
    v_ref[0] = g1[:, HD_C:]
    ki_ref[0] = g2[:, :D_I]
    lane = lax.broadcasted_iota(jnp.int32, (rows, lanes), 1)
    low = lane < HD_C
    g1r = pltpu.roll(g1, HD_C, 1)
    g2r = pltpu.roll(g2, D_I, 1)
    kk_new = jnp.where(low, g1, g2r).astype(BF16)
    kk2_new = jnp.where(low, g2, g1r).astype(BF16)
    vv_new = jnp.where(low, g1r, 1.0).astype(BF16)
    if rows % kt == 0:
        base_tile = (hist_len + t0) // kt
        for i in range(rows // kt):
            kk_ref[base_tile + i] = kk_new[i * kt:(i + 1) * kt]
            kk2_ref[base_tile + i] = kk2_new[i * kt:(i + 1) * kt]
            vv_ref[base_tile + i] = vv_new[i * kt:(i + 1) * kt]
    else:
        zpad = jnp.zeros((kt - rows, lanes), BF16)
        kk_ref[hist_tiles] = jnp.concatenate([kk_new, zpad], axis=0)
        kk2_ref[hist_tiles] = jnp.concatenate([kk2_new, zpad], axis=0)
        vv_ref[hist_tiles] = jnp.concatenate([vv_new, zpad], axis=0)
    kv_len = hist_len + t0 + rows

    w_idx = g2[:, D_I:D_I + H_I] * ((H_I ** -0.5) * (D_I ** -0.5))
    tri = tri_ref[...]
    ones_rhs = jnp.ones((kt, lanes), BF16)
    hb = min(rows, ATTN_HALF_ROWS)
    n_half = rows // hb
    n_new = max(rows // kt, 1)
    low_h = lax.broadcasted_iota(jnp.int32, (hb, lanes), 1) < HD_C

    jb = (hist_len + t0) // kt
    n_tiles = jb + n_new

    for hf in range(n_half):
        zr = z[hf * hb:(hf + 1) * hb]
        for g in range(n_pairs):
            grp = zr[:, g * lanes:(g + 1) * lanes] * (HD_C ** -0.5 * LOG2E)
            qe_ref[(hf * n_pairs + g) * hb:(hf * n_pairs + g + 1) * hb] = (
                jnp.where(low_h, grp, 0.0).astype(BF16))
            qo_ref[(hf * n_pairs + g) * hb:(hf * n_pairs + g + 1) * hb] = (
                jnp.where(low_h, 0.0, grp).astype(BF16))
    for g in range(n_ipairs):
        grp = z[:, d_c + g * lanes:d_c + (g + 1) * lanes]
        ie_ref[g * rows:(g + 1) * rows] = jnp.where(low, grp, 0.0).astype(BF16)
        io_ref[g * rows:(g + 1) * rows] = jnp.where(low, 0.0, grp).astype(BF16)
    for h in range(H_I):
        wb_ref[h] = jnp.broadcast_to(w_idx[:, h:h + 1], (rows, lanes))

    def score_body(j, r0=0):
        nr = rows - r0
        ie = jnp.concatenate([ie_ref[g * rows + r0:(g + 1) * rows] for g in range(n_ipairs)], axis=0)
        io = jnp.concatenate([io_ref[g * rows + r0:(g + 1) * rows] for g in range(n_ipairs)], axis=0)
        se = jnp.maximum(dot_keys_t(ie, kk2_ref, kk2t_ref if has_hist else None, [j]), 0.0)
        so = jnp.maximum(dot_keys_t(io, kk_ref, kkt_ref if has_hist else None, [j]), 0.0)
        score = jnp.zeros((nr, kt), F32)
        for g in range(n_ipairs):
            we = wb_ref[2 * g, r0:rows]
            wo = wb_ref[2 * g + 1, r0:rows]
            score = score + se[g * nr:(g + 1) * nr] * jnp.concatenate([we] * (kt // lanes), axis=1)
            score = score + so[g * nr:(g + 1) * nr] * jnp.concatenate([wo] * (kt // lanes), axis=1)
        qchunk = (hist_len + t0 + r0 + lax.broadcasted_iota(jnp.int32, (nr, 1), 0)) >> CHUNK_SHIFT
        kpos = lax.broadcasted_iota(jnp.int32, (nr, kt), 1) + j * kt
        adm = jnp.logical_and((kpos >> CHUNK_SHIFT) <= qchunk, kpos < kv_len)
        keys_ref[j, r0:rows] = _sortable(jnp.where(adm, score, NEG_INF))
        if r0:
            keys_ref[j, 0:r0] = jnp.full((r0, kt), NEG_INF_KEY, jnp.int32)

    def score_two(k):
        score_body(2 * k)
        score_body(2 * k + 1)
    _for_range(0, jb // 2, score_two)
    score_body(jb)
    for dj in range(1, n_new):
        first_half = min(hf for hf in range(n_half)
                         if ((hf + 1) * hb - qb) // lanes >= (kt // lanes) * dj)
        score_body(jb + dj, first_half * hb)

    def fill_body(j):
        keys_ref[j] = jnp.full((rows, kt), INT_MIN, jnp.int32)
    _for_range(n_tiles, keys_ref.shape[0], fill_body)

    n_groups = keys_ref.shape[0] * (kt // lanes) // SEARCH_GROUP

    def plane_body(rc, carry):
        r = pl.multiple_of(rc * SEARCH_ROWS, SEARCH_ROWS)
        for gi in range(n_groups):
            a = []
            for i in range(SEARCH_GROUP):
                lt = gi * SEARCH_GROUP + i
                a.append(keys_ref[lt // (kt // lanes), pl.ds(r, SEARCH_ROWS),
                                  (lt % (kt // lanes)) * lanes:(lt % (kt // lanes) + 1) * lanes])
            for sh, msk in ((8, 0x00FF00FF), (4, 0x0F0F0F0F), (2, 0x33333333), (1, 0x55555555)):
                for k in range(SEARCH_GROUP):
                    if k & sh == 0:
                        tmp = (a[k] ^ lax.shift_right_logical(a[k + sh], sh)) & msk
                        a[k] = a[k] ^ tmp
                        a[k + sh] = a[k + sh] ^ (tmp << sh)
            for w in range(SEARCH_GROUP):
                planes_ref[gi, w, pl.ds(r, SEARCH_ROWS), :] = a[w]
        return carry

    def plane_body_half(rc, carry):
        half = SEARCH_GROUP // 2
        r = pl.multiple_of(rc * SEARCH_ROWS, SEARCH_ROWS)
        a = [keys_ref[lt // (kt // lanes), pl.ds(r, SEARCH_ROWS),
                      (lt % (kt // lanes)) * lanes:(lt % (kt // lanes) + 1) * lanes] for lt in range(half)]
        for sh, msk in ((4, 0x0F0F0F0F), (2, 0x33333333), (1, 0x55555555)):
            for k in range(half):
                if k & sh == 0:
                    tmp = (a[k] ^ lax.shift_right_logical(a[k + sh], sh)) & msk
                    a[k] = a[k] ^ tmp
                    a[k + sh] = a[k + sh] ^ (tmp << sh)
        top_bytes = jnp.int32(-16711936)
        for w in range(half):
            word = a[w] & top_bytes
            if w == 0:
                word = word | jnp.int32(0x00FF0000)
            planes_ref[0, w, pl.ds(r, SEARCH_ROWS), :] = word
            planes_ref[0, w + half, pl.ds(r, SEARCH_ROWS), :] = (a[w] << 8) & top_bytes
        return carry

    if n_groups == 1 and not single_step:
        few_keys = n_tiles * (kt // lanes) <= SEARCH_GROUP // 2

        @pl.when(few_keys)
        def _():
            lax.fori_loop(0, rows // SEARCH_ROWS, plane_body_half, 0)

        @pl.when(jnp.logical_not(few_keys))
        def _():
            lax.fori_loop(0, rows // SEARCH_ROWS, plane_body, 0)
    else:
        lax.fori_loop(0, rows // SEARCH_ROWS, plane_body, 0)

    alive = [jnp.full((rows, lanes), -65536, jnp.int32) for _ in range(n_groups)]
    above = jnp.zeros((rows, 1), F32)
    thr_u = jnp.zeros((rows, 1), jnp.int32)
    for b in range(31, -1, -1):
        w = (31 - b) if b >= 16 else (15 - b)
        if b == 15:
            alive = [lax.shift_right_logical(a, 16) for a in alive]
        ones = []
        for gi in range(n_groups):
            plane = planes_ref[gi, w]
            if b == 31:
                plane = ~plane
            ones.append(alive[gi] & plane)
        pc = lax.population_count(ones[0])
        for gi in range(1, n_groups):
            pc = pc + lax.population_count(ones[gi])
        cnt = jnp.sum(pc.astype(F32), axis=1, keepdims=True)
        take = (above + cnt) >= float(top)
        alive = [jnp.where(take, o, a ^ o) for a, o in zip(alive, ones)]
        above = jnp.where(take, above, above + cnt)
        thr_u = jnp.where(take, thr_u | jnp.int32(INT_MIN if b == 31 else (1 << b)), thr_u)
    thr_ref[...] = jnp.broadcast_to(thr_u ^ jnp.int32(INT_MIN), (rows, lanes))
    need_ref[...] = jnp.broadcast_to(float(top) - above, (rows, lanes))
    off_ref[...] = jnp.zeros((rows, lanes), F32)
    m_ref[...] = jnp.full(m_ref.shape, M_INIT, F32)
    acc_ref[...] = jnp.zeros(acc_ref.shape, F32)

    def attn_half(tiles, hf):
        rs = slice(hf * hb, (hf + 1) * hb)
        js = [j for j, _ in tiles]
        n_lane_tiles = len(tiles) * (kt // lanes)
        thr_t = jnp.concatenate([thr_ref[rs]] * (kt // lanes), axis=1)
        need_t = jnp.concatenate([need_ref[rs]] * (kt // lanes), axis=1)
        masks = []
        for j in js:
            kj = keys_ref[j, rs]
            eq = jnp.where(kj == thr_t, 1.0, 0.0)
            eq_b = eq.astype(BF16)
            rank = _dot(eq_b, tri) + jnp.concatenate([off_ref[rs]] * (kt // lanes), axis=1)
            self_ = jnp.where(kj > thr_t, 1.0, jnp.where(rank <= need_t, eq, 0.0))
            sel = jnp.where(kj > NEG_INF_KEY, self_, 0.0) > 0.5
            masks.append(jnp.where(sel, 0.0, NEG_INF))
            off_ref[rs] = off_ref[rs] + _dot(eq_b, ones_rhs)
        mask_add = masks[0] if len(masks) == 1 else jnp.concatenate(masks, axis=1)
        q_rows = slice(hf * n_pairs * hb, (hf + 1) * n_pairs * hb)
        lg = (dot_keys_t(qe_ref[q_rows], kk_ref, kkt_ref if has_hist else None, js),
              dot_keys_t(qo_ref[q_rows], kk2_ref, kk2t_ref if has_hist else None, js))
        ps = []
        alphas = []
        for eo in range(2):
            for g in range(n_pairs):
                h = 2 * g + eo
                l = lg[eo][g * hb:(g + 1) * hb]
                if any(dj is not None for _, dj in tiles):
                    row_parts = []
                    for sub in range(hb // qb):
                        sblk = ((hf * (hb // qb) + sub) * qb) // lanes
                        parts = []
                        for ti, (_, dj) in enumerate(tiles):
                            for c in range(kt // lanes):
                                lane0 = ti * kt + c * lanes
                                part = l[sub * qb:(sub + 1) * qb, lane0:lane0 + lanes]
                                if dj is not None and (kt // lanes) * dj + c - sblk in (-1, 0):
                                    part = part + bias_ref[(kt // lanes) * dj + c - sblk + 1, h]
                                parts.append(part)
                        row_parts.append(jnp.concatenate(parts, axis=1))
                    l = row_parts[0] if len(row_parts) == 1 else jnp.concatenate(row_parts, axis=0)
                l = l + mask_add
                st = slice(((hf * 2 + eo) * n_pairs + g) * hb, ((hf * 2 + eo) * n_pairs + g + 1) * hb)
                m_old = m_ref[st]
                m_new = jnp.maximum(m_old, jnp.max(l, axis=1, keepdims=True))
                alphas.append(jnp.exp2(m_old - m_new))
                ps.append(jnp.exp2(l - jnp.concatenate([m_new] * n_lane_tiles, axis=1)).astype(BF16))
                m_ref[st] = m_new
        a_rows = slice(hf * H_C * hb, (hf + 1) * H_C * hb)
        acc_ref[a_rows] = (acc_ref[a_rows] * jnp.concatenate(alphas, axis=0)
                           + dot_values(jnp.concatenate(ps, axis=0), js))

    def far_pair(k):
        for hf in range(n_half):
            attn_half([(2 * k, None), (2 * k + 1, None)], hf)
    _for_range(0, jb // 2 - 1, far_pair)

    def last_old_pair():
        for hf in range(n_half):
            attn_half([(jb - 2, -2), (jb - 1, -1)], hf)
    if single_step:
        if jb >= 2:
            last_old_pair()
    else:
        pl.when(jb >= 2)(last_old_pair)

    for hf in range(n_half):
        new = [(jb + dj, dj) for dj in range(n_new)
               if ((hf + 1) * hb - qb) // lanes >= (kt // lanes) * dj]
        for i in range(0, len(new), 2):
            attn_half(new[i:i + 2], hf)
        for g in range(n_pairs):
            oe = acc_ref[((hf * 2) * n_pairs + g) * hb:((hf * 2) * n_pairs + g + 1) * hb]
            oo = acc_ref[((hf * 2 + 1) * n_pairs + g) * hb:((hf * 2 + 1) * n_pairs + g + 1) * hb]
            num = jnp.where(low_h, oe, pltpu.roll(oo, HD_C, 1))
            den = jnp.where(low_h, pltpu.roll(oe, HD_C, 1), oo)
            yc_ref[hf * hb:(hf + 1) * hb, g * lanes:(g + 1) * lanes] = num / den
        rs = slice(hf * hb, (hf + 1) * hb)
        y_cat = jnp.concatenate([yc_ref[rs], yd_ref[rs]], axis=1).astype(BF16)
        y_ref[0, rs, :] = x[rs] + _dot(y_cat, wout_ref[...])


def _odd_layer(x, k_hist, v_hist, ki_hist, pool_hist, n_valid_hist, g_norm, w_in, w_pool, s_pool,
               w_out, rel_bias, *, rows):
    b, t, d = x.shape
    hist_len = k_hist.shape[1]
    has_hist = hist_len > 0
    d_c = H_C * HD_C
    d_qi = H_I * D_I
    d_d = d - d_c
    qb = min(2 * CHUNK, rows)
    kt = KEY_TILE
    lanes = V7X_LANES
    s_total = hist_len + t
    top = min(TOPK_MAX, s_total // 4)
    assert t % rows == 0 and rows % qb == 0 and hist_len % kt == 0
    assert rows % kt == 0 or (rows == t and rows == qb and rows <= CHUNK)
    assert hist_len % (2 * kt) == 0 and (rows % (2 * kt) == 0 or rows == t)
    n_tiles = (s_total + kt - 1) // kt
    tiles_per_group = SEARCH_GROUP * lanes // kt
    n_groups = (n_tiles + tiles_per_group - 1) // tiles_per_group
    n_tiles_pad = n_groups * tiles_per_group
    assert rows % SEARCH_ROWS == 0 and rows % min(rows, ATTN_HALF_ROWS) == 0

    offs = np.cumsum([0, d_c, HD_C, HD_C, d_qi, D_I, H_I]).tolist()
    q_w, k_w, v_w, qi_w, ki_w, wi_w = (w_in[:, offs[i]:offs[i + 1]] for i in range(6))
    xd_w = w_in[:, offs[6]:]
    pad_w = jnp.zeros((d, lanes - D_I - H_I), w_in.dtype)
    w_all = jnp.concatenate([q_w, qi_w, xd_w, k_w, v_w, ki_w, wi_w, pad_w], axis=1).astype(BF16)
    wpool_bd = jax.scipy.linalg.block_diag(*[w_pool[g] for g in range(w_pool.shape[0])]).astype(BF16)

    rel = (lanes * jnp.arange(-1, 1, dtype=jnp.int32)[:, None, None]
           + jnp.arange(lanes, dtype=jnp.int32)[None, None, :]
           - jnp.arange(qb, dtype=jnp.int32)[None, :, None])
    bucket = _t5_bucket(rel)
    far_bucket = _t5_bucket(jnp.int32(-2 * lanes))
    rb_shift = rel_bias - rel_bias[far_bucket][None, :]
    tri = (jnp.arange(kt)[:, None] <= jnp.arange(kt)[None, :]).astype(BF16)

    args = [x]
    in_specs = [pl.BlockSpec((1, rows, d), lambda i, j: (i, j, 0))]
    if has_hist:
        assert rows == t, "history tiles are addressed statically"
        args += [jnp.swapaxes(k_hist, 1, 2), jnp.swapaxes(v_hist, 1, 2), jnp.swapaxes(ki_hist, 1, 2)]
        in_specs += [pl.BlockSpec((1, HD_C, hist_len), lambda i, j: (i, 0, 0))] * 2
        in_specs += [pl.BlockSpec((1, D_I, hist_len), lambda i, j: (i, 0, 0))]
    consts = [g_norm.reshape(1, d), w_all, wpool_bd, s_pool.reshape(1, d_d), w_out.astype(BF16), bucket]
    args += [pool_hist] + consts + [rb_shift, tri]
    in_specs += ([pl.BlockSpec((1, POOL_HIST, d_d), lambda i, j: (i, 0, 0))]
                 + [_const_spec(a.shape) for a in consts]
                 + [pl.BlockSpec(memory_space=pltpu.SMEM), _const_spec(tri.shape)])
    kern = functools.partial(_odd_kernel, rows=rows, qb=qb, hist_len=hist_len,
                             n_valid_hist=n_valid_hist, top=top, has_hist=has_hist,
                             single_step=(rows == t))
    return pl.pallas_call(
        kern,
        grid=(b, t // rows),
        in_specs=in_specs,
        out_specs=[
            pl.BlockSpec((1, rows, d), lambda i, j: (i, j, 0)),
            pl.BlockSpec((1, rows, HD_C), lambda i, j: (i, j, 0)),
            pl.BlockSpec((1, rows, HD_C), lambda i, j: (i, j, 0)),
            pl.BlockSpec((1, rows, D_I), lambda i, j: (i, j, 0)),
            pl.BlockSpec((1, POOL_HIST, d_d), lambda i, j: (i, 0, 0)),
        ],
        out_shape=[jax.ShapeDtypeStruct((b, t, d), F32),
                   jax.ShapeDtypeStruct((b, t, HD_C), F32),
                   jax.ShapeDtypeStruct((b, t, HD_C), F32),
                   jax.ShapeDtypeStruct((b, t, D_I), F32),
                   jax.ShapeDtypeStruct((b, POOL_HIST, d_d), F32)],
        scratch_shapes=[
            pltpu.VMEM((n_tiles, kt, lanes), BF16),
            pltpu.VMEM((n_tiles, kt, lanes), BF16),
            pltpu.VMEM((n_tiles, kt, lanes), BF16),
            pltpu.VMEM((n_tiles_pad, rows, kt), jnp.int32),
            pltpu.VMEM((n_groups, SEARCH_GROUP, rows, lanes), jnp.int32),
            pltpu.VMEM((H_C * rows, lanes), F32),
            pltpu.VMEM((H_C * rows, lanes), F32),
            pltpu.VMEM((rows, lanes), F32),
            pltpu.VMEM((rows, lanes), jnp.int32),
            pltpu.VMEM((rows, lanes), F32),
            pltpu.VMEM((3, H_C, qb, lanes), F32),
            pltpu.VMEM((POOL_PAD + rows, d_d), F32),
            pltpu.VMEM((rows, d_c), F32),
            pltpu.VMEM((rows, d_d), F32),
            pltpu.VMEM((H_C // 2 * rows, lanes), BF16),
            pltpu.VMEM((H_C // 2 * rows, lanes), BF16),
            pltpu.VMEM((H_I // 2 * rows, lanes), BF16),
            pltpu.VMEM((H_I // 2 * rows, lanes), BF16),
            pltpu.VMEM((H_I, rows, lanes), F32),
        ] + ([pltpu.VMEM((hist_len // kt, lanes, kt), BF16)] * 3 if has_hist else []),
        compiler_params=pltpu.CompilerParams(
            dimension_semantics=("arbitrary", "arbitrary"),
            vmem_limit_bytes=V7X_VMEM_LIMIT_BYTES),
        name="odd_layer",
    )(*args)


def kernel(x_prompt, x_sample, cache_b_conv, cache_c_k, cache_c_v, cache_c_kidx, cache_d_pool, cache_ffn_conv, ln_mix, ln_ffn, ln_final, e_w_in, e_ln_v_g, e_ln_v_b, e_w_s, e_b_s, e_w_dw, e_b_dw, e_ln_c_g, e_ln_c_b, e_w_out, o_w_in, o_w_pool, o_s_pool, o_w_out, rel_bias, f_w_up, f_w_dw, f_b_dw, f_w_down):
    hp, hs = x_prompt, x_sample
    bp, bs = x_prompt.shape[0], x_sample.shape[0]
    ts = x_sample.shape[1]
    depth = ln_mix.shape[0]
    d_ff = f_w_down.shape[1]
    a_s_l, b_p_l, b_s_l = [], [], []
    ck_p_l, cv_p_l, cki_p_l, ck_s_l, cv_s_l, cki_s_l, d_p_l, d_s_l = [], [], [], [], [], [], [], []
    f_p_l, f_s_l = [], []
    f_w_up_b = f_w_up.astype(BF16)
    f_w_down_b = f_w_down.astype(BF16)
    for layer in range(depth):
        i = layer // 2
        if layer % 2 == 0:
            ew = (ln_mix[layer], e_w_in[i], e_ln_v_g[i], e_ln_v_b[i], e_w_s[i], e_b_s[i], e_w_dw[i],
                  e_b_dw[i], e_ln_c_g[i], e_ln_c_b[i], e_w_out[i])
            hp, b_p = _even_layer(hp, jnp.zeros((bp, K_B - 1, e_w_dw.shape[-1]), F32), *ew,
                                  n_streams=1, rows=PROMPT_ROWS, emit_v=False)
            hs, a_s, b_s = _even_layer(hs, cache_b_conv[i], *ew, n_streams=bs, rows=ts, emit_v=True)
            a_s_l.append(a_s); b_p_l.append(b_p); b_s_l.append(b_s)
        else:
            ow = (ln_mix[layer], o_w_in[i], o_w_pool[i], o_s_pool[i], o_w_out[i], rel_bias)
            d_d = o_w_pool.shape[1] * o_w_pool.shape[2]
            hp, k_p, v_p, ki_p, d_p = _odd_layer(
                hp, jnp.zeros((bp, 0, HD_C), F32), jnp.zeros((bp, 0, HD_C), F32),
                jnp.zeros((bp, 0, D_I), F32), jnp.zeros((bp, POOL_HIST, d_d), F32), 0, *ow,
                rows=PROMPT_ROWS)
            hs, k_s, v_s, ki_s, d_s = _odd_layer(
                hs, cache_c_k[i], cache_c_v[i], cache_c_kidx[i], cache_d_pool[i], POOL_HIST, *ow,
                rows=ts)
            ck_p_l.append(k_p); cv_p_l.append(v_p); cki_p_l.append(ki_p)
            ck_s_l.append(k_s); cv_s_l.append(v_s); cki_s_l.append(ki_s)
            d_p_l.append(d_p); d_s_l.append(d_s)
        g_final = ln_final if layer == depth - 1 else None
        fw = (ln_ffn[layer], f_w_up_b, f_w_dw[layer], f_b_dw[layer], f_w_down_b, g_final, layer)
        hp, f_p = _conv_ffn(hp, jnp.zeros((bp, K_FFN - 1, d_ff), F32), *fw,
                            n_streams=1, rows=PROMPT_ROWS)
        hs, f_s = _conv_ffn(hs, cache_ffn_conv[layer], *fw, n_streams=bs, rows=ts)
        f_p_l.append(f_p); f_s_l.append(f_s)
    return (hp, hs,
            jnp.stack(a_s_l), jnp.stack(b_p_l), jnp.stack(b_s_l),
            jnp.stack(ck_p_l), jnp.stack(cv_p_l), jnp.stack(cki_p_l),
            jnp.stack(ck_s_l), jnp.stack(cv_s_l), jnp.stack(cki_s_l),
            jnp.stack(d_p_l), jnp.stack(d_s_l),
            jnp.stack(f_p_l), jnp.stack(f_s_l))
```
